```python
import math
import jax, jax.numpy as jnp
from jax import lax
import numpy as np

D_MODEL = 1024
BATCH = 2
SEQ = 8192
DEPTH = 1
DEC_BATCH = 128
DEC_SEQ = 4
PAST_LEN = 8192
PAGE_SIZE = 128

HEAD_DIM = 64
ATT_WIDTH = D_MODEL // 2
ATT_HEADS = ATT_WIDTH // HEAD_DIM
SSM_WIDTH = D_MODEL - ATT_WIDTH
SSM_CH = 16
SSM_GROUPS = SSM_WIDTH // SSM_CH
SSM_STATE = 64
DILATIONS = ((128, 1), (512, 4), (2048, 16))
MAX_SPAN = max(w for w, _ in DILATIONS)
Q_BLOCK = 128
FFN_HIDDEN = -(-8 * D_MODEL // (3 * 256)) * 256
N_MOD = 6
EPS = 1e-6

kernel_name = 'hymba_s5_longnet_decode_step'


def rms_norm(x, gain):
    xf = x.astype(jnp.float32)
    return xf * lax.rsqrt(jnp.mean(xf * xf, axis=-1, keepdims=True) + EPS) * gain.astype(jnp.float32)


def ada_modulation(c, w_ada, b_ada):
    mod = jax.nn.silu(c.astype(jnp.float32)) @ w_ada.astype(jnp.float32) + b_ada.astype(jnp.float32)
    return jnp.split(mod[:, None, :], N_MOD, axis=-1)


def combine_dilations(outs, lses):
    w = jax.nn.softmax(jnp.stack(lses, axis=0), axis=0)
    return jnp.einsum('pbth,pbthd->bthd', w, jnp.stack(outs, axis=0))


def dilated_band_attention(q, k, v, dil, n_back):
    b, s, h, dh = q.shape
    n = s // dil
    n_pad = -(-n // Q_BLOCK) * Q_BLOCK
    nb = n_pad // Q_BLOCK
    z = b * dil

    def to_sub(a):
        a = a.reshape(b, n, dil, h, dh).transpose(0, 2, 1, 3, 4).reshape(z, n, h, dh)
        return jnp.pad(a, ((0, 0), (0, n_pad - n), (0, 0), (0, 0)))

    qs, ks, vs = to_sub(q), to_sub(k), to_sub(v)
    front = ((0, 0), (Q_BLOCK, 0), (0, 0), (0, 0))

    def blocks_with_prev(a):
        a = jnp.pad(a, front)
        prev = a[:, :n_pad].reshape(z, nb, Q_BLOCK, h, dh)
        cur = a[:, Q_BLOCK:].reshape(z, nb, Q_BLOCK, h, dh)
        return jnp.concatenate([prev, cur], axis=2)

    kb, vb = blocks_with_prev(ks), blocks_with_prev(vs)
    qb = qs.reshape(z, nb, Q_BLOCK, h, dh)
    scores = jnp.einsum('znqhd,znkhd->znhqk', qb, kb) * (HEAD_DIM ** -0.5)
    qi = jnp.arange(Q_BLOCK)[:, None]
    ki = jnp.arange(2 * Q_BLOCK)[None, :]
    dist = qi + Q_BLOCK - ki
    blk = jnp.arange(nb)[:, None, None]
    valid = (dist >= 0) & (dist <= n_back) & (blk * Q_BLOCK - Q_BLOCK + ki >= 0)
    scores = jnp.where(valid[None, :, None], scores, -jnp.inf)
    lse = jax.nn.logsumexp(scores, axis=-1)
    p = jnp.exp(scores - lse[..., None])
    o = jnp.einsum('znhqk,znkhd->znqhd', p, vb).reshape(z, n_pad, h, dh)[:, :n]
    lse = lse.transpose(0, 1, 3, 2).reshape(z, n_pad, h)[:, :n]
    o = o.reshape(b, dil, n, h, dh).transpose(0, 2, 1, 3, 4).reshape(b, s, h, dh)
    lse = lse.reshape(b, dil, n, h).transpose(0, 2, 1, 3).reshape(b, s, h)
    return o, lse


def dilated_gather_attention(q, k_all, v_all, n_past):
    t = q.shape[1]
    outs, lses = [], []
    for window, dil in DILATIONS:
        n_back = window // dil
        idx = n_past + jnp.arange(t)[:, None] - dil * jnp.arange(n_back + 1)[None, :]
        valid = idx >= 0
        idx = jnp.maximum(idx, 0)
        kg, vg = k_all[:, idx], v_all[:, idx]
        scores = jnp.einsum('bthd,btmhd->bthm', q, kg) * (HEAD_DIM ** -0.5)
        scores = jnp.where(valid[None, :, None, :], scores, -jnp.inf)
        lse = jax.nn.logsumexp(scores, axis=-1)
        p = jnp.exp(scores - lse[..., None])
        outs.append(jnp.einsum('bthm,btmhd->bthd', p, vg))
        lses.append(lse)
    return combine_dilations(outs, lses)


def attend_prompt(q, k, v):
    outs, lses = [], []
    for window, dil in DILATIONS:
        o, l = dilated_band_attention(q, k, v, dil, window // dil)
        outs.append(o)
        lses.append(l)
    keep = min(MAX_SPAN, q.shape[1])
    return combine_dilations(outs, lses), k[:, -keep:], v[:, -keep:]


def make_sample_attend(cache_k, cache_v):
    def attend(q, k, v):
        n_past, t = cache_k.shape[1], q.shape[1]
        k_all = jnp.concatenate([cache_k.astype(jnp.float32), k], axis=1)
        v_all = jnp.concatenate([cache_v.astype(jnp.float32), v], axis=1)
        o = dilated_gather_attention(q, k_all, v_all, n_past)
        return o, k_all[:, t:], v_all[:, t:]
    return attend


def _linear_combine(left, right):
    a_l, b_l = left
    a_r, b_r = right
    return a_r * a_l, a_r * b_l + b_r


def s5_mixer(u, h0, a_re, a_im, log_dt, b_re, b_im, c_re, c_im, d_skip, w_glu):
    f32 = jnp.float32
    bsz, t, _ = u.shape
    u = u.astype(f32)
    a = lax.complex(a_re.astype(f32), a_im.astype(f32))
    dt = jnp.exp(log_dt.astype(f32))[:, None]
    a_bar = jnp.exp(dt * a)
    b_bar = ((a_bar - 1.0) / a)[..., None] * lax.complex(b_re.astype(f32), b_im.astype(f32))
    c_mat = lax.complex(c_re.astype(f32), c_im.astype(f32))
    ug = u.reshape(bsz, t, SSM_GROUPS, SSM_CH).astype(jnp.complex64)
    bu = jnp.einsum('btgc,gnc->btgn', ug, b_bar)
    if h0 is not None:
        bu = bu.at[:, 0].add(a_bar * h0)
    a_seq = jnp.broadcast_to(a_bar, bu.shape)
    _, h = lax.associative_scan(_linear_combine, (a_seq, bu), axis=1)
    y = jnp.einsum('gcn,btgn->btgc', c_mat, h).real.reshape(bsz, t, SSM_WIDTH) + d_skip.astype(f32) * u
    y = jax.nn.gelu(y)
    y = y * jax.nn.sigmoid(y @ w_glu.astype(f32))
    return y, h[:, -1]


def decoder_layer(x, c, attend, h0, norm1_g, norm2_g, w_ada, b_ada, w_in, q_gain, k_gain,
                  ssm_a_re, ssm_a_im, ssm_log_dt, ssm_b_re, ssm_b_im, ssm_c_re, ssm_c_im, ssm_d, w_glu,
                  attn_out_g, ssm_out_g, w_out, w_gate, w_up, w_down):
    f32 = jnp.float32
    bsz, t, _ = x.shape
    shift1, scale1, gate1, shift2, scale2, gate2 = ada_modulation(c, w_ada, b_ada)
    h = rms_norm(x, norm1_g) * (1.0 + scale1) + shift1
    zcols = h @ w_in.astype(f32)
    q = zcols[..., :ATT_WIDTH].reshape(bsz, t, ATT_HEADS, HEAD_DIM)
    k = zcols[..., ATT_WIDTH:2 * ATT_WIDTH].reshape(bsz, t, ATT_HEADS, HEAD_DIM)
    v = zcols[..., 2 * ATT_WIDTH:3 * ATT_WIDTH].reshape(bsz, t, ATT_HEADS, HEAD_DIM)
    u = zcols[..., 3 * ATT_WIDTH:]
    q = rms_norm(q, q_gain)
    k = rms_norm(k, k_gain)
    o_att, k_keep, v_keep = attend(q, k, v)
    y_ssm, h_last = s5_mixer(u, h0, ssm_a_re, ssm_a_im, ssm_log_dt, ssm_b_re, ssm_b_im,
                             ssm_c_re, ssm_c_im, ssm_d, w_glu)
    merged = jnp.concatenate([rms_norm(o_att.reshape(bsz, t, ATT_WIDTH), attn_out_g),
                              rms_norm(y_ssm, ssm_out_g)], axis=-1)
    x1 = x.astype(f32) + gate1 * (merged @ w_out.astype(f32))
    h2 = rms_norm(x1, norm2_g) * (1.0 + scale2) + shift2
    ff = (jax.nn.silu(h2 @ w_gate.astype(f32)) * (h2 @ w_up.astype(f32))) @ w_down.astype(f32)
    y = x1 + gate2 * ff
    return y.astype(x.dtype), k_keep, v_keep, h_last


def setup_inputs(seed: int = 0) -> dict:
    key = jax.random.key(seed)
    ks = jax.random.split(key, 32)
    f32 = jnp.float32
    L = DEPTH
    n_buf = min(MAX_SPAN, PAST_LEN)

    def nrm(k, shape, scale):
        return jax.random.normal(k, shape, f32) * scale

    n_idx = jnp.arange(SSM_STATE, dtype=f32)
    return {
        'x_prompt': nrm(ks[0], (BATCH, SEQ, D_MODEL), 1.0),
        'x_sample': nrm(ks[1], (DEC_BATCH, DEC_SEQ, D_MODEL), 1.0),
        'cache_k': nrm(ks[2], (L, DEC_BATCH, n_buf, ATT_HEADS, HEAD_DIM), 1.0),
        'cache_v': nrm(ks[3], (L, DEC_BATCH, n_buf, ATT_HEADS, HEAD_DIM), 1.0),
        'state_ssm_re': nrm(ks[4], (L, DEC_BATCH, SSM_GROUPS, SSM_STATE), 0.3),
        'state_ssm_im': nrm(ks[5], (L, DEC_BATCH, SSM_GROUPS, SSM_STATE), 0.3),
        'c_prompt': nrm(ks[6], (BATCH, D_MODEL), 1.0),
        'c_sample': nrm(ks[7], (DEC_BATCH, D_MODEL), 1.0),
        'norm1_g': 1.0 + nrm(ks[8], (L, D_MODEL), 0.02),
        'norm2_g': 1.0 + nrm(ks[9], (L, D_MODEL), 0.02),
        'w_ada': nrm(ks[10], (L, D_MODEL, N_MOD * D_MODEL), 0.5 * D_MODEL ** -0.5),
        'b_ada': nrm(ks[11], (L, N_MOD * D_MODEL), 0.02),
        'w_in': nrm(ks[12], (L, D_MODEL, 3 * ATT_WIDTH + SSM_WIDTH), D_MODEL ** -0.5),
        'q_gain': 1.0 + nrm(ks[13], (L, HEAD_DIM), 0.02),
        'k_gain': 1.0 + nrm(ks[14], (L, HEAD_DIM), 0.02),
        'ssm_a_re': -0.5 + nrm(ks[15], (L, SSM_GROUPS, SSM_STATE), 0.01),
        'ssm_a_im': math.pi * n_idx + nrm(ks[16], (L, SSM_GROUPS, SSM_STATE), 0.01),
        'ssm_log_dt': jax.random.uniform(ks[17], (L, SSM_GROUPS), f32, math.log(1e-3), math.log(1e-1)),
        'ssm_b_re': nrm(ks[18], (L, SSM_GROUPS, SSM_STATE, SSM_CH), (2 * SSM_CH) ** -0.5),
        'ssm_b_im': nrm(ks[19], (L, SSM_GROUPS, SSM_STATE, SSM_CH), (2 * SSM_CH) ** -0.5),
        'ssm_c_re': nrm(ks[20], (L, SSM_GROUPS, SSM_CH, SSM_STATE), (2 * SSM_STATE) ** -0.5),
        'ssm_c_im': nrm(ks[21], (L, SSM_GROUPS, SSM_CH, SSM_STATE), (2 * SSM_STATE) ** -0.5),
        'ssm_d': nrm(ks[22], (L, SSM_WIDTH), 1.0),
        'w_glu': nrm(ks[23], (L, SSM_WIDTH, SSM_WIDTH), SSM_WIDTH ** -0.5),
        'attn_out_g': 1.0 + nrm(ks[24], (L, ATT_WIDTH), 0.02),
        'ssm_out_g': 1.0 + nrm(ks[25], (L, SSM_WIDTH), 0.02),
        'w_out': nrm(ks[26], (L, D_MODEL, D_MODEL), D_MODEL ** -0.5),
        'w_gate': nrm(ks[27], (L, D_MODEL, FFN_HIDDEN), D_MODEL ** -0.5),
        'w_up': nrm(ks[28], (L, D_MODEL, FFN_HIDDEN), D_MODEL ** -0.5),
        'w_down': nrm(ks[29], (L, FFN_HIDDEN, D_MODEL), FFN_HIDDEN ** -0.5),
    }


def reference(x_prompt, x_sample, cache_k, cache_v, state_ssm_re, state_ssm_im, c_prompt, c_sample,
              norm1_g, norm2_g, w_ada, b_ada, w_in, q_gain, k_gain,
              ssm_a_re, ssm_a_im, ssm_log_dt, ssm_b_re, ssm_b_im, ssm_c_re, ssm_c_im, ssm_d, w_glu,
              attn_out_g, ssm_out_g, w_out, w_gate, w_up, w_down):
    f32 = jnp.float32
    xp, xs = x_prompt, x_sample
    kp_l, vp_l, hp_l, ks_l, vs_l, hs_l = [], [], [], [], [], []
    for layer in range(DEPTH):
        w = tuple(p[layer] for p in (norm1_g, norm2_g, w_ada, b_ada, w_in, q_gain, k_gain,
                                     ssm_a_re, ssm_a_im, ssm_log_dt, ssm_b_re, ssm_b_im, ssm_c_re, ssm_c_im,
                                     ssm_d, w_glu, attn_out_g, ssm_out_g, w_out, w_gate, w_up, w_down))
        xp, kp, vp, hp = decoder_layer(xp, c_prompt, attend_prompt, None, *w)
        h0 = lax.complex(state_ssm_re[layer].astype(f32), state_ssm_im[layer].astype(f32))
        xs, ks_, vs_, hs = decoder_layer(xs, c_sample, make_sample_attend(cache_k[layer], cache_v[layer]), h0, *w)
        kp_l.append(kp); vp_l.append(vp); hp_l.append(hp)
        ks_l.append(ks_); vs_l.append(vs_); hs_l.append(hs)
    hp_all = jnp.stack(hp_l, axis=0)
    hs_all = jnp.stack(hs_l, axis=0)
    return (xp, xs, jnp.stack(kp_l, axis=0), jnp.stack(vp_l, axis=0), hp_all.real, hp_all.imag,
            jnp.stack(ks_l, axis=0), jnp.stack(vs_l, axis=0), hs_all.real, hs_all.imag)
```

```python
import functools
import math

import jax
import jax.numpy as jnp
import numpy as np
from jax import lax
from jax.experimental import pallas as pl
from jax.experimental.pallas import tpu as pltpu

F32 = jnp.float32
BF16 = jnp.bfloat16

D_MODEL = 1024
HEAD_DIM = 64
ATT_WIDTH = 512
ATT_HEADS = 8
SSM_WIDTH = 512
SSM_CH = 16
SSM_GROUPS = 32
SSM_STATE = 64
DILATIONS = ((128, 1), (512, 4), (2048, 16))
N_BACK = 128
MAX_SPAN = 2048
FFN_HIDDEN = 2816
N_MOD = 6
EPS = 1e-6

LANES = 128
SUBLANES = 8
VMEM_LIMIT = 56 * 1024 * 1024

TOKEN_TILE = 512
SUPER = 2048
QB = 128
CHUNK = 16
NEG = -1e30


def _cparams(sem=None):
    return pltpu.CompilerParams(dimension_semantics=sem, vmem_limit_bytes=VMEM_LIMIT)


def _const_spec(shape):
    nd = len(shape)
    return pl.BlockSpec(shape, lambda *_: (0,) * nd, pipeline_mode=pl.Buffered(1))


def _sigmoid(x):
    return 1.0 / (1.0 + jnp.exp(-x))


def _split_bf16(a):
    hi = a.astype(BF16)
    lo = (a - hi.astype(F32)).astype(BF16)
    return hi, lo


def _ada_kernel(c_ref, w_ref, b_ref, o_ref):
    c = c_ref[...]
    a = c * _sigmoid(c)
    a_hi, a_lo = _split_bf16(a)
    w_hi, w_lo = _split_bf16(w_ref[...])
    acc = jnp.dot(a_hi, w_hi, preferred_element_type=F32)
    acc += jnp.dot(a_hi, w_lo, preferred_element_type=F32)
    acc += jnp.dot(a_lo, w_hi, preferred_element_type=F32)
    o_ref[...] = acc + b_ref[...]


def _ada(c_all, w_ada, b_ada):
    rows = c_all.shape[0]
    n = w_ada.shape[1]
    tn = 1024
    return pl.pallas_call(
        _ada_kernel,
        grid=(n // tn,),
        in_specs=[pl.BlockSpec((rows, D_MODEL), lambda j: (0, 0)),
                  pl.BlockSpec((D_MODEL, tn), lambda j: (0, j)),
                  pl.BlockSpec((1, tn), lambda j: (0, j))],
        out_specs=pl.BlockSpec((rows, tn), lambda j: (0, j)),
        out_shape=jax.ShapeDtypeStruct((rows, n), F32),
        compiler_params=_cparams(("arbitrary",)),
        name="ada",
    )(c_all, w_ada, b_ada)


def _head_rms(z, gain):
    lane = lax.broadcasted_iota(jnp.int32, (1, LANES), 1)
    lo = lane < HEAD_DIM
    outs = []
    for c in range(z.shape[1] // LANES):
        blk = z[:, c * LANES:(c + 1) * LANES]
        sq = blk * blk
        s_lo = jnp.sum(jnp.where(lo, sq, 0.0), axis=-1, keepdims=True)
        s_hi = jnp.sum(jnp.where(lo, 0.0, sq), axis=-1, keepdims=True)
        inv = jnp.where(lo, lax.rsqrt(s_lo * (1.0 / HEAD_DIM) + EPS), lax.rsqrt(s_hi * (1.0 / HEAD_DIM) + EPS))
        outs.append(blk * inv)
    return jnp.concatenate(outs, axis=1) * gain


def _inproj_kernel(x_ref, scale_ref, shift_ref, g_ref, w_ref, qg_ref, kg_ref, *out_refs, kt_first):
    q_ref, k_ref, v_ref, u_ref = out_refs[:4]
    x = x_ref[...]
    ms = jnp.mean(x * x, axis=-1, keepdims=True)
    h = x * lax.rsqrt(ms + EPS) * g_ref[...]
    h = h * (1.0 + scale_ref[...]) + shift_ref[...]
    z = jnp.dot(h.astype(BF16), w_ref[...], preferred_element_type=F32)
    q = _head_rms(z[:, 0:ATT_WIDTH], qg_ref[...]) * (HEAD_DIM ** -0.5)
    k = _head_rms(z[:, ATT_WIDTH:2 * ATT_WIDTH], kg_ref[...])
    v = z[:, 2 * ATT_WIDTH:3 * ATT_WIDTH]
    q_ref[...] = q
    k_ref[...] = k
    v_ref[...] = v
    u_ref[...] = z[:, 3 * ATT_WIDTH:]
    if kt_first is not None:
        kt_ref, vt_ref = out_refs[4:]

        @pl.when(pl.program_id(1) >= kt_first)
        def _():
            kt_ref[...] = k.T
            vt_ref[...] = v.T


def _inproj_prompt(x, scale, shift, g, w_bf, qg, kg):
    b, s, _ = x.shape
    tm = TOKEN_TILE
    nt = s // tm
    keep = min(MAX_SPAN, s)
    kt_first = nt - keep // tm
    tok = pl.BlockSpec((None, tm, D_MODEL), lambda bi, i: (bi, i, 0))
    mod = pl.BlockSpec((None, 1, D_MODEL), lambda bi, i: (bi, 0, 0))
    out_tok = pl.BlockSpec((None, tm, ATT_WIDTH), lambda bi, i: (bi, i, 0))
    out_t = pl.BlockSpec((None, ATT_WIDTH, tm), lambda bi, i: (bi, 0, jnp.maximum(i - kt_first, 0)))
    sd = jax.ShapeDtypeStruct((b, s, ATT_WIDTH), F32)
    sdt = jax.ShapeDtypeStruct((b, ATT_WIDTH, keep), F32)
    return pl.pallas_call(
        functools.partial(_inproj_kernel, kt_first=kt_first),
        grid=(b, nt),
        in_specs=[tok, mod, mod, _const_spec((1, D_MODEL)), _const_spec((D_MODEL, 4 * ATT_WIDTH)),
                  _const_spec((1, ATT_WIDTH)), _const_spec((1, ATT_WIDTH))],
        out_specs=[out_tok] * 4 + [out_t] * 2,
        out_shape=[sd] * 4 + [sdt] * 2,
        compiler_params=_cparams(("arbitrary", "arbitrary")),
        name="inproj_prompt",
    )(x, scale, shift, g, w_bf, qg, kg)


def _inproj_sample(x, scale, shift, g, w_bf, qg, kg):
    t = x.shape[0]
    tm = min(TOKEN_TILE, t)
    tok = pl.BlockSpec((tm, D_MODEL), lambda i: (i, 0))
    out_tok = pl.BlockSpec((tm, ATT_WIDTH), lambda i: (i, 0))
    sd = jax.ShapeDtypeStruct((t, ATT_WIDTH), F32)
    return pl.pallas_call(
        functools.partial(_inproj_kernel, kt_first=None),
        grid=(t // tm,),
        in_specs=[tok, tok, tok, _const_spec((1, D_MODEL)), _const_spec((D_MODEL, 4 * ATT_WIDTH)),
                  _const_spec((1, ATT_WIDTH)), _const_spec((1, ATT_WIDTH))],
        out_specs=[out_tok] * 4,
        out_shape=[sd] * 4,
        compiler_params=_cparams(("arbitrary",)),
        name="inproj_sample",
    )(x, scale, shift, g, w_bf, qg, kg)


def _attn_kernel(q_ref, k_ref, v_ref, bias_ref, o_ref, acc_ref, m_ref, l_ref):
    s_len = q_ref.shape[0]
    lane = lax.broadcasted_iota(jnp.int32, (1, LANES), 1)
    lo = lane < HEAD_DIM
    nt_contract = (((1,), (1,)), ((), ()))

    def super_body(st, carry):
        t0 = st * SUPER
        for p, (_, d) in enumerate(DILATIONS):
            per_res = SUPER // d // QB

            def tile_body(ti, c, p=p, d=d, per_res=per_res):
                r = ti // per_res
                jt = ti % per_res
                row0 = r + d * QB * jt
                qs = t0 + row0
                first = jnp.logical_and(st == 0, jt == 0)
                ks = jnp.where(first, qs, qs - d * QB)
                bias = bias_ref[jnp.where(first, 0, 1)]
                q = q_ref[pl.ds(qs, QB, stride=d), :]
                k = k_ref[pl.ds(ks, 2 * QB, stride=d), :].astype(BF16)
                v = v_ref[pl.ds(ks, 2 * QB, stride=d), :]
                acc = None
                ms = []
                ls = []
                for half in (lo, jnp.logical_not(lo)):
                    qh = jnp.where(half, q, 0.0).astype(BF16)
                    vh = jnp.where(half, v, 0.0).astype(BF16)
                    s = lax.dot_general(qh, k, nt_contract, preferred_element_type=F32) + bias
                    m = jnp.max(s, axis=-1, keepdims=True)
                    pr = jnp.exp(s - m)
                    ls.append(jnp.sum(pr, axis=-1, keepdims=True))
                    ms.append(m)
                    pv = jnp.dot(pr.astype(BF16), vh, preferred_element_type=F32)
                    acc = pv if acc is None else acc + pv
                rows = pl.ds(row0, QB, stride=d)
                acc_ref[p, rows, :] = acc
                m_ref[p, rows, :] = jnp.where(lo, ms[0], ms[1])
                l_ref[p, rows, :] = jnp.where(lo, ls[0], ls[1])
                return c

            lax.fori_loop(0, SUPER // QB, tile_body, 0)

        def merge_body(ci, c):
            rows = pl.ds(pl.multiple_of(ci * QB, QB), QB)
            m1, m2, m3 = m_ref[0, rows, :], m_ref[1, rows, :], m_ref[2, rows, :]
            mm = jnp.maximum(jnp.maximum(m1, m2), m3)
            w1, w2, w3 = jnp.exp(m1 - mm), jnp.exp(m2 - mm), jnp.exp(m3 - mm)
            num = w1 * acc_ref[0, rows, :] + w2 * acc_ref[1, rows, :] + w3 * acc_ref[2, rows, :]
            den = w1 * l_ref[0, rows, :] + w2 * l_ref[1, rows, :] + w3 * l_ref[2, rows, :]
            o_ref[pl.ds(pl.multiple_of(t0 + ci * QB, QB), QB), :] = num / den
            return c

        lax.fori_loop(0, SUPER // QB, merge_body, 0)
        return carry

    lax.fori_loop(0, s_len // SUPER, super_body, 0)


def _attn_bias():
    qi = np.arange(QB)[:, None]
    ki = np.arange(2 * QB)[None, :]
    first = (ki <= qi)
    dist = qi + QB - ki
    normal = (dist >= 0) & (dist <= N_BACK)
    return jnp.asarray(np.where(np.stack([first, normal]), 0.0, NEG), F32)


def _attn_prompt(q, k, v):
    b, s, _ = q.shape
    assert s % SUPER == 0 and s >= 2 * SUPER
    blk = pl.BlockSpec((None, s, LANES), lambda bi, hp: (bi, 0, hp))
    return pl.pallas_call(
        _attn_kernel,
        grid=(b, ATT_WIDTH // LANES),
        in_specs=[blk, blk, blk, _const_spec((2, QB, 2 * QB))],
        out_specs=blk,
        out_shape=jax.ShapeDtypeStruct((b, s, ATT_WIDTH), F32),
        scratch_shapes=[pltpu.VMEM((3, SUPER, LANES), F32)] * 3,
        compiler_params=_cparams(("arbitrary", "arbitrary")),
        name="attn_prompt",
    )(q, k, v, _attn_bias())


def _sa_kernel(kt_ref, vt_ref, qx_ref, kn_ref, vn_ref, w_ref, hm_ref, okt_ref, ovt_ref, o_ref, pad_ref):
    n_buf = kt_ref.shape[1]
    t_new = kn_ref.shape[0]
    ext = n_buf + LANES

    def extend(buf_ref, new_ref):
        pad_ref[...] = jnp.zeros_like(pad_ref)
        pad_ref[0:t_new, :] = new_ref[...]
        return jnp.concatenate([buf_ref[...], pad_ref[...].T], axis=1)

    k_ext = extend(kt_ref, kn_ref)
    okt_ref[...] = pltpu.roll(k_ext, ext - t_new, axis=1)[:, 0:n_buf]
    v_ext = extend(vt_ref, vn_ref)
    ovt_ref[...] = pltpu.roll(v_ext, ext - t_new, axis=1)[:, 0:n_buf]

    w = w_ref[...]
    s = jnp.dot(qx_ref[...], k_ext.astype(BF16), preferred_element_type=F32)
    s = jnp.where(w > 0.0, s, NEG)
    m = jnp.max(s, axis=-1, keepdims=True)
    e = w * jnp.exp(s - m)
    l = jnp.sum(e, axis=-1, keepdims=True)
    pv = lax.dot_general(e.astype(BF16), v_ext.astype(BF16), (((1,), (1,)), ((), ())),
                         preferred_element_type=F32)
    pv = pv * hm_ref[...] / l
    o_ref[...] = jnp.sum(pv.reshape(t_new, ATT_HEADS, ATT_WIDTH), axis=1)


def _sa_weights(t_new, n_buf):
    ext = n_buf + LANES
    pos = np.arange(ext)[None, :]
    i = np.arange(t_new)[:, None]
    delta = n_buf + i - pos
    is_real = (pos < n_buf + t_new)
    w = np.zeros((t_new, ext), np.float32)
    for window, d in DILATIONS:
        w += ((delta >= 0) & (delta % d == 0) & (delta <= window) & is_real)
    return jnp.asarray(np.repeat(w, ATT_HEADS, axis=0), F32)


def _head_mask():
    h = np.arange(ATT_HEADS)[:, None]
    c = np.arange(ATT_WIDTH)[None, :] // HEAD_DIM
    return (h == c).astype(np.float32)


def _sample_attention(kt, vt, q, k_new, v_new):
    bd, _, n_buf = kt.shape
    t_new = q.shape[1]
    assert n_buf == MAX_SPAN, "window buffer must cover the longest pattern (no invalid positions)"
    rows = t_new * ATT_HEADS
    hm = jnp.asarray(np.tile(_head_mask(), (t_new, 1)), F32)
    qx = (jnp.repeat(q, ATT_HEADS, axis=1) * hm[None]).astype(BF16)
    buf = pl.BlockSpec((None, ATT_WIDTH, n_buf), lambda b: (b, 0, 0))
    new = pl.BlockSpec((None, t_new, ATT_WIDTH), lambda b: (b, 0, 0))
    sd_buf = jax.ShapeDtypeStruct((bd, ATT_WIDTH, n_buf), F32)
    return pl.pallas_call(
        _sa_kernel,
        grid=(bd,),
        in_specs=[buf, buf, pl.BlockSpec((None, rows, ATT_WIDTH), lambda b: (b, 0, 0)), new, new,
                  _const_spec((rows, n_buf + LANES)), _const_spec((rows, ATT_WIDTH))],
        out_specs=[buf, buf, new],
        out_shape=[sd_buf, sd_buf, jax.ShapeDtypeStruct((bd, t_new, ATT_WIDTH), F32)],
        scratch_shapes=[pltpu.VMEM((LANES, ATT_WIDTH), F32)],
        compiler_params=_cparams(("arbitrary",)),
        name="sample_attn",
    )(kt, vt, qx, k_new, v_new, _sa_weights(t_new, n_buf), hm)


def _s5_tables(a_re, a_im, log_dt, b_re, b_im, c_re, c_im, d_skip):
    g, n = a_re.shape
    dt = jnp.exp(log_dt)[:, None]
    x, y = dt * a_re, dt * a_im
    ks = jnp.arange(CHUNK + 1, dtype=F32)[:, None, None]
    mag = jnp.exp(ks * x[None])
    pw_re, pw_im = mag * jnp.cos(ks * y[None]), mag * jnp.sin(ks * y[None])
    e_re = jnp.expm1(x) * jnp.cos(y) - 2.0 * jnp.sin(0.5 * y) ** 2
    e_im = jnp.exp(x) * jnp.sin(y)
    den = a_re * a_re + a_im * a_im
    f_re = (e_re * a_re + e_im * a_im) / den
    f_im = (e_im * a_re - e_re * a_im) / den
    bb_re = f_re[..., None] * b_re - f_im[..., None] * b_im
    bb_im = f_re[..., None] * b_im + f_im[..., None] * b_re
    return dict(pw_re=pw_re, pw_im=pw_im, bb_re=bb_re, bb_im=bb_im)


def _s5_chunk_operators(tb, c_re, c_im, d_skip):
    pw_re, pw_im, bb_re, bb_im = tb["pw_re"], tb["pw_im"], tb["bb_re"], tb["bb_im"]
    g, n, ch = bb_re.shape
    L = CHUNK
    hp = lax.Precision.HIGHEST
    ca_re = c_re[None] * pw_re[:, :, None, :] - c_im[None] * pw_im[:, :, None, :]
    ca_im = c_re[None] * pw_im[:, :, None, :] + c_im[None] * pw_re[:, :, None, :]
    kern = (jnp.einsum("kgcn,gnd->kgcd", ca_re[:L], bb_re, precision=hp)
            - jnp.einsum("kgcn,gnd->kgcd", ca_im[:L], bb_im, precision=hp))
    kern = kern.at[0].add(jnp.eye(ch, dtype=F32)[None] * d_skip.reshape(g, ch)[:, :, None])
    s_idx = jnp.arange(L)[:, None]
    t_idx = jnp.arange(L)[None, :]
    lag = t_idx - s_idx
    toe = jnp.where((lag >= 0)[:, :, None, None, None], kern[jnp.clip(lag, 0, L - 1)], 0.0)
    toe = toe.transpose(2, 0, 4, 1, 3).reshape(g, L * ch, L * ch)
    rev = L - 1 - jnp.arange(L)
    ws_re = pw_re[rev][:, :, :, None] * bb_re[None] - pw_im[rev][:, :, :, None] * bb_im[None]
    ws_im = pw_re[rev][:, :, :, None] * bb_im[None] + pw_im[rev][:, :, :, None] * bb_re[None]
    ws_re = ws_re.transpose(1, 0, 3, 2).reshape(g, L * ch, n)
    ws_im = ws_im.transpose(1, 0, 3, 2).reshape(g, L * ch, n)
    vr = ca_re[1:].transpose(1, 3, 0, 2).reshape(g, n, L * ch)
    vi = (-ca_im[1:]).transpose(1, 3, 0, 2).reshape(g, n, L * ch)
    gp = g // 2
    z_w = jnp.zeros((gp, L * ch, n), F32)
    wr, wi = ws_re.reshape(gp, 2, L * ch, n), ws_im.reshape(gp, 2, L * ch, n)
    w_pair = jnp.concatenate([
        jnp.concatenate([wr[:, 0], z_w, wi[:, 0], z_w], axis=2),
        jnp.concatenate([z_w, wr[:, 1], z_w, wi[:, 1]], axis=2)], axis=1)
    z_v = jnp.zeros((gp, n, L * ch), F32)
    vr, vi = vr.reshape(gp, 2, n, L * ch), vi.reshape(gp, 2, n, L * ch)
    v_pair = jnp.concatenate([
        jnp.concatenate([vr[:, 0], z_v], axis=2), jnp.concatenate([z_v, vr[:, 1]], axis=2),
        jnp.concatenate([vi[:, 0], z_v], axis=2), jnp.concatenate([z_v, vi[:, 1]], axis=2)], axis=1)
    a16_re = pw_re[L].reshape(1, g * n)
    a16_im = pw_im[L].reshape(1, g * n)
    return toe.astype(BF16), w_pair.astype(BF16), v_pair.astype(BF16), a16_re, a16_im


GROUPS_PER_BLOCK = LANES // SSM_CH
REGROUP_ROWS = 32


def _granule_id():
    return lax.broadcasted_iota(jnp.int32, (1, LANES), 1) // SSM_CH


def _ssm_a_kernel(u_ref, w_ref, xg_ref, sre_ref, sim_ref, xs_ref):
    nb, s_len, _ = u_ref.shape
    n_chunks = s_len // CHUNK
    gran = _granule_id()
    half = CHUNK // 2

    for b in range(nb):
        def regroup(ct, c, b=b):
            c0 = pl.multiple_of(ct * REGROUP_ROWS, REGROUP_ROWS)
            rows = pl.ds(b * n_chunks + c0, REGROUP_ROWS)
            for j in range(2):
                z = [u_ref[b, pl.ds(c0 * CHUNK + half * j + tt, REGROUP_ROWS, stride=CHUNK), :]
                     for tt in range(half)]
                for p in range(GROUPS_PER_BLOCK):
                    acc = z[p]
                    for tt in range(half):
                        if tt != p:
                            rolled = pltpu.roll(z[tt], SSM_CH * ((tt - p) % GROUPS_PER_BLOCK), axis=1)
                            acc = jnp.where(gran == tt, rolled, acc)
                    xs_ref[2 * p + j, rows, :] = acc
            return c
        lax.fori_loop(0, n_chunks // REGROUP_ROWS, regroup, 0)

    for col in range(2 * GROUPS_PER_BLOCK):
        xg_ref[:, col * LANES:(col + 1) * LANES] = xs_ref[col].astype(BF16)
    for pp in range(GROUPS_PER_BLOCK // 2):
        xp = xg_ref[:, pp * 4 * LANES:(pp + 1) * 4 * LANES]
        st = jnp.dot(xp, w_ref[pp], preferred_element_type=F32)
        sre_ref[pp] = st[:, 0:LANES]
        sim_ref[pp] = st[:, LANES:2 * LANES]


def _ssm_scan_kernel(sre_ref, sim_ref, are_ref, aim_ref, hre_ref, him_ref, fre_ref, fim_ref, *, nb):
    ncb, rows, _ = sre_ref.shape
    n_chunks = rows // nb
    a_re = [are_ref[cb] for cb in range(ncb)]
    a_im = [aim_ref[cb] for cb in range(ncb)]

    def body(c, carry):
        r = pl.ds(c, nb, stride=n_chunks)
        out = []
        for cb in range(ncb):
            h_re, h_im = carry[2 * cb], carry[2 * cb + 1]
            hre_ref[cb, r, :] = h_re
            him_ref[cb, r, :] = h_im
            out.append(a_re[cb] * h_re - a_im[cb] * h_im + sre_ref[cb, r, :])
            out.append(a_re[cb] * h_im + a_im[cb] * h_re + sim_ref[cb, r, :])
        return tuple(out)

    z = jnp.zeros((nb, LANES), F32)
    fin = lax.fori_loop(0, n_chunks, body, (z,) * (2 * ncb), unroll=4)
    for cb in range(ncb):
        fre_ref[cb] = fin[2 * cb]
        fim_ref[cb] = fin[2 * cb + 1]


def _ssm_c_kernel(xg_ref, hre_ref, him_ref, t_ref, v_ref, y_ref, ys_ref):
    nb, s_len, _ = y_ref.shape
    n_chunks = s_len // CHUNK
    gran = _granule_id()
    half = CHUNK // 2
    for pp in range(GROUPS_PER_BLOCK // 2):
        hcat = jnp.concatenate([hre_ref[pp], him_ref[pp]], axis=1).astype(BF16)
        inter = jnp.dot(hcat, v_ref[pp], preferred_element_type=F32)
        for e in range(2):
            g = 2 * pp + e
            intra = jnp.dot(xg_ref[:, g * 2 * LANES:(g + 1) * 2 * LANES], t_ref[g],
                            preferred_element_type=F32)
            yg = intra + inter[:, e * 2 * LANES:(e + 1) * 2 * LANES]
            ys_ref[2 * g] = yg[:, 0:LANES]
            ys_ref[2 * g + 1] = yg[:, LANES:2 * LANES]

    for b in range(nb):
        def regroup(ct, c, b=b):
            c0 = pl.multiple_of(ct * REGROUP_ROWS, REGROUP_ROWS)
            rows = pl.ds(b * n_chunks + c0, REGROUP_ROWS)
            for j in range(2):
                yp =[ys_ref[2 * p + j, rows, :] for p in range(GROUPS_PER_BLOCK)]
                for tt in range(half):
                    acc = yp[tt]
                    for p in range(GROUPS_PER_BLOCK):
                        if p != tt:
                            rolled = pltpu.roll(yp[p], SSM_CH * ((p - tt) % GROUPS_PER_BLOCK), axis=1)
                            acc = jnp.where(gran == p, rolled, acc)
                    y_ref[b, pl.ds(c0 * CHUNK + half * j + tt, REGROUP_ROWS, stride=CHUNK), :] = acc
            return c
        lax.fori_loop(0, n_chunks // REGROUP_ROWS, regroup, 0)


def _ssm_prompt(u, toe, w_pair, v_pair, a16_re, a16_im):
    b, s, _ = u.shape
    rows = b * (s // CHUNK)
    nblk = SSM_WIDTH // LANES
    gw = GROUPS_PER_BLOCK * CHUNK * SSM_CH
    ncb = GROUPS_PER_BLOCK * SSM_STATE // LANES
    ublk = pl.BlockSpec((b, s, LANES), lambda j: (0, 0, j))
    sblk = pl.BlockSpec((ncb, rows, LANES), lambda j: (j, 0, 0))
    sd_h = jax.ShapeDtypeStruct((nblk * ncb, rows, LANES), F32)
    xg, s_re, s_im = pl.pallas_call(
        _ssm_a_kernel,
        grid=(nblk,),
        in_specs=[ublk, pl.BlockSpec((GROUPS_PER_BLOCK // 2, 4 * LANES, 2 * LANES), lambda j: (j, 0, 0))],
        out_specs=[pl.BlockSpec((rows, gw), lambda j: (0, j)), sblk, sblk],
        out_shape=[jax.ShapeDtypeStruct((rows, nblk * gw), BF16), sd_h, sd_h],
        scratch_shapes=[pltpu.VMEM((gw // LANES, rows, LANES), F32)],
        compiler_params=_cparams(("arbitrary",)),
        name="ssm_chunk_states",
    )(u, w_pair)

    ablk = pl.BlockSpec((ncb, 1, LANES), lambda j: (j, 0, 0))
    fblk = pl.BlockSpec((ncb, b, LANES), lambda j: (j, 0, 0))
    sd_f = jax.ShapeDtypeStruct((nblk * ncb, b, LANES), F32)
    a16_re = a16_re.reshape(nblk * ncb, 1, LANES)
    a16_im = a16_im.reshape(nblk * ncb, 1, LANES)
    h_re, h_im, f_re, f_im = pl.pallas_call(
        functools.partial(_ssm_scan_kernel, nb=b),
        grid=(nblk,),
        in_specs=[sblk, sblk, ablk, ablk],
        out_specs=[sblk, sblk, fblk, fblk],
        out_shape=[sd_h, sd_h, sd_f, sd_f],
        compiler_params=_cparams(("arbitrary",)),
        name="ssm_scan",
    )(s_re, s_im, a16_re, a16_im)

    y = pl.pallas_call(
        _ssm_c_kernel,
        grid=(nblk,),
        in_specs=[pl.BlockSpec((rows, gw), lambda j: (0, j)), sblk, sblk,
                  pl.BlockSpec((GROUPS_PER_BLOCK, 2 * LANES, 2 * LANES), lambda j: (j, 0, 0)),
                  pl.BlockSpec((GROUPS_PER_BLOCK // 2, 2 * LANES, 4 * LANES), lambda j: (j, 0, 0))],
        out_specs=ublk,
        out_shape=jax.ShapeDtypeStruct((b, s, SSM_WIDTH), F32),
        scratch_shapes=[pltpu.VMEM((gw // LANES, rows, LANES), F32)],
        compiler_params=_cparams(("arbitrary",)),
        name="ssm_outputs",
    )(xg, h_re, h_im, toe, v_pair)
    to_rows = lambda f: jnp.transpose(f, (1, 0, 2)).reshape(b, nblk * ncb * LANES)
    return y, to_rows(f_re), to_rows(f_im)


def _ssm_sample_kernel(u_ref, hre_ref, him_ref, are_ref, aim_ref, bre_ref, bim_ref, cre_ref, cim_ref, d_ref,
                       y_ref, ore_ref, oim_ref, *, t_new):
    h_re = hre_ref[...].T
    h_im = him_ref[...].T
    a_re, a_im = are_ref[...], aim_ref[...]
    bd = h_re.shape[0]
    for t in range(t_new):
        u = u_ref[t]
        ub = u.astype(BF16)
        n_re = a_re * h_re - a_im * h_im + jnp.dot(ub, bre_ref[...], preferred_element_type=F32)
        n_im = a_re * h_im + a_im * h_re + jnp.dot(ub, bim_ref[...], preferred_element_type=F32)
        h_re, h_im = n_re, n_im
        y = (jnp.dot(h_re.astype(BF16), cre_ref[...], preferred_element_type=F32)
             + jnp.dot(h_im.astype(BF16), cim_ref[...], preferred_element_type=F32) + d_ref[...] * u)
        y_ref[t] = y
    ore_ref[...] = h_re.T
    oim_ref[...] = h_im.T


def _block_diag(m):
    g, r, c = m.shape
    eye = jnp.eye(g, dtype=m.dtype)
    return (eye[:, None, :, None] * m[:, :, None, :]).reshape(g * r, g * c)


def _ssm_sample(u, h0_re_t, h0_im_t, tb, c_re, c_im, d_skip, t_new):
    gn = SSM_GROUPS * SSM_STATE
    a_re = tb["pw_re"][1].reshape(1, gn)
    a_im = tb["pw_im"][1].reshape(1, gn)
    b_re = _block_diag(tb["bb_re"].transpose(0, 2, 1)).astype(BF16)
    b_im = _block_diag(tb["bb_im"].transpose(0, 2, 1)).astype(BF16)
    cb_re = _block_diag(c_re.transpose(0, 2, 1)).astype(BF16)
    cb_im = _block_diag(-c_im.transpose(0, 2, 1)).astype(BF16)
    bd = u.shape[1]
    full = lambda shape: pl.BlockSpec(shape, lambda i: (0,) * len(shape))
    return pl.pallas_call(
        functools.partial(_ssm_sample_kernel, t_new=t_new),
        grid=(1,),
        in_specs=[full((t_new, bd, SSM_WIDTH)), full((gn, bd)), full((gn, bd)), full((1, gn)), full((1, gn)),
                  full((SSM_WIDTH, gn)), full((SSM_WIDTH, gn)), full((gn, SSM_WIDTH)), full((gn, SSM_WIDTH)),
                  full((1, SSM_WIDTH))],
        out_specs=[full((t_new, bd, SSM_WIDTH)), full((gn, bd)), full((gn, bd))],
        out_shape=[jax.ShapeDtypeStruct((t_new, bd, SSM_WIDTH), F32),
                   jax.ShapeDtypeStruct((gn, bd), F32), jax.ShapeDtypeStruct((gn, bd), F32)],
        compiler_params=_cparams(("arbitrary",)),
        name="ssm_sample",
    )(u, h0_re_t, h0_im_t, a_re, a_im, b_re, b_im, cb_re, cb_im, d_skip.reshape(1, SSM_WIDTH))


def _rms(x, gain):
    return x * lax.rsqrt(jnp.mean(x * x, axis=-1, keepdims=True) + EPS) * gain


def _gelu_tanh(x):
    return 0.5 * x * (1.0 + jnp.tanh(math.sqrt(2.0 / math.pi) * (x + 0.044715 * (x * x * x))))


def _post_kernel(x_ref, oa_ref, ys_ref, g1_ref, sc2_ref, sh2_ref, g2_ref, n2_ref, ag_ref, sg_ref,
                 wglu_ref, wout_ref, wg_ref, wu_ref, wd_ref, o_ref):
    ya = _gelu_tanh(ys_ref[...])
    ya = ya * _sigmoid(jnp.dot(ya.astype(BF16), wglu_ref[...], preferred_element_type=F32))
    merged = jnp.concatenate([_rms(oa_ref[...], ag_ref[...]), _rms(ya, sg_ref[...])], axis=1)
    x1 = x_ref[...] + g1_ref[...] * jnp.dot(merged.astype(BF16), wout_ref[...], preferred_element_type=F32)
    h2 = (_rms(x1, n2_ref[...]) * (1.0 + sc2_ref[...]) + sh2_ref[...]).astype(BF16)
    gate = jnp.dot(h2, wg_ref[...], preferred_element_type=F32)
    up = jnp.dot(h2, wu_ref[...], preferred_element_type=F32)
    act = (gate * _sigmoid(gate) * up).astype(BF16)
    o_ref[...] = x1 + g2_ref[...] * jnp.dot(act, wd_ref[...], preferred_element_type=F32)


def _post(x, o_att, y_ssm, mods, consts, weights, per_row_mod):
    tm = TOKEN_TILE
    if per_row_mod:
        t = x.shape[0]
        tm = min(tm, t)
        grid = (t // tm,)
        tok = lambda w: pl.BlockSpec((tm, w), lambda i: (i, 0))
        mod = tok(D_MODEL)
        sem = ("arbitrary",)
        out_shape = jax.ShapeDtypeStruct((t, D_MODEL), F32)
    else:
        b, s, _ = x.shape
        grid = (b, s // tm)
        tok = lambda w: pl.BlockSpec((None, tm, w), lambda bi, i: (bi, i, 0))
        mod = pl.BlockSpec((None, 1, D_MODEL), lambda bi, i: (bi, 0, 0))
        sem = ("arbitrary", "arbitrary")
        out_shape = jax.ShapeDtypeStruct((b, s, D_MODEL), F32)
    return pl.pallas_call(
        _post_kernel,
        grid=grid,
        in_specs=[tok(D_MODEL), tok(ATT_WIDTH), tok(SSM_WIDTH), mod, mod, mod, mod]
                 + [_const_spec(c.shape) for c in consts] + [_const_spec(w.shape) for w in weights],
        out_specs=tok(D_MODEL),
        out_shape=out_shape,
        compiler_params=_cparams(sem),
        name="post_sample" if per_row_mod else "post_prompt",
    )(x, o_att, y_ssm, *mods, *consts, *weights)


def kernel(x_prompt, x_sample, cache_k, cache_v, state_ssm_re, state_ssm_im, c_prompt, c_sample, norm1_g, norm2_g, w_ada, b_ada, w_in, q_gain, k_gain, ssm_a_re, ssm_a_im, ssm_log_dt, ssm_b_re, ssm_b_im, ssm_c_re, ssm_c_im, ssm_d, w_glu, attn_out_g, ssm_out_g, w_out, w_gate, w_up, w_down):
    depth = norm1_g.shape[0]
    assert depth == 1, "one decoder layer"
    b, s, _ = x_prompt.shape
    bd, t_new, _ = x_sample.shape
    n_buf = cache_k.shape[2]
    L = 0

    n_c = b + bd
    pad = (-n_c) % SUBLANES
    c_all = jnp.concatenate([c_prompt, c_sample, jnp.zeros((pad, D_MODEL), F32)], axis=0)
    mod = _ada(c_all, w_ada[L], b_ada[L].reshape(1, -1))
    mod_p = [mod[0:b, i * D_MODEL:(i + 1) * D_MODEL].reshape(b, 1, D_MODEL) for i in range(N_MOD)]
    mod_s = [jnp.repeat(mod[b:n_c, i * D_MODEL:(i + 1) * D_MODEL], t_new, axis=0) for i in range(N_MOD)]

    w_in_bf = w_in[L].astype(BF16)
    n1 = norm1_g[L].reshape(1, D_MODEL)
    qg = jnp.tile(q_gain[L], ATT_HEADS).reshape(1, ATT_WIDTH)
    kg = jnp.tile(k_gain[L], ATT_HEADS).reshape(1, ATT_WIDTH)
    consts = (norm2_g[L].reshape(1, D_MODEL), attn_out_g[L].reshape(1, ATT_WIDTH), ssm_out_g[L].reshape(1, SSM_WIDTH))
    weights = tuple(w[L].astype(BF16) for w in (w_glu, w_out, w_gate, w_up, w_down))

    tb = _s5_tables(ssm_a_re[L], ssm_a_im[L], ssm_log_dt[L], ssm_b_re[L], ssm_b_im[L], ssm_c_re[L], ssm_c_im[L], ssm_d[L])
    toe, w_pair, v_pair, a16_re, a16_im = _s5_chunk_operators(tb, ssm_c_re[L], ssm_c_im[L], ssm_d[L])

    q, k, v, u, kt_p, vt_p = _inproj_prompt(x_prompt, mod_p[1], mod_p[0], n1, w_in_bf, qg, kg)
    o_att = _attn_prompt(q, k, v)
    y_ssm, f_re, f_im = _ssm_prompt(u, toe, w_pair, v_pair, a16_re, a16_im)
    y_prompt = _post(x_prompt, o_att, y_ssm, (mod_p[2], mod_p[4], mod_p[3], mod_p[5]), consts, weights, False)

    xs = x_sample.reshape(bd * t_new, D_MODEL)
    qs, ks, vs, us = _inproj_sample(xs, mod_s[1], mod_s[0], n1, w_in_bf, qg, kg)
    kt = jnp.transpose(cache_k[L], (0, 2, 3, 1)).reshape(bd, ATT_WIDTH, n_buf)
    vt = jnp.transpose(cache_v[L], (0, 2, 3, 1)).reshape(bd, ATT_WIDTH, n_buf)
    okt, ovt, o_att_s = _sample_attention(kt, vt, qs.reshape(bd, t_new, ATT_WIDTH),
                                          ks.reshape(bd, t_new, ATT_WIDTH), vs.reshape(bd, t_new, ATT_WIDTH))
    gn = SSM_GROUPS * SSM_STATE
    h0_re = jnp.transpose(state_ssm_re[L], (1, 2, 0)).reshape(gn, bd)
    h0_im = jnp.transpose(state_ssm_im[L], (1, 2, 0)).reshape(gn, bd)
    us_t = jnp.transpose(us.reshape(bd, t_new, SSM_WIDTH), (1, 0, 2))
    y_ssm_t, hs_re, hs_im = _ssm_sample(us_t, h0_re, h0_im, tb, ssm_c_re[L], ssm_c_im[L], ssm_d[L], t_new)
    y_ssm_s = jnp.transpose(y_ssm_t, (1, 0, 2)).reshape(bd * t_new, SSM_WIDTH)
    y_sample = _post(xs, o_att_s.reshape(bd * t_new, ATT_WIDTH), y_ssm_s,
                     (mod_s[2], mod_s[4], mod_s[3], mod_s[5]), consts, weights, True)

    def from_t(a, nb, keep):
        return jnp.transpose(a.reshape(nb, ATT_HEADS, HEAD_DIM, keep), (0, 3, 1, 2))[None]

    def state_from_t(a):
        return jnp.transpose(a.reshape(SSM_GROUPS, SSM_STATE, bd), (2, 0, 1))[None]

    keep = min(MAX_SPAN, s)
    return (y_prompt, y_sample.reshape(bd, t_new, D_MODEL),
            from_t(kt_p, b, keep), from_t(vt_p, b, keep),
            f_re.reshape(b, SSM_GROUPS, SSM_STATE)[None], f_im.reshape(b, SSM_GROUPS, SSM_STATE)[None],
            from_t(okt, bd, n_buf), from_t(ovt, bd, n_buf),
            state_from_t(hs_re), state_from_t(hs_im))
```

```python
import functools
import math

import jax
import jax.numpy as jnp
import numpy as np
from jax import lax
from jax.experimental import pallas as pl
from jax.experimental.pallas import tpu as pltpu

F32 = jnp.float32
BF16 = jnp.bfloat16

D_MODEL = 1024
HEAD_DIM = 64
ATT_WIDTH = 512
ATT_HEADS = 8
SSM_WIDTH = 512
SSM_CH = 16
SSM_GROUPS = 32
SSM_STATE = 64
DILATIONS = ((128, 1), (512, 4), (2048, 16))
N_BACK = 128
MAX_SPAN = 2048
FFN_HIDDEN = 2816
N_MOD = 6
EPS = 1e-6

LANES = 128
SUBLANES = 8
VMEM_LIMIT = 56 * 1024 * 1024

TOKEN_TILE = 512
SUPER = 2048
QB = 128
TILE_GROUP = 4
Q_SCALE = HEAD_DIM ** -0.5 * math.log2(math.e)
CHUNK = 16
NEG = -1e30


def _cparams(sem=None):
    return pltpu.CompilerParams(dimension_semantics=sem, vmem_limit_bytes=VMEM_LIMIT)


def _const_spec(shape):
    nd = len(shape)
    return pl.BlockSpec(shape, lambda *_: (0,) * nd, pipeline_mode=pl.Buffered(1))


def _sigmoid(x):
    return 1.0 / (1.0 + jnp.exp(-x))


def _split_bf16(a):
    hi = a.astype(BF16)
    lo = (a - hi.astype(F32)).astype(BF16)
    return hi, lo


def _ada_kernel(c_ref, w_ref, b_ref, o_ref):
    c = c_ref[...]
    a = c * _sigmoid(c)
    a_hi, a_lo = _split_bf16(a)
    w_hi, w_lo = _split_bf16(w_ref[...])
    acc = jnp.dot(a_hi, w_hi, preferred_element_type=F32)
    acc += jnp.dot(a_hi, w_lo, preferred_element_type=F32)
    acc += jnp.dot(a_lo, w_hi, preferred_element_type=F32)
    o_ref[...] = acc + b_ref[...]


def _ada(c_all, w_ada, b_ada):
    rows = c_all.shape[0]
    n = w_ada.shape[1]
    tn = 1024
    return pl.pallas_call(
        _ada_kernel,
        grid=(n // tn,),
        in_specs=[pl.BlockSpec((rows, D_MODEL), lambda j: (0, 0)),
                  pl.BlockSpec((D_MODEL, tn), lambda j: (0, j)),
                  pl.BlockSpec((1, tn), lambda j: (0, j))],
        out_specs=pl.BlockSpec((rows, tn), lambda j: (0, j)),
        out_shape=jax.ShapeDtypeStruct((rows, n), F32),
        compiler_params=_cparams(("arbitrary",)),
        name="ada",
    )(c_all, w_ada, b_ada)


def _head_rms(z, gain):
    lane = lax.broadcasted_iota(jnp.int32, (1, LANES), 1)
    lo = lane < HEAD_DIM
    outs = []
    for c in range(z.shape[1] // LANES):
        blk = z[:, c * LANES:(c + 1) * LANES]
        sq = blk * blk
        s_lo = jnp.sum(jnp.where(lo, sq, 0.0), axis=-1, keepdims=True)
        s_hi = jnp.sum(jnp.where(lo, 0.0, sq), axis=-1, keepdims=True)
        inv = jnp.where(lo, lax.rsqrt(s_lo * (1.0 / HEAD_DIM) + EPS), lax.rsqrt(s_hi * (1.0 / HEAD_DIM) + EPS))
        outs.append(blk * inv)
    return jnp.concatenate(outs, axis=1) * gain


def _inproj_kernel(x_ref, scale_ref, shift_ref, g_ref, w_ref, qg_ref, kg_ref, *out_refs, kt_first):
    q_ref, k_ref, v_ref, u_ref = out_refs[:4]
    x = x_ref[...]
    ms = jnp.mean(x * x, axis=-1, keepdims=True)
    h = x * lax.rsqrt(ms + EPS) * g_ref[...]
    h = h * (1.0 + scale_ref[...]) + shift_ref[...]
    z = jnp.dot(h.astype(BF16), w_ref[...], preferred_element_type=F32)
    q = _head_rms(z[:, 0:ATT_WIDTH], qg_ref[...]) * Q_SCALE
    k = _head_rms(z[:, ATT_WIDTH:2 * ATT_WIDTH], kg_ref[...])
    v = z[:, 2 * ATT_WIDTH:3 * ATT_WIDTH]
    q_ref[...] = q
    k_ref[...] = k
    v_ref[...] = v
    u_ref[...] = z[:, 3 * ATT_WIDTH:]
    if kt_first is not None:
        kt_ref, vt_ref = out_refs[4:]

        @pl.when(pl.program_id(1) >= kt_first)
        def _():
            kt_ref[...] = k.T
            vt_ref[...] = v.T


def _inproj_prompt(x, scale, shift, g, w_bf, qg, kg):
    b, s, _ = x.shape
    tm = TOKEN_TILE
    nt = s // tm
    keep = min(MAX_SPAN, s)
    kt_first = nt - keep // tm
    tok = pl.BlockSpec((None, tm, D_MODEL), lambda bi, i: (bi, i, 0))
    mod = pl.BlockSpec((None, 1, D_MODEL), lambda bi, i: (bi, 0, 0))
    out_tok = pl.BlockSpec((None, tm, ATT_WIDTH), lambda bi, i: (bi, i, 0))
    out_t = pl.BlockSpec((None, ATT_WIDTH, tm), lambda bi, i: (bi, 0, jnp.maximum(i - kt_first, 0)))
    sd = jax.ShapeDtypeStruct((b, s, ATT_WIDTH), F32)
    sdt = jax.ShapeDtypeStruct((b, ATT_WIDTH, keep), F32)
    return pl.pallas_call(
        functools.partial(_inproj_kernel, kt_first=kt_first),
        grid=(b, nt),
        in_specs=[tok, mod, mod, _const_spec((1, D_MODEL)), _const_spec((D_MODEL, 4 * ATT_WIDTH)),
                  _const_spec((1, ATT_WIDTH)), _const_spec((1, ATT_WIDTH))],
        out_specs=[out_tok] * 4 + [out_t] * 2,
        out_shape=[sd] * 4 + [sdt] * 2,
        compiler_params=_cparams(("arbitrary", "arbitrary")),
        name="inproj_prompt",
    )(x, scale, shift, g, w_bf, qg, kg)


def _inproj_sample(x, scale, shift, g, w_bf, qg, kg):
    t = x.shape[0]
    tm = min(TOKEN_TILE, t)
    tok = pl.BlockSpec((tm, D_MODEL), lambda i: (i, 0))
    out_tok = pl.BlockSpec((tm, ATT_WIDTH), lambda i: (i, 0))
    sd = jax.ShapeDtypeStruct((t, ATT_WIDTH), F32)
    return pl.pallas_call(
        functools.partial(_inproj_kernel, kt_first=None),
        grid=(t // tm,),
        in_specs=[tok, tok, tok, _const_spec((1, D_MODEL)), _const_spec((D_MODEL, 4 * ATT_WIDTH)),
                  _const_spec((1, ATT_WIDTH)), _const_spec((1, ATT_WIDTH))],
        out_specs=[out_tok] * 4,
        out_shape=[sd] * 4,
        compiler_params=_cparams(("arbitrary",)),
        name="inproj_sample",
    )(x, scale, shift, g, w_bf, qg, kg)


def _attn_kernel(q_ref, k_ref, v_ref, bias_ref, o_ref, acc_ref, m_ref, l_ref):
    s_len = q_ref.shape[0]
    lane = lax.broadcasted_iota(jnp.int32, (1, LANES), 1)
    lo = lane < HEAD_DIM
    nt_contract = (((1,), (1,)), ((), ()))

    def super_body(st, carry):
        t0 = st * SUPER
        for p, (_, d) in enumerate(DILATIONS):
            per_res = SUPER // d // QB

            def group_body(gi, c, p=p, d=d, per_res=per_res):
                tiles = []
                for u in range(TILE_GROUP):
                    ti = gi * TILE_GROUP + u
                    r = ti // per_res
                    jt = ti % per_res
                    row0 = r + d * QB * jt
                    qs = t0 + row0
                    first = jnp.logical_and(st == 0, jt == 0)
                    ks = jnp.where(first, qs, qs - d * QB)
                    bias = bias_ref[jnp.where(first, 0, 1)]
                    q = q_ref[pl.ds(qs, QB, stride=d), :]
                    k = k_ref[pl.ds(ks, 2 * QB, stride=d), :].astype(BF16)
                    v = v_ref[pl.ds(ks, 2 * QB, stride=d), :].astype(BF16)
                    q2 = jnp.concatenate([jnp.where(lo, q, 0.0), jnp.where(lo, 0.0, q)], axis=0).astype(BF16)
                    s = lax.dot_general(q2, k, nt_contract, preferred_element_type=F32)
                    tiles.append((s, bias, v, row0))
                probs = []
                for s, bias, v, row0 in tiles:
                    s = s + jnp.concatenate([bias, bias], axis=0)
                    m = jnp.max(s, axis=-1, keepdims=True)
                    pr = jnp.exp2(s - m)
                    l = jnp.sum(pr, axis=-1, keepdims=True)
                    probs.append((pr.astype(BF16), m, l))
                for (s, bias, v, row0), (pr, m, l) in zip(tiles, probs):
                    pv = jnp.dot(pr, v, preferred_element_type=F32)
                    rows = pl.ds(row0, QB, stride=d)
                    acc_ref[p, rows, :] = jnp.where(lo, pv[0:QB], pv[QB:2 * QB])
                    m_ref[p, rows, :] = jnp.where(lo, m[0:QB], m[QB:2 * QB])
                    l_ref[p, rows, :] = jnp.where(lo, l[0:QB], l[QB:2 * QB])
                return c

            lax.fori_loop(0, SUPER // QB // TILE_GROUP, group_body, 0)

        def merge_body(ci, c):
            rows = pl.ds(pl.multiple_of(ci * QB, QB), QB)
            m1, m2, m3 = m_ref[0, rows, :], m_ref[1, rows, :], m_ref[2, rows, :]
            mm = jnp.maximum(jnp.maximum(m1, m2), m3)
            w1, w2, w3 = jnp.exp2(m1 - mm), jnp.exp2(m2 - mm), jnp.exp2(m3 - mm)
            num = w1 * acc_ref[0, rows, :] + w2 * acc_ref[1, rows, :] + w3 * acc_ref[2, rows, :]
            den = w1 * l_ref[0, rows, :] + w2 * l_ref[1, rows, :] + w3 * l_ref[2, rows, :]
            o_ref[pl.ds(pl.multiple_of(t0 + ci * QB, QB), QB), :] = num / den
            return c

        lax.fori_loop(0, SUPER // QB, merge_body, 0)
        return carry

    lax.fori_loop(0, s_len // SUPER, super_body, 0)


def _attn_bias():
    qi = np.arange(QB)[:, None]
    ki = np.arange(2 * QB)[None, :]
    first = (ki <= qi)
    dist = qi + QB - ki
    normal = (dist >= 0) & (dist <= N_BACK)
    return jnp.asarray(np.where(np.stack([first, normal]), 0.0, NEG), F32)


def _attn_prompt(q, k, v):
    b, s, _ = q.shape
    assert s % SUPER == 0 and s >= 2 * SUPER
    blk = pl.BlockSpec((None, s, LANES), lambda bi, hp: (bi, 0, hp))
    return pl.pallas_call(
        _attn_kernel,
        grid=(b, ATT_WIDTH // LANES),
        in_specs=[blk, blk, blk, _const_spec((2, QB, 2 * QB))],
        out_specs=blk,
        out_shape=jax.ShapeDtypeStruct((b, s, ATT_WIDTH), F32),
        scratch_shapes=[pltpu.VMEM((3, SUPER, LANES), F32)] * 3,
        compiler_params=_cparams(("arbitrary", "arbitrary")),
        name="attn_prompt",
    )(q, k, v, _attn_bias())


def _sa_kernel(kt_ref, vt_ref, qx_ref, kn_ref, vn_ref, w_ref, hm_ref, okt_ref, ovt_ref, o_ref, pad_ref):
    n_buf = kt_ref.shape[1]
    t_new = kn_ref.shape[0]
    ext = n_buf + LANES

    def extend(buf_ref, new_ref):
        pad_ref[...] = jnp.zeros_like(pad_ref)
        pad_ref[0:t_new, :] = new_ref[...]
        return jnp.concatenate([buf_ref[...], pad_ref[...].T], axis=1)

    k_ext = extend(kt_ref, kn_ref)
    okt_ref[...] = pltpu.roll(k_ext, ext - t_new, axis=1)[:, 0:n_buf]
    v_ext = extend(vt_ref, vn_ref)
    ovt_ref[...] = pltpu.roll(v_ext, ext - t_new, axis=1)[:, 0:n_buf]

    w = w_ref[...]
    s = jnp.dot(qx_ref[...], k_ext.astype(BF16), preferred_element_type=F32)
    s = jnp.where(w > 0.0, s, NEG)
    m = jnp.max(s, axis=-1, keepdims=True)
    e = w * jnp.exp2(s - m)
    l = jnp.sum(e, axis=-1, keepdims=True)
    pv = lax.dot_general(e.astype(BF16), v_ext.astype(BF16), (((1,), (1,)), ((), ())),
                         preferred_element_type=F32)
    pv = pv * hm_ref[...] / l
    o_ref[...] = jnp.sum(pv.reshape(t_new, ATT_HEADS, ATT_WIDTH), axis=1)


def _sa_weights(t_new, n_buf):
    ext = n_buf + LANES
    pos = np.arange(ext)[None, :]
    i = np.arange(t_new)[:, None]
    delta = n_buf + i - pos
    is_real = (pos < n_buf + t_new)
    w = np.zeros((t_new, ext), np.float32)
    for window, d in DILATIONS:
        w += ((delta >= 0) & (delta % d == 0) & (delta <= window) & is_real)
    return jnp.asarray(np.repeat(w, ATT_HEADS, axis=0), F32)


def _head_mask():
    h = np.arange(ATT_HEADS)[:, None]
    c = np.arange(ATT_WIDTH)[None, :] // HEAD_DIM
    return (h == c).astype(np.float32)


def _sample_attention(kt, vt, q, k_new, v_new):
    bd, _, n_buf = kt.shape
    t_new = q.shape[1]
    assert n_buf == MAX_SPAN, "window buffer must cover the longest pattern (no invalid positions)"
    rows = t_new * ATT_HEADS
    hm = jnp.asarray(np.tile(_head_mask(), (t_new, 1)), F32)
    qx = (jnp.repeat(q, ATT_HEADS, axis=1) * hm[None]).astype(BF16)
    buf = pl.BlockSpec((None, ATT_WIDTH, n_buf), lambda b: (b, 0, 0))
    new = pl.BlockSpec((None, t_new, ATT_WIDTH), lambda b: (b, 0, 0))
    sd_buf = jax.ShapeDtypeStruct((bd, ATT_WIDTH, n_buf), F32)
    return pl.pallas_call(
        _sa_kernel,
        grid=(bd,),
        in_specs=[buf, buf, pl.BlockSpec((None, rows, ATT_WIDTH), lambda b: (b, 0, 0)), new, new,
                  _const_spec((rows, n_buf + LANES)), _const_spec((rows, ATT_WIDTH))],
        out_specs=[buf, buf, new],
        out_shape=[sd_buf, sd_buf, jax.ShapeDtypeStruct((bd, t_new, ATT_WIDTH), F32)],
        scratch_shapes=[pltpu.VMEM((LANES, ATT_WIDTH), F32)],
        compiler_params=_cparams(("arbitrary",)),
        name="sample_attn",
    )(kt, vt, qx, k_new, v_new, _sa_weights(t_new, n_buf), hm)


def _s5_tables(a_re, a_im, log_dt, b_re, b_im):
    g, n = a_re.shape
    dt = jnp.exp(log_dt)[:, None]
    x, y = dt * a_re, dt * a_im
    ks = jnp.arange(CHUNK + 1, dtype=F32)
    mag = jnp.exp(x[:, :, None] * ks)
    pw_re, pw_im = mag * jnp.cos(y[:, :, None] * ks), mag * jnp.sin(y[:, :, None] * ks)
    e_re = jnp.expm1(x) * jnp.cos(y) - 2.0 * jnp.sin(0.5 * y) ** 2
    e_im = jnp.exp(x) * jnp.sin(y)
    den = a_re * a_re + a_im * a_im
    f_re = (e_re * a_re + e_im * a_im) / den
    f_im = (e_im * a_re - e_re * a_im) / den
    bb_re = f_re[..., None] * b_re - f_im[..., None] * b_im
    bb_im = f_re[..., None] * b_im + f_im[..., None] * b_re
    return dict(pw_re=pw_re, pw_im=pw_im, bb_re=bb_re, bb_im=bb_im)


def _s5_chunk_operators(tb, c_re, c_im, d_skip):
    pw_re, pw_im, bb_re, bb_im = tb["pw_re"], tb["pw_im"], tb["bb_re"], tb["bb_im"]
    g, n, ch = bb_re.shape
    L = CHUNK
    hp = lax.Precision.HIGHEST
    ct_re, ct_im = c_re.transpose(0, 2, 1), c_im.transpose(0, 2, 1)
    pr, pi = pw_re[:, :, :, None], pw_im[:, :, :, None]
    ca_re = ct_re[:, :, None, :] * pr - ct_im[:, :, None, :] * pi
    ca_im = ct_re[:, :, None, :] * pi + ct_im[:, :, None, :] * pr
    a_cat = jnp.concatenate([ca_re[:, :, :L].reshape(g, n, L * ch), -ca_im[:, :, :L].reshape(g, n, L * ch)], axis=1)
    b_cat = jnp.concatenate([bb_re, bb_im], axis=1)
    base = jnp.einsum("gnd,gnx->gdx", b_cat, a_cat, precision=hp)
    skip = jnp.eye(ch, dtype=F32)[None] * d_skip.reshape(g, ch, 1)
    base = base + jnp.pad(skip, ((0, 0), (0, 0), (0, (L - 1) * ch)))
    toe = jnp.stack([jnp.pad(base, ((0, 0), (0, 0), (s * ch, 0)))[:, :, :L * ch] for s in range(L)], axis=1)
    toe = toe.reshape(g, L * ch, L * ch)
    rv_re = pw_re[:, :, L - 1::-1].transpose(0, 2, 1)[:, :, None, :]
    rv_im = pw_im[:, :, L - 1::-1].transpose(0, 2, 1)[:, :, None, :]
    bt_re, bt_im = bb_re.transpose(0, 2, 1)[:, None], bb_im.transpose(0, 2, 1)[:, None]
    ws_re = (rv_re * bt_re - rv_im * bt_im).reshape(g, L * ch, n)
    ws_im = (rv_re * bt_im + rv_im * bt_re).reshape(g, L * ch, n)
    vr = ca_re[:, :, 1:].reshape(g, n, L * ch)
    vi = (-ca_im[:, :, 1:]).reshape(g, n, L * ch)
    gp = g // 2
    z_w = jnp.zeros((gp, L * ch, n), F32)
    wr, wi = ws_re.reshape(gp, 2, L * ch, n), ws_im.reshape(gp, 2, L * ch, n)
    w_pair = jnp.concatenate([
        jnp.concatenate([wr[:, 0], z_w, wi[:, 0], z_w], axis=2),
        jnp.concatenate([z_w, wr[:, 1], z_w, wi[:, 1]], axis=2)], axis=1)
    z_v = jnp.zeros((gp, n, L * ch), F32)
    vr, vi = vr.reshape(gp, 2, n, L * ch), vi.reshape(gp, 2, n, L * ch)
    v_pair = jnp.concatenate([
        jnp.concatenate([vr[:, 0], z_v], axis=2), jnp.concatenate([z_v, vr[:, 1]], axis=2),
        jnp.concatenate([vi[:, 0], z_v], axis=2), jnp.concatenate([z_v, vi[:, 1]], axis=2)], axis=1)
    a16_re = pw_re[:, :, L].reshape(1, g * n)
    a16_im = pw_im[:, :, L].reshape(1, g * n)
    return toe.astype(BF16), w_pair.astype(BF16), v_pair.astype(BF16), a16_re, a16_im


GROUPS_PER_BLOCK = LANES // SSM_CH
REGROUP_ROWS = 32


def _granule_id():
    return lax.broadcasted_iota(jnp.int32, (1, LANES), 1) // SSM_CH


def _ssm_a_kernel(u_ref, w_ref, xg_ref, sre_ref, sim_ref, xs_ref):
    nb, s_len, _ = u_ref.shape
    n_chunks = s_len // CHUNK
    gran = _granule_id()
    half = CHUNK // 2

    for b in range(nb):
        def regroup(ct, c, b=b):
            c0 = pl.multiple_of(ct * REGROUP_ROWS, REGROUP_ROWS)
            rows = pl.ds(b * n_chunks + c0, REGROUP_ROWS)
            for j in range(2):
                z = [u_ref[b, pl.ds(c0 * CHUNK + half * j + tt, REGROUP_ROWS, stride=CHUNK), :]
                     for tt in range(half)]
                for p in range(GROUPS_PER_BLOCK):
                    acc = z[p]
                    for tt in range(half):
                        if tt != p:
                            rolled = pltpu.roll(z[tt], SSM_CH * ((tt - p) % GROUPS_PER_BLOCK), axis=1)
                            acc = jnp.where(gran == tt, rolled, acc)
                    xs_ref[2 * p + j, rows, :] = acc
            return c
        lax.fori_loop(0, n_chunks // REGROUP_ROWS, regroup, 0)

    for col in range(2 * GROUPS_PER_BLOCK):
        xg_ref[:, col * LANES:(col + 1) * LANES] = xs_ref[col].astype(BF16)
    for pp in range(GROUPS_PER_BLOCK // 2):
        xp = xg_ref[:, pp * 4 * LANES:(pp + 1) * 4 * LANES]
        st = jnp.dot(xp, w_ref[pp], preferred_element_type=F32)
        sre_ref[pp] = st[:, 0:LANES]
        sim_ref[pp] = st[:, LANES:2 * LANES]


def _ssm_scan_kernel(sre_ref, sim_ref, are_ref, aim_ref, hre_ref, him_ref, fre_ref, fim_ref, *, nb):
    ncb, rows, _ = sre_ref.shape
    n_chunks = rows // nb
    a_re = [are_ref[cb] for cb in range(ncb)]
    a_im = [aim_ref[cb] for cb in range(ncb)]

    def body(c, carry):
        r = pl.ds(c, nb, stride=n_chunks)
        out = []
        for cb in range(ncb):
            h_re, h_im = carry[2 * cb], carry[2 * cb + 1]
            hre_ref[cb, r, :] = h_re
            him_ref[cb, r, :] = h_im
            out.append(a_re[cb] * h_re - a_im[cb] * h_im + sre_ref[cb, r, :])
            out.append(a_re[cb] * h_im + a_im[cb] * h_re + sim_ref[cb, r, :])
        return tuple(out)

    z = jnp.zeros((nb, LANES), F32)
    fin = lax.fori_loop(0, n_chunks, body, (z,) * (2 * ncb), unroll=4)
    for cb in range(ncb):
        fre_ref[cb] = fin[2 * cb]
        fim_ref[cb] = fin[2 * cb + 1]


def _ssm_c_kernel(xg_ref, hre_ref, him_ref, t_ref, v_ref, y_ref, ys_ref):
    nb, s_len, _ = y_ref.shape
    n_chunks = s_len // CHUNK
    gran = _granule_id()
    half = CHUNK // 2
    for pp in range(GROUPS_PER_BLOCK // 2):
        hcat = jnp.concatenate([hre_ref[pp], him_ref[pp]], axis=1).astype(BF16)
        inter = jnp.dot(hcat, v_ref[pp], preferred_element_type=F32)
        for e in range(2):
            g = 2 * pp + e
            intra = jnp.dot(xg_ref[:, g * 2 * LANES:(g + 1) * 2 * LANES], t_ref[g],
                            preferred_element_type=F32)
            yg = intra + inter[:, e * 2 * LANES:(e + 1) * 2 * LANES]
            ys_ref[2 * g] = yg[:, 0:LANES]
            ys_ref[2 * g + 1] = yg[:, LANES:2 * LANES]

    for b in range(nb):
        def regroup(ct, c, b=b):
            c0 = pl.multiple_of(ct * REGROUP_ROWS, REGROUP_ROWS)
            rows = pl.ds(b * n_chunks + c0, REGROUP_ROWS)
            for j in range(2):
                yp =[ys_ref[2 * p + j, rows, :] for p in range(GROUPS_PER_BLOCK)]
                for tt in range(half):
                    acc = yp[tt]
                    for p in range(GROUPS_PER_BLOCK):
                        if p != tt:
                            rolled = pltpu.roll(yp[p], SSM_CH * ((p - tt) % GROUPS_PER_BLOCK), axis=1)
                            acc = jnp.where(gran == p, rolled, acc)
                    y_ref[b, pl.ds(c0 * CHUNK + half * j + tt, REGROUP_ROWS, stride=CHUNK), :] = acc
            return c
        lax.fori_loop(0, n_chunks // REGROUP_ROWS, regroup, 0)


def _ssm_prompt(u, toe, w_pair, v_pair, a16_re, a16_im):
    b, s, _ = u.shape
    rows = b * (s // CHUNK)
    nblk = SSM_WIDTH // LANES
    gw = GROUPS_PER_BLOCK * CHUNK * SSM_CH
    ncb = GROUPS_PER_BLOCK * SSM_STATE // LANES
    ublk = pl.BlockSpec((b, s, LANES), lambda j: (0, 0, j))
    sblk = pl.BlockSpec((ncb, rows, LANES), lambda j: (j, 0, 0))
    sd_h = jax.ShapeDtypeStruct((nblk * ncb, rows, LANES), F32)
    xg, s_re, s_im = pl.pallas_call(
        _ssm_a_kernel,
        grid=(nblk,),
        in_specs=[ublk, pl.BlockSpec((GROUPS_PER_BLOCK // 2, 4 * LANES, 2 * LANES), lambda j: (j, 0, 0))],
        out_specs=[pl.BlockSpec((rows, gw), lambda j: (0, j)), sblk, sblk],
        out_shape=[jax.ShapeDtypeStruct((rows, nblk * gw), BF16), sd_h, sd_h],
        scratch_shapes=[pltpu.VMEM((gw // LANES, rows, LANES), F32)],
        compiler_params=_cparams(("arbitrary",)),
        name="ssm_chunk_states",
    )(u, w_pair)

    ablk = pl.BlockSpec((ncb, 1, LANES), lambda j: (j, 0, 0))
    fblk = pl.BlockSpec((ncb, b, LANES), lambda j: (j, 0, 0))
    sd_f = jax.ShapeDtypeStruct((nblk * ncb, b, LANES), F32)
    a16_re = a16_re.reshape(nblk * ncb, 1, LANES)
    a16_im = a16_im.reshape(nblk * ncb, 1, LANES)
    h_re, h_im, f_re, f_im = pl.pallas_call(
        functools.partial(_ssm_scan_kernel, nb=b),
        grid=(nblk,),
        in_specs=[sblk, sblk, ablk, ablk],
        out_specs=[sblk, sblk, fblk, fblk],
        out_shape=[sd_h, sd_h, sd_f, sd_f],
        compiler_params=_cparams(("arbitrary",)),
        name="ssm_scan",
    )(s_re, s_im, a16_re, a16_im)

    y = pl.pallas_call(
        _ssm_c_kernel,
        grid=(nblk,),
        in_specs=[pl.BlockSpec((rows, gw), lambda j: (0, j)), sblk, sblk,
                  pl.BlockSpec((GROUPS_PER_BLOCK, 2 * LANES, 2 * LANES), lambda j: (j, 0, 0)),
                  pl.BlockSpec((GROUPS_PER_BLOCK // 2, 2 * LANES, 4 * LANES), lambda j: (j, 0, 0))],
        out_specs=ublk,
        out_shape=jax.ShapeDtypeStruct((b, s, SSM_WIDTH), F32),
        scratch_shapes=[pltpu.VMEM((gw // LANES, rows, LANES), F32)],
        compiler_params=_cparams(("arbitrary",)),
        name="ssm_outputs",
    )(xg, h_re, h_im, toe, v_pair)
    to_rows = lambda f: jnp.transpose(f, (1, 0, 2)).reshape(b, nblk * ncb * LANES)
    return y, to_rows(f_re), to_rows(f_im)


def _ssm_sample_kernel(u_ref, hre_ref, him_ref, are_ref, aim_ref, bre_ref, bim_ref, cre_ref, cim_ref, d_ref,
                       y_ref, ore_ref, oim_ref, *, t_new):
    h_re = hre_ref[...].T
    h_im = him_ref[...].T
    a_re, a_im = are_ref[...], aim_ref[...]
    bd = h_re.shape[0]
    for t in range(t_new):
        u = u_ref[t]
        ub = u.astype(BF16)
        n_re = a_re * h_re - a_im * h_im + jnp.dot(ub, bre_ref[...], preferred_element_type=F32)
        n_im = a_re * h_im + a_im * h_re + jnp.dot(ub, bim_ref[...], preferred_element_type=F32)
        h_re, h_im = n_re, n_im
        y = (jnp.dot(h_re.astype(BF16), cre_ref[...], preferred_element_type=F32)
             + jnp.dot(h_im.astype(BF16), cim_ref[...], preferred_element_type=F32) + d_ref[...] * u)
        y_ref[t] = y
    ore_ref[...] = h_re.T
    oim_ref[...] = h_im.T


def _block_diag(m):
    g, r, c = m.shape
    eye = jnp.eye(g, dtype=m.dtype)
    return (eye[:, None, :, None] * m[:, :, None, :]).reshape(g * r, g * c)


def _ssm_sample(u, h0_re_t, h0_im_t, tb, c_re, c_im, d_skip, t_new):
    gn = SSM_GROUPS * SSM_STATE
    a_re = tb["pw_re"][:, :, 1].reshape(1, gn)
    a_im = tb["pw_im"][:, :, 1].reshape(1, gn)
    b_re = _block_diag(tb["bb_re"].transpose(0, 2, 1)).astype(BF16)
    b_im = _block_diag(tb["bb_im"].transpose(0, 2, 1)).astype(BF16)
    cb_re = _block_diag(c_re.transpose(0, 2, 1)).astype(BF16)
    cb_im = _block_diag(-c_im.transpose(0, 2, 1)).astype(BF16)
    bd = u.shape[1]
    full = lambda shape: pl.BlockSpec(shape, lambda i: (0,) * len(shape))
    return pl.pallas_call(
        functools.partial(_ssm_sample_kernel, t_new=t_new),
        grid=(1,),
        in_specs=[full((t_new, bd, SSM_WIDTH)), full((gn, bd)), full((gn, bd)), full((1, gn)), full((1, gn)),
                  full((SSM_WIDTH, gn)), full((SSM_WIDTH, gn)), full((gn, SSM_WIDTH)), full((gn, SSM_WIDTH)),
                  full((1, SSM_WIDTH))],
        out_specs=[full((t_new, bd, SSM_WIDTH)), full((gn, bd)), full((gn, bd))],
        out_shape=[jax.ShapeDtypeStruct((t_new, bd, SSM_WIDTH), F32),
                   jax.ShapeDtypeStruct((gn, bd), F32), jax.ShapeDtypeStruct((gn, bd), F32)],
        compiler_params=_cparams(("arbitrary",)),
        name="ssm_sample",
    )(u, h0_re_t, h0_im_t, a_re, a_im, b_re, b_im, cb_re, cb_im, d_skip.reshape(1, SSM_WIDTH))


def _rms(x, gain):
    return x * lax.rsqrt(jnp.mean(x * x, axis=-1, keepdims=True) + EPS) * gain


def _gelu_tanh(x):
    return 0.5 * x * (1.0 + jnp.tanh(math.sqrt(2.0 / math.pi) * (x + 0.044715 * (x * x * x))))


def _post_kernel(x_ref, oa_ref, ys_ref, g1_ref, sc2_ref, sh2_ref, g2_ref, n2_ref, ag_ref, sg_ref,
                 wglu_ref, wout_ref, wg_ref, wu_ref, wd_ref, o_ref):
    ya = _gelu_tanh(ys_ref[...])
    ya = ya * _sigmoid(jnp.dot(ya.astype(BF16), wglu_ref[...], preferred_element_type=F32))
    merged = jnp.concatenate([_rms(oa_ref[...], ag_ref[...]), _rms(ya, sg_ref[...])], axis=1)
    x1 = x_ref[...] + g1_ref[...] * jnp.dot(merged.astype(BF16), wout_ref[...], preferred_element_type=F32)
    h2 = (_rms(x1, n2_ref[...]) * (1.0 + sc2_ref[...]) + sh2_ref[...]).astype(BF16)
    gate = jnp.dot(h2, wg_ref[...], preferred_element_type=F32)
    up = jnp.dot(h2, wu_ref[...], preferred_element_type=F32)
    act = (gate * _sigmoid(gate) * up).astype(BF16)
    o_ref[...] = x1 + g2_ref[...] * jnp.dot(act, wd_ref[...], preferred_element_type=F32)


def _post(x, o_att, y_ssm, mods, consts, weights, per_row_mod):
    tm = TOKEN_TILE
    if per_row_mod:
        t = x.shape[0]
        tm = min(tm, t)
        grid = (t // tm,)
        tok = lambda w: pl.BlockSpec((tm, w), lambda i: (i, 0))
        mod = tok(D_MODEL)
        sem = ("arbitrary",)
        out_shape = jax.ShapeDtypeStruct((t, D_MODEL), F32)
    else:
        b, s, _ = x.shape
        grid = (b, s // tm)
        tok = lambda w: pl.BlockSpec((None, tm, w), lambda bi, i: (bi, i, 0))
        mod = pl.BlockSpec((None, 1, D_MODEL), lambda bi, i: (bi, 0, 0))
        sem = ("arbitrary", "arbitrary")
        out_shape = jax.ShapeDtypeStruct((b, s, D_MODEL), F32)
    return pl.pallas_call(
        _post_kernel,
        grid=grid,
        in_specs=[tok(D_MODEL), tok(ATT_WIDTH), tok(SSM_WIDTH), mod, mod, mod, mod]
                 + [_const_spec(c.shape) for c in consts] + [_const_spec(w.shape) for w in weights],
        out_specs=tok(D_MODEL),
        out_shape=out_shape,
        compiler_params=_cparams(sem),
        name="post_sample" if per_row_mod else "post_prompt",
    )(x, o_att, y_ssm, *mods, *consts, *weights)


def kernel(x_prompt, x_sample, cache_k, cache_v, state_ssm_re, state_ssm_im, c_prompt, c_sample, norm1_g, norm2_g, w_ada, b_ada, w_in, q_gain, k_gain, ssm_a_re, ssm_a_im, ssm_log_dt, ssm_b_re, ssm_b_im, ssm_c_re, ssm_c_im, ssm_d, w_glu, attn_out_g, ssm_out_g, w_out, w_gate, w_up, w_down):
    depth = norm1_g.shape[0]
    assert depth == 1, "one decoder layer"
    b, s, _ = x_prompt.shape
    bd, t_new, _ = x_sample.shape
    n_buf = cache_k.shape[2]
    L = 0

    n_c = b + bd
    pad = (-n_c) % SUBLANES
    c_all = jnp.concatenate([c_prompt, c_sample, jnp.zeros((pad, D_MODEL), F32)], axis=0)
    mod = _ada(c_all, w_ada[L], b_ada[L].reshape(1, -1))
    mod_p = [mod[0:b, i * D_MODEL:(i + 1) * D_MODEL].reshape(b, 1, D_MODEL) for i in range(N_MOD)]
    mod_s = [jnp.repeat(mod[b:n_c, i * D_MODEL:(i + 1) * D_MODEL], t_new, axis=0) for i in range(N_MOD)]

    w_in_bf = w_in[L].astype(BF16)
    n1 = norm1_g[L].reshape(1, D_MODEL)
    qg = jnp.tile(q_gain[L], ATT_HEADS).reshape(1, ATT_WIDTH)
    kg = jnp.tile(k_gain[L], ATT_HEADS).reshape(1, ATT_WIDTH)
    consts = (norm2_g[L].reshape(1, D_MODEL), attn_out_g[L].reshape(1, ATT_WIDTH), ssm_out_g[L].reshape(1, SSM_WIDTH))
    weights = tuple(w[L].astype(BF16) for w in (w_glu, w_out, w_gate, w_up, w_down))

    tb = _s5_tables(ssm_a_re[L], ssm_a_im[L], ssm_log_dt[L], ssm_b_re[L], ssm_b_im[L])
    toe, w_pair, v_pair, a16_re, a16_im = _s5_chunk_operators(tb, ssm_c_re[L], ssm_c_im[L], ssm_d[L])

    q, k, v, u, kt_p, vt_p = _inproj_prompt(x_prompt, mod_p[1], mod_p[0], n1, w_in_bf, qg, kg)
    o_att = _attn_prompt(q, k, v)
    y_ssm, f_re, f_im = _ssm_prompt(u, toe, w_pair, v_pair, a16_re, a16_im)
    y_prompt = _post(x_prompt, o_att, y_ssm, (mod_p[2], mod_p[4], mod_p[3], mod_p[5]), consts, weights, False)

    xs = x_sample.reshape(bd * t_new, D_MODEL)
    qs, ks, vs, us = _inproj_sample(xs, mod_s[1], mod_s[0], n1, w_in_bf, qg, kg)
    kt = jnp.transpose(cache_k[L], (0, 2, 3, 1)).reshape(bd, ATT_WIDTH, n_buf)
    vt = jnp.transpose(cache_v[L], (0, 2, 3, 1)).reshape(bd, ATT_WIDTH, n_buf)
    okt, ovt, o_att_s = _sample_attention(kt, vt, qs.reshape(bd, t_new, ATT_WIDTH),
                                          ks.reshape(bd, t_new, ATT_WIDTH), vs.reshape(bd, t_new, ATT_WIDTH))
    gn = SSM_GROUPS * SSM_STATE
    h0_re = jnp.transpose(state_ssm_re[L], (1, 2, 0)).reshape(gn, bd)
    h0_im = jnp.transpose(state_ssm_im[L], (1, 2, 0)).reshape(gn, bd)
    us_t = jnp.transpose(us.reshape(bd, t_new, SSM_WIDTH), (1, 0, 2))
    y_ssm_t, hs_re, hs_im = _ssm_sample(us_t, h0_re, h0_im, tb, ssm_c_re[L], ssm_c_im[L], ssm_d[L], t_new)
    y_ssm_s = jnp.transpose(y_ssm_t, (1, 0, 2)).reshape(bd * t_new, SSM_WIDTH)
    y_sample = _post(xs, o_att_s.reshape(bd * t_new, ATT_WIDTH), y_ssm_s,
                     (mod_s[2], mod_s[4], mod_s[3], mod_s[5]), consts, weights, True)

    def from_t(a, nb, keep):
        return jnp.transpose(a.reshape(nb, ATT_HEADS, HEAD_DIM, keep), (0, 3, 1, 2))[None]

    def state_from_t(a):
        return jnp.transpose(a.reshape(SSM_GROUPS, SSM_STATE, bd), (2, 0, 1))[None]

    keep = min(MAX_SPAN, s)
    return (y_prompt, y_sample.reshape(bd, t_new, D_MODEL),
            from_t(kt_p, b, keep), from_t(vt_p, b, keep),
            f_re.reshape(b, SSM_GROUPS, SSM_STATE)[None], f_im.reshape(b, SSM_GROUPS, SSM_STATE)[None],
            from_t(okt, bd, n_buf), from_t(ovt, bd, n_buf),
            state_from_t(hs_re), state_from_t(hs_im))
```

```python
import functools
import math

import jax
import jax.numpy as jnp
import numpy as np
from jax import lax
from jax.experimental import pallas as pl
from jax.experimental.pallas import tpu as pltpu

F32 = jnp.float32
BF16 = jnp.bfloat16

D_MODEL = 1024
HEAD_DIM = 64
ATT_WIDTH = 512
ATT_HEADS = 8
SSM_WIDTH = 512
SSM_CH = 16
SSM_GROUPS = 32
SSM_STATE = 64
DILATIONS = ((128, 1), (512, 4), (2048, 16))
N_BACK = 128
MAX_SPAN = 2048
FFN_HIDDEN = 2816
N_MOD = 6
EPS = 1e-6

LANES = 128
SUBLANES = 8
VMEM_LIMIT = 56 * 1024 * 1024

TOKEN_TILE = 512
SUPER = 2048
QB = 128
TILE_GROUP = 4
Q_SCALE = HEAD_DIM ** -0.5 * math.log2(math.e)
CHUNK = 16
NEG = -1e30


def _cparams(sem=None):
    return pltpu.CompilerParams(dimension_semantics=sem, vmem_limit_bytes=VMEM_LIMIT)


def _const_spec(shape):
    nd = len(shape)
    return pl.BlockSpec(shape, lambda *_: (0,) * nd, pipeline_mode=pl.Buffered(1))


def _sigmoid(x):
    return 1.0 / (1.0 + jnp.exp(-x))


def _split_bf16(a):
    hi = a.astype(BF16)
    lo = (a - hi.astype(F32)).astype(BF16)
    return hi, lo


def _ada_kernel(c_ref, w_ref, b_ref, o_ref):
    c = c_ref[...]
    a = c * _sigmoid(c)
    a_hi, a_lo = _split_bf16(a)
    w_hi, w_lo = _split_bf16(w_ref[...])
    acc = jnp.dot(a_hi, w_hi, preferred_element_type=F32)
    acc += jnp.dot(a_hi, w_lo, preferred_element_type=F32)
    acc += jnp.dot(a_lo, w_hi, preferred_element_type=F32)
    o_ref[...] = acc + b_ref[...]


def _ada(c_all, w_ada, b_ada):
    rows = c_all.shape[0]
    n = w_ada.shape[1]
    tn = 1024
    return pl.pallas_call(
        _ada_kernel,
        grid=(n // tn,),
        in_specs=[pl.BlockSpec((rows, D_MODEL), lambda j: (0, 0)),
                  pl.BlockSpec((D_MODEL, tn), lambda j: (0, j)),
                  pl.BlockSpec((1, tn), lambda j: (0, j))],
        out_specs=pl.BlockSpec((rows, tn), lambda j: (0, j)),
        out_shape=jax.ShapeDtypeStruct((rows, n), F32),
        compiler_params=_cparams(("arbitrary",)),
        name="ada",
    )(c_all, w_ada, b_ada)


def _head_rms(z, gain):
    lane = lax.broadcasted_iota(jnp.int32, (1, LANES), 1)
    lo = lane < HEAD_DIM
    outs = []
    for c in range(z.shape[1] // LANES):
        blk = z[:, c * LANES:(c + 1) * LANES]
        sq = blk * blk
        s_lo = jnp.sum(jnp.where(lo, sq, 0.0), axis=-1, keepdims=True)
        s_hi = jnp.sum(jnp.where(lo, 0.0, sq), axis=-1, keepdims=True)
        inv = jnp.where(lo, lax.rsqrt(s_lo * (1.0 / HEAD_DIM) + EPS), lax.rsqrt(s_hi * (1.0 / HEAD_DIM) + EPS))
        outs.append(blk * inv)
    return jnp.concatenate(outs, axis=1) * gain


def _inproj_body(x_ref, scale_ref, shift_ref, g_ref, w_ref, qg_ref, kg_ref, *out_refs, kt_rule=None):
    q_ref, k_ref, v_ref, u_ref = out_refs[:4]
    x = x_ref[...]
    ms = jnp.mean(x * x, axis=-1, keepdims=True)
    h = x * lax.rsqrt(ms + EPS) * g_ref[...]
    h = h * (1.0 + scale_ref[...]) + shift_ref[...]
    z = jnp.dot(h.astype(BF16), w_ref[...], preferred_element_type=F32)
    q = _head_rms(z[:, 0:ATT_WIDTH], qg_ref[...]) * Q_SCALE
    k = _head_rms(z[:, ATT_WIDTH:2 * ATT_WIDTH], kg_ref[...])
    v = z[:, 2 * ATT_WIDTH:3 * ATT_WIDTH]
    q_ref[...] = q
    k_ref[...] = k
    v_ref[...] = v
    u_ref[...] = z[:, 3 * ATT_WIDTH:]
    if kt_rule is not None:
        per_seq, kt_first = kt_rule
        kt_ref, vt_ref = out_refs[4:]

        @pl.when(pl.program_id(0) % per_seq >= kt_first)
        def _():
            kt_ref[...] = k.T
            vt_ref[...] = v.T


def _inproj_sample(x, scale, shift, g, w_bf, qg, kg):
    t = x.shape[0]
    tm = min(TOKEN_TILE, t)
    tok = pl.BlockSpec((tm, D_MODEL), lambda i: (i, 0))
    out_tok = pl.BlockSpec((tm, ATT_WIDTH), lambda i: (i, 0))
    sd = jax.ShapeDtypeStruct((t, ATT_WIDTH), F32)
    return pl.pallas_call(
        _inproj_body,
        grid=(t // tm,),
        in_specs=[tok, tok, tok, _const_spec((1, D_MODEL)), _const_spec((D_MODEL, 4 * ATT_WIDTH)),
                  _const_spec((1, ATT_WIDTH)), _const_spec((1, ATT_WIDTH))],
        out_specs=[out_tok] * 4,
        out_shape=[sd] * 4,
        compiler_params=_cparams(("arbitrary",)),
        name="inproj_sample",
    )(x, scale, shift, g, w_bf, qg, kg)


def _attn_kernel(q_ref, k_ref, v_ref, bias_ref, o_ref, acc_ref, m_ref, l_ref):
    s_len = q_ref.shape[0]
    lane = lax.broadcasted_iota(jnp.int32, (1, LANES), 1)
    lo = lane < HEAD_DIM
    nt_contract = (((1,), (1,)), ((), ()))

    def super_body(st, carry):
        t0 = st * SUPER
        for p, (_, d) in enumerate(DILATIONS):
            per_res = SUPER // d // QB

            def group_body(gi, c, p=p, d=d, per_res=per_res):
                tiles = []
                for u in range(TILE_GROUP):
                    ti = gi * TILE_GROUP + u
                    r = ti // per_res
                    jt = ti % per_res
                    row0 = r + d * QB * jt
                    qs = t0 + row0
                    first = jnp.logical_and(st == 0, jt == 0)
                    ks = jnp.where(first, qs, qs - d * QB)
                    bias = bias_ref[jnp.where(first, 0, 1)]
                    q = q_ref[pl.ds(qs, QB, stride=d), :]
                    k = k_ref[pl.ds(ks, 2 * QB, stride=d), :].astype(BF16)
                    v = v_ref[pl.ds(ks, 2 * QB, stride=d), :].astype(BF16)
                    q2 = jnp.concatenate([jnp.where(lo, q, 0.0), jnp.where(lo, 0.0, q)], axis=0).astype(BF16)
                    s = lax.dot_general(q2, k, nt_contract, preferred_element_type=F32)
                    tiles.append((s, bias, v, row0))
                probs = []
                for s, bias, v, row0 in tiles:
                    s = s + jnp.concatenate([bias, bias], axis=0)
                    m = jnp.max(s, axis=-1, keepdims=True)
                    pr = jnp.exp2(s - m)
                    l = jnp.sum(pr, axis=-1, keepdims=True)
                    probs.append((pr.astype(BF16), m, l))
                for (s, bias, v, row0), (pr, m, l) in zip(tiles, probs):
                    pv = jnp.dot(pr, v, preferred_element_type=F32)
                    rows = pl.ds(row0, QB, stride=d)
                    acc_ref[p, rows, :] = jnp.where(lo, pv[0:QB], pv[QB:2 * QB])
                    m_ref[p, rows, :] = jnp.where(lo, m[0:QB], m[QB:2 * QB])
                    l_ref[p, rows, :] = jnp.where(lo, l[0:QB], l[QB:2 * QB])
                return c

            lax.fori_loop(0, SUPER // QB // TILE_GROUP, group_body, 0)

        def merge_body(ci, c):
            rows = pl.ds(pl.multiple_of(ci * QB, QB), QB)
            m1, m2, m3 = m_ref[0, rows, :], m_ref[1, rows, :], m_ref[2, rows, :]
            mm = jnp.maximum(jnp.maximum(m1, m2), m3)
            w1, w2, w3 = jnp.exp2(m1 - mm), jnp.exp2(m2 - mm), jnp.exp2(m3 - mm)
            num = w1 * acc_ref[0, rows, :] + w2 * acc_ref[1, rows, :] + w3 * acc_ref[2, rows, :]
            den = w1 * l_ref[0, rows, :] + w2 * l_ref[1, rows, :] + w3 * l_ref[2, rows, :]
            o_ref[pl.ds(pl.multiple_of(t0 + ci * QB, QB), QB), :] = num / den
            return c

        lax.fori_loop(0, SUPER // QB, merge_body, 0)
        return carry

    lax.fori_loop(0, s_len // SUPER, super_body, 0)


def _attn_bias():
    qi = np.arange(QB)[:, None]
    ki = np.arange(2 * QB)[None, :]
    first = (ki <= qi)
    dist = qi + QB - ki
    normal = (dist >= 0) & (dist <= N_BACK)
    return jnp.asarray(np.where(np.stack([first, normal]), 0.0, NEG), F32)


def _attn_prompt(q, k, v):
    b, s, _ = q.shape
    assert s % SUPER == 0 and s >= 2 * SUPER
    blk = pl.BlockSpec((None, s, LANES), lambda bi, hp: (bi, 0, hp))
    return pl.pallas_call(
        _attn_kernel,
        grid=(b, ATT_WIDTH // LANES),
        in_specs=[blk, blk, blk, _const_spec((2, QB, 2 * QB))],
        out_specs=blk,
        out_shape=jax.ShapeDtypeStruct((b, s, ATT_WIDTH), F32),
        scratch_shapes=[pltpu.VMEM((3, SUPER, LANES), F32)] * 3,
        compiler_params=_cparams(("arbitrary", "arbitrary")),
        name="attn_prompt",
    )(q, k, v, _attn_bias())


SA_HEADS = 4
SA_WIDTH = SA_HEADS * HEAD_DIM
N_SA_IN = 7


def _stream_kernel(*refs, rider, n_rider_in, n_rider_out):
    kt_ref, vt_ref, qx_ref, kn_ref, vn_ref, w_ref, hm_ref = refs[:N_SA_IN]
    rider_in = refs[N_SA_IN:N_SA_IN + n_rider_in]
    outs = refs[len(refs) - 1 - 3 - n_rider_out:len(refs) - 1]
    okt_ref, ovt_ref, o_ref = outs[:3]
    rider_out = outs[3:]
    pad_ref = refs[-1]
    n_buf = kt_ref.shape[1]
    t_new = kn_ref.shape[0]
    ext = n_buf + LANES

    def extend(buf_ref, new_ref):
        pad_ref[...] = jnp.zeros_like(pad_ref)
        pad_ref[0:t_new, :] = new_ref[...]
        return jnp.concatenate([buf_ref[...], pad_ref[...].T], axis=1)

    k_ext = extend(kt_ref, kn_ref)
    v_ext = extend(vt_ref, vn_ref)
    w = w_ref[...]
    s = jnp.dot(qx_ref[...], k_ext.astype(BF16), preferred_element_type=F32)
    s = jnp.where(w > 0.0, s, NEG)
    m = jnp.max(s, axis=-1, keepdims=True)
    e = w * jnp.exp2(s - m)
    l = jnp.sum(e, axis=-1, keepdims=True)
    pv = lax.dot_general(e.astype(BF16), v_ext.astype(BF16), (((1,), (1,)), ((), ())),
                         preferred_element_type=F32)
    pv = pv * hm_ref[...] / l
    o = pv[0:t_new]
    for h in range(1, SA_HEADS):
        o = o + pv[h * t_new:(h + 1) * t_new]
    o_ref[...] = o

    rider(*rider_in, *rider_out)

    okt_ref[...] = pltpu.roll(k_ext, ext - t_new, axis=1)[:, 0:n_buf]
    ovt_ref[...] = pltpu.roll(v_ext, ext - t_new, axis=1)[:, 0:n_buf]


def _sa_weights(t_new, n_buf):
    ext = n_buf + LANES
    pos = np.arange(ext)[None, :]
    i = np.arange(t_new)[:, None]
    delta = n_buf + i - pos
    is_real = (pos < n_buf + t_new)
    w = np.zeros((t_new, ext), np.float32)
    for window, d in DILATIONS:
        w += ((delta >= 0) & (delta % d == 0) & (delta <= window) & is_real)
    return jnp.asarray(np.tile(w, (SA_HEADS, 1)), F32)


def _head_mask(t_new):
    h = np.repeat(np.arange(SA_HEADS), t_new)[:, None]
    c = np.arange(SA_WIDTH)[None, :] // HEAD_DIM
    return (h == c).astype(np.float32)


SA_NBLK = ATT_WIDTH // SA_WIDTH


def _stream(kt, vt, prev, batch_lo, n_batches, qx, k_new, v_new, rider, rider_args, rider_in_specs,
            rider_out_specs, rider_out_shapes, name):
    bd, _, n_buf = kt.shape
    t_new = k_new.shape[1]
    assert n_buf == MAX_SPAN, "window buffer must cover the longest pattern (no invalid positions)"
    rows = t_new * SA_HEADS
    buf = pl.BlockSpec((None, SA_WIDTH, n_buf), lambda i: (batch_lo + i // SA_NBLK, i % SA_NBLK, 0))
    new = pl.BlockSpec((None, t_new, SA_WIDTH), lambda i: (batch_lo + i // SA_NBLK, 0, i % SA_NBLK))
    o_blk = pl.BlockSpec((None, t_new, SA_WIDTH), lambda i: (i // SA_NBLK, 0, i % SA_NBLK))
    qx_blk = pl.BlockSpec((None, None, rows, SA_WIDTH), lambda i: (batch_lo + i // SA_NBLK, i % SA_NBLK, 0, 0))
    sd_buf = jax.ShapeDtypeStruct((bd, ATT_WIDTH, n_buf), F32)
    in_specs = [buf, buf, qx_blk, new, new, _const_spec((rows, n_buf + LANES)), _const_spec((rows, SA_WIDTH))]
    in_specs += list(rider_in_specs)
    args = [kt, vt, qx, k_new, v_new, _sa_weights(t_new, n_buf), jnp.asarray(_head_mask(t_new), F32)]
    args += list(rider_args)
    aliases = {}
    if prev is not None:
        aliases = {len(args): 0, len(args) + 1: 1}
        in_specs += [pl.BlockSpec(memory_space=pl.ANY)] * 2
        args += list(prev)
    return pl.pallas_call(
        functools.partial(_stream_kernel, rider=rider, n_rider_in=len(rider_in_specs),
                          n_rider_out=len(rider_out_specs)),
        grid=(n_batches * SA_NBLK,),
        in_specs=in_specs,
        out_specs=[buf, buf, o_blk] + list(rider_out_specs),
        out_shape=[sd_buf, sd_buf, jax.ShapeDtypeStruct((n_batches, t_new, ATT_WIDTH), F32)] + list(rider_out_shapes),
        input_output_aliases=aliases,
        scratch_shapes=[pltpu.VMEM((LANES, SA_WIDTH), F32)],
        compiler_params=_cparams(("arbitrary",)),
        name=name,
    )(*args)


def _expand_queries(q):
    bd, t_new, _ = q.shape
    hm = jnp.asarray(_head_mask(t_new), F32)
    qb = q.reshape(bd, t_new, SA_NBLK, SA_WIDTH).transpose(0, 2, 1, 3)
    return (jnp.tile(qb, (1, 1, SA_HEADS, 1)) * hm).astype(BF16)


def _rider_tiling(b, s, steps):
    assert (b * s) % steps == 0
    tm = b * s // steps
    assert tm % LANES == 0 and s % tm == 0
    return tm, s // tm


def _stream_inproj(kt, vt, batch_lo, n_batches, qx, k_new, v_new, x, scale, shift, g, w_bf, qg, kg):
    b, s, _ = x.shape
    tm, per_seq = _rider_tiling(b, s, n_batches * SA_NBLK)
    keep = min(MAX_SPAN, s)
    assert keep % tm == 0
    kt_first = per_seq - keep // tm
    tok = lambda w: pl.BlockSpec((tm, w), lambda i: (i, 0))
    mod = pl.BlockSpec((None, 1, D_MODEL), lambda i: (i // per_seq, 0, 0))
    out_t = pl.BlockSpec((None, ATT_WIDTH, tm), lambda i: (i // per_seq, 0, jnp.maximum(i % per_seq - kt_first, 0)))
    sd = jax.ShapeDtypeStruct((b * s, ATT_WIDTH), F32)
    sdt = jax.ShapeDtypeStruct((b, ATT_WIDTH, keep), F32)
    outs = _stream(
        kt, vt, None, batch_lo, n_batches, qx, k_new, v_new,
        functools.partial(_inproj_body, kt_rule=(per_seq, kt_first)),
        (x.reshape(b * s, D_MODEL), scale, shift, g, w_bf, qg, kg),
        [tok(D_MODEL), mod, mod, _const_spec((1, D_MODEL)), _const_spec((D_MODEL, 4 * ATT_WIDTH)),
         _const_spec((1, ATT_WIDTH)), _const_spec((1, ATT_WIDTH))],
        [tok(ATT_WIDTH)] * 4 + [out_t] * 2, [sd] * 4 + [sdt] * 2, "stream_inproj")
    okt, ovt, o_s, q, k, v, u, kt_p, vt_p = outs
    r3 = lambda a: a.reshape(b, s, ATT_WIDTH)
    return (okt, ovt), o_s, r3(q), r3(k), r3(v), r3(u), kt_p, vt_p


def _stream_post(kt, vt, prev, batch_lo, n_batches, qx, k_new, v_new, x, o_att, y_ssm, mods, consts, weights):
    b, s, _ = x.shape
    tm, per_seq = _rider_tiling(b, s, n_batches * SA_NBLK)
    tok = lambda w: pl.BlockSpec((tm, w), lambda i: (i, 0))
    mod = pl.BlockSpec((None, 1, D_MODEL), lambda i: (i // per_seq, 0, 0))
    okt, ovt, o_s, y = _stream(
        kt, vt, prev, batch_lo, n_batches, qx, k_new, v_new, _post_body,
        (x.reshape(b * s, D_MODEL), o_att.reshape(b * s, ATT_WIDTH), y_ssm.reshape(b * s, SSM_WIDTH),
         *mods, *consts, *weights),
        [tok(D_MODEL), tok(ATT_WIDTH), tok(SSM_WIDTH), mod, mod, mod, mod]
        + [_const_spec(c.shape) for c in consts] + [_const_spec(w.shape) for w in weights],
        [tok(D_MODEL)], [jax.ShapeDtypeStruct((b * s, D_MODEL), F32)], "stream_post")
    return okt, ovt, o_s, y.reshape(b, s, D_MODEL)


def _s5_tables(a_re, a_im, log_dt, b_re, b_im):
    g, n = a_re.shape
    dt = jnp.exp(log_dt)[:, None]
    x, y = dt * a_re, dt * a_im
    ks = jnp.arange(CHUNK + 1, dtype=F32)
    mag = jnp.exp(x[:, :, None] * ks)
    pw_re, pw_im = mag * jnp.cos(y[:, :, None] * ks), mag * jnp.sin(y[:, :, None] * ks)
    e_re = jnp.expm1(x) * jnp.cos(y) - 2.0 * jnp.sin(0.5 * y) ** 2
    e_im = jnp.exp(x) * jnp.sin(y)
    den = a_re * a_re + a_im * a_im
    f_re = (e_re * a_re + e_im * a_im) / den
    f_im = (e_im * a_re - e_re * a_im) / den
    bb_re = f_re[..., None] * b_re - f_im[..., None] * b_im
    bb_im = f_re[..., None] * b_im + f_im[..., None] * b_re
    return dict(pw_re=pw_re, pw_im=pw_im, bb_re=bb_re, bb_im=bb_im)


def _s5_chunk_operators(tb, c_re, c_im, d_skip):
    pw_re, pw_im, bb_re, bb_im = tb["pw_re"], tb["pw_im"], tb["bb_re"], tb["bb_im"]
    g, n, ch = bb_re.shape
    L = CHUNK
    hp = lax.Precision.HIGHEST
    ct_re, ct_im = c_re.transpose(0, 2, 1), c_im.transpose(0, 2, 1)
    pr, pi = pw_re[:, :, :, None], pw_im[:, :, :, None]
    ca_re = ct_re[:, :, None, :] * pr - ct_im[:, :, None, :] * pi
    ca_im = ct_re[:, :, None, :] * pi + ct_im[:, :, None, :] * pr
    a_cat = jnp.concatenate([ca_re[:, :, :L].reshape(g, n, L * ch), -ca_im[:, :, :L].reshape(g, n, L * ch)], axis=1)
    b_cat = jnp.concatenate([bb_re, bb_im], axis=1)
    base = jnp.einsum("gnd,gnx->gdx", b_cat, a_cat, precision=hp)
    skip = jnp.eye(ch, dtype=F32)[None] * d_skip.reshape(g, ch, 1)
    base = base + jnp.pad(skip, ((0, 0), (0, 0), (0, (L - 1) * ch)))
    toe = jnp.stack([jnp.pad(base, ((0, 0), (0, 0), (s * ch, 0)))[:, :, :L * ch] for s in range(L)], axis=1)
    toe = toe.reshape(g, L * ch, L * ch)
    rv_re = pw_re[:, :, L - 1::-1].transpose(0, 2, 1)[:, :, None, :]
    rv_im = pw_im[:, :, L - 1::-1].transpose(0, 2, 1)[:, :, None, :]
    bt_re, bt_im = bb_re.transpose(0, 2, 1)[:, None], bb_im.transpose(0, 2, 1)[:, None]
    ws_re = (rv_re * bt_re - rv_im * bt_im).reshape(g, L * ch, n)
    ws_im = (rv_re * bt_im + rv_im * bt_re).reshape(g, L * ch, n)
    vr = ca_re[:, :, 1:].reshape(g, n, L * ch)
    vi = (-ca_im[:, :, 1:]).reshape(g, n, L * ch)
    gp = g // 2
    z_w = jnp.zeros((gp, L * ch, n), F32)
    wr, wi = ws_re.reshape(gp, 2, L * ch, n), ws_im.reshape(gp, 2, L * ch, n)
    w_pair = jnp.concatenate([
        jnp.concatenate([wr[:, 0], z_w, wi[:, 0], z_w], axis=2),
        jnp.concatenate([z_w, wr[:, 1], z_w, wi[:, 1]], axis=2)], axis=1)
    z_v = jnp.zeros((gp, n, L * ch), F32)
    vr, vi = vr.reshape(gp, 2, n, L * ch), vi.reshape(gp, 2, n, L * ch)
    v_pair = jnp.concatenate([
        jnp.concatenate([vr[:, 0], z_v], axis=2), jnp.concatenate([z_v, vr[:, 1]], axis=2),
        jnp.concatenate([vi[:, 0], z_v], axis=2), jnp.concatenate([z_v, vi[:, 1]], axis=2)], axis=1)
    a16_re = pw_re[:, :, L].reshape(1, g * n)
    a16_im = pw_im[:, :, L].reshape(1, g * n)
    return toe.astype(BF16), w_pair.astype(BF16), v_pair.astype(BF16), a16_re, a16_im


GROUPS_PER_BLOCK = LANES // SSM_CH
REGROUP_ROWS = 32


def _granule_transpose(arrs):
    n = GROUPS_PER_BLOCK
    gran = lax.broadcasted_iota(jnp.int32, (1, LANES), 1) // SSM_CH
    cur = list(arrs)
    k = n // 2
    while k >= 1:
        keep_low = (gran % (2 * k)) < k
        nxt = list(cur)
        for t in range(n):
            if t % (2 * k) < k:
                a, b = cur[t], cur[t + k]
                nxt[t] = jnp.where(keep_low, a, pltpu.roll(b, k * SSM_CH, axis=1))
                nxt[t + k] = jnp.where(keep_low, pltpu.roll(a, LANES - k * SSM_CH, axis=1), b)
        cur = nxt
        k //= 2
    return cur


def _ssm_a_kernel(u_ref, w_ref, xg_ref, sre_ref, sim_ref, xs_ref):
    nb, s_len, _ = u_ref.shape
    n_chunks = s_len // CHUNK
    half = CHUNK // 2

    for b in range(nb):
        def regroup(ct, c, b=b):
            c0 = pl.multiple_of(ct * REGROUP_ROWS, REGROUP_ROWS)
            rows = pl.ds(b * n_chunks + c0, REGROUP_ROWS)
            for j in range(2):
                z = [u_ref[b, pl.ds(c0 * CHUNK + half * j + tt, REGROUP_ROWS, stride=CHUNK), :]
                     for tt in range(half)]
                for p, xp in enumerate(_granule_transpose(z)):
                    xs_ref[2 * p + j, rows, :] = xp
            return c
        lax.fori_loop(0, n_chunks // REGROUP_ROWS, regroup, 0)

    for col in range(2 * GROUPS_PER_BLOCK):
        xg_ref[:, col * LANES:(col + 1) * LANES] = xs_ref[col].astype(BF16)
    for pp in range(GROUPS_PER_BLOCK // 2):
        xp = xg_ref[:, pp * 4 * LANES:(pp + 1) * 4 * LANES]
        st = jnp.dot(xp, w_ref[pp], preferred_element_type=F32)
        sre_ref[pp] = st[:, 0:LANES]
        sim_ref[pp] = st[:, LANES:2 * LANES]


def _ssm_scan_kernel(sre_ref, sim_ref, are_ref, aim_ref, hre_ref, him_ref, fre_ref, fim_ref, *, nb):
    ncb, rows, _ = sre_ref.shape
    n_chunks = rows // nb
    a_re = [are_ref[cb] for cb in range(ncb)]
    a_im = [aim_ref[cb] for cb in range(ncb)]

    def body(c, carry):
        r = pl.ds(c, nb, stride=n_chunks)
        out = []
        for cb in range(ncb):
            h_re, h_im = carry[2 * cb], carry[2 * cb + 1]
            hre_ref[cb, r, :] = h_re
            him_ref[cb, r, :] = h_im
            out.append(a_re[cb] * h_re - a_im[cb] * h_im + sre_ref[cb, r, :])
            out.append(a_re[cb] * h_im + a_im[cb] * h_re + sim_ref[cb, r, :])
        return tuple(out)

    z = jnp.zeros((nb, LANES), F32)
    fin = lax.fori_loop(0, n_chunks, body, (z,) * (2 * ncb), unroll=4)
    for cb in range(ncb):
        fre_ref[cb] = fin[2 * cb]
        fim_ref[cb] = fin[2 * cb + 1]


def _ssm_c_kernel(xg_ref, hre_ref, him_ref, t_ref, v_ref, y_ref, ys_ref):
    nb, s_len, _ = y_ref.shape
    n_chunks = s_len // CHUNK
    half = CHUNK // 2
    for pp in range(GROUPS_PER_BLOCK // 2):
        hcat = jnp.concatenate([hre_ref[pp], him_ref[pp]], axis=1).astype(BF16)
        inter = jnp.dot(hcat, v_ref[pp], preferred_element_type=F32)
        for e in range(2):
            g = 2 * pp + e
            intra = jnp.dot(xg_ref[:, g * 2 * LANES:(g + 1) * 2 * LANES], t_ref[g],
                            preferred_element_type=F32)
            yg = intra + inter[:, e * 2 * LANES:(e + 1) * 2 * LANES]
            ys_ref[2 * g] = yg[:, 0:LANES]
            ys_ref[2 * g + 1] = yg[:, LANES:2 * LANES]

    for b in range(nb):
        def regroup(ct, c, b=b):
            c0 = pl.multiple_of(ct * REGROUP_ROWS, REGROUP_ROWS)
            rows = pl.ds(b * n_chunks + c0, REGROUP_ROWS)
            for j in range(2):
                yp =[ys_ref[2 * p + j, rows, :] for p in range(GROUPS_PER_BLOCK)]
                for tt, zt in enumerate(_granule_transpose(yp)):
                    y_ref[b, pl.ds(c0 * CHUNK + half * j + tt, REGROUP_ROWS, stride=CHUNK), :] = zt
            return c
        lax.fori_loop(0, n_chunks // REGROUP_ROWS, regroup, 0)


def _ssm_prompt(u, toe, w_pair, v_pair, a16_re, a16_im):
    b, s, _ = u.shape
    rows = b * (s // CHUNK)
    nblk = SSM_WIDTH // LANES
    gw = GROUPS_PER_BLOCK * CHUNK * SSM_CH
    ncb = GROUPS_PER_BLOCK * SSM_STATE // LANES
    ublk = pl.BlockSpec((b, s, LANES), lambda j: (0, 0, j))
    sblk = pl.BlockSpec((ncb, rows, LANES), lambda j: (j, 0, 0))
    sd_h = jax.ShapeDtypeStruct((nblk * ncb, rows, LANES), F32)
    xg, s_re, s_im = pl.pallas_call(
        _ssm_a_kernel,
        grid=(nblk,),
        in_specs=[ublk, pl.BlockSpec((GROUPS_PER_BLOCK // 2, 4 * LANES, 2 * LANES), lambda j: (j, 0, 0))],
        out_specs=[pl.BlockSpec((rows, gw), lambda j: (0, j)), sblk, sblk],
        out_shape=[jax.ShapeDtypeStruct((rows, nblk * gw), BF16), sd_h, sd_h],
        scratch_shapes=[pltpu.VMEM((gw // LANES, rows, LANES), F32)],
        compiler_params=_cparams(("arbitrary",)),
        name="ssm_chunk_states",
    )(u, w_pair)

    ablk = pl.BlockSpec((ncb, 1, LANES), lambda j: (j, 0, 0))
    fblk = pl.BlockSpec((ncb, b, LANES), lambda j: (j, 0, 0))
    sd_f = jax.ShapeDtypeStruct((nblk * ncb, b, LANES), F32)
    a16_re = a16_re.reshape(nblk * ncb, 1, LANES)
    a16_im = a16_im.reshape(nblk * ncb, 1, LANES)
    h_re, h_im, f_re, f_im = pl.pallas_call(
        functools.partial(_ssm_scan_kernel, nb=b),
        grid=(nblk,),
        in_specs=[sblk, sblk, ablk, ablk],
        out_specs=[sblk, sblk, fblk, fblk],
        out_shape=[sd_h, sd_h, sd_f, sd_f],
        compiler_params=_cparams(("arbitrary",)),
        name="ssm_scan",
    )(s_re, s_im, a16_re, a16_im)

    y = pl.pallas_call(
        _ssm_c_kernel,
        grid=(nblk,),
        in_specs=[pl.BlockSpec((rows, gw), lambda j: (0, j)), sblk, sblk,
                  pl.BlockSpec((GROUPS_PER_BLOCK, 2 * LANES, 2 * LANES), lambda j: (j, 0, 0)),
                  pl.BlockSpec((GROUPS_PER_BLOCK // 2, 2 * LANES, 4 * LANES), lambda j: (j, 0, 0))],
        out_specs=ublk,
        out_shape=jax.ShapeDtypeStruct((b, s, SSM_WIDTH), F32),
        scratch_shapes=[pltpu.VMEM((gw // LANES, rows, LANES), F32)],
        compiler_params=_cparams(("arbitrary",)),
        name="ssm_outputs",
    )(xg, h_re, h_im, toe, v_pair)
    to_rows = lambda f: jnp.transpose(f, (1, 0, 2)).reshape(b, nblk * ncb * LANES)
    return y, to_rows(f_re), to_rows(f_im)


def _ssm_sample_kernel(u_ref, hre_ref, him_ref, are_ref, aim_ref, bre_ref, bim_ref, cre_ref, cim_ref, d_ref,
                       y_ref, ore_ref, oim_ref, *, t_new):
    h_re = hre_ref[...].T
    h_im = him_ref[...].T
    a_re, a_im = are_ref[...], aim_ref[...]
    bd = h_re.shape[0]
    for t in range(t_new):
        u = u_ref[t]
        ub = u.astype(BF16)
        n_re = a_re * h_re - a_im * h_im + jnp.dot(ub, bre_ref[...], preferred_element_type=F32)
        n_im = a_re * h_im + a_im * h_re + jnp.dot(ub, bim_ref[...], preferred_element_type=F32)
        h_re, h_im = n_re, n_im
        y = (jnp.dot(h_re.astype(BF16), cre_ref[...], preferred_element_type=F32)
             + jnp.dot(h_im.astype(BF16), cim_ref[...], preferred_element_type=F32) + d_ref[...] * u)
        y_ref[t] = y
    ore_ref[...] = h_re.T
    oim_ref[...] = h_im.T


def _block_diag(m):
    g, r, c = m.shape
    eye = jnp.eye(g, dtype=m.dtype)
    return (eye[:, None, :, None] * m[:, :, None, :]).reshape(g * r, g * c)


def _ssm_sample(u, h0_re_t, h0_im_t, tb, c_re, c_im, d_skip, t_new):
    gn = SSM_GROUPS * SSM_STATE
    a_re = tb["pw_re"][:, :, 1].reshape(1, gn)
    a_im = tb["pw_im"][:, :, 1].reshape(1, gn)
    b_re = _block_diag(tb["bb_re"].transpose(0, 2, 1)).astype(BF16)
    b_im = _block_diag(tb["bb_im"].transpose(0, 2, 1)).astype(BF16)
    cb_re = _block_diag(c_re.transpose(0, 2, 1)).astype(BF16)
    cb_im = _block_diag(-c_im.transpose(0, 2, 1)).astype(BF16)
    bd = u.shape[1]
    full = lambda shape: pl.BlockSpec(shape, lambda i: (0,) * len(shape))
    return pl.pallas_call(
        functools.partial(_ssm_sample_kernel, t_new=t_new),
        grid=(1,),
        in_specs=[full((t_new, bd, SSM_WIDTH)), full((gn, bd)), full((gn, bd)), full((1, gn)), full((1, gn)),
                  full((SSM_WIDTH, gn)), full((SSM_WIDTH, gn)), full((gn, SSM_WIDTH)), full((gn, SSM_WIDTH)),
                  full((1, SSM_WIDTH))],
        out_specs=[full((t_new, bd, SSM_WIDTH)), full((gn, bd)), full((gn, bd))],
        out_shape=[jax.ShapeDtypeStruct((t_new, bd, SSM_WIDTH), F32),
                   jax.ShapeDtypeStruct((gn, bd), F32), jax.ShapeDtypeStruct((gn, bd), F32)],
        compiler_params=_cparams(("arbitrary",)),
        name="ssm_sample",
    )(u, h0_re_t, h0_im_t, a_re, a_im, b_re, b_im, cb_re, cb_im, d_skip.reshape(1, SSM_WIDTH))


def _rms(x, gain):
    return x * lax.rsqrt(jnp.mean(x * x, axis=-1, keepdims=True) + EPS) * gain


def _gelu_tanh(x):
    return 0.5 * x * (1.0 + jnp.tanh(math.sqrt(2.0 / math.pi) * (x + 0.044715 * (x * x * x))))


def _post_body(x_ref, oa_ref, ys_ref, g1_ref, sc2_ref, sh2_ref, g2_ref, n2_ref, ag_ref, sg_ref,
               wglu_ref, wout_ref, wg_ref, wu_ref, wd_ref, o_ref):
    ya = _gelu_tanh(ys_ref[...])
    ya = ya * _sigmoid(jnp.dot(ya.astype(BF16), wglu_ref[...], preferred_element_type=F32))
    merged = jnp.concatenate([_rms(oa_ref[...], ag_ref[...]), _rms(ya, sg_ref[...])], axis=1)
    x1 = x_ref[...] + g1_ref[...] * jnp.dot(merged.astype(BF16), wout_ref[...], preferred_element_type=F32)
    h2 = (_rms(x1, n2_ref[...]) * (1.0 + sc2_ref[...]) + sh2_ref[...]).astype(BF16)
    gate = jnp.dot(h2, wg_ref[...], preferred_element_type=F32)
    up = jnp.dot(h2, wu_ref[...], preferred_element_type=F32)
    act = (gate * _sigmoid(gate) * up).astype(BF16)
    o_ref[...] = x1 + g2_ref[...] * jnp.dot(act, wd_ref[...], preferred_element_type=F32)


def _post_sample(x, o_att, y_ssm, mods, consts, weights):
    t = x.shape[0]
    tm = min(TOKEN_TILE, t)
    tok = lambda w: pl.BlockSpec((tm, w), lambda i: (i, 0))
    mod = tok(D_MODEL)
    return pl.pallas_call(
        _post_body,
        grid=(t // tm,),
        in_specs=[tok(D_MODEL), tok(ATT_WIDTH), tok(SSM_WIDTH), mod, mod, mod, mod]
                 + [_const_spec(c.shape) for c in consts] + [_const_spec(w.shape) for w in weights],
        out_specs=tok(D_MODEL),
        out_shape=jax.ShapeDtypeStruct((t, D_MODEL), F32),
        compiler_params=_cparams(("arbitrary",)),
        name="post_sample",
    )(x, o_att, y_ssm, *mods, *consts, *weights)


def kernel(x_prompt, x_sample, cache_k, cache_v, state_ssm_re, state_ssm_im, c_prompt, c_sample, norm1_g, norm2_g, w_ada, b_ada, w_in, q_gain, k_gain, ssm_a_re, ssm_a_im, ssm_log_dt, ssm_b_re, ssm_b_im, ssm_c_re, ssm_c_im, ssm_d, w_glu, attn_out_g, ssm_out_g, w_out, w_gate, w_up, w_down):
    depth = norm1_g.shape[0]
    assert depth == 1, "one decoder layer"
    b, s, _ = x_prompt.shape
    bd, t_new, _ = x_sample.shape
    n_buf = cache_k.shape[2]
    L = 0

    n_c = b + bd
    pad = (-n_c) % SUBLANES
    c_all = jnp.concatenate([c_prompt, c_sample, jnp.zeros((pad, D_MODEL), F32)], axis=0)
    mod = _ada(c_all, w_ada[L], b_ada[L].reshape(1, -1))
    mod_p = [mod[0:b, i * D_MODEL:(i + 1) * D_MODEL].reshape(b, 1, D_MODEL) for i in range(N_MOD)]
    mod_s = [jnp.repeat(mod[b:n_c, i * D_MODEL:(i + 1) * D_MODEL], t_new, axis=0) for i in range(N_MOD)]

    w_in_bf = w_in[L].astype(BF16)
    n1 = norm1_g[L].reshape(1, D_MODEL)
    qg = jnp.tile(q_gain[L], ATT_HEADS).reshape(1, ATT_WIDTH)
    kg = jnp.tile(k_gain[L], ATT_HEADS).reshape(1, ATT_WIDTH)
    consts = (norm2_g[L].reshape(1, D_MODEL), attn_out_g[L].reshape(1, ATT_WIDTH), ssm_out_g[L].reshape(1, SSM_WIDTH))
    weights = tuple(w[L].astype(BF16) for w in (w_glu, w_out, w_gate, w_up, w_down))

    tb = _s5_tables(ssm_a_re[L], ssm_a_im[L], ssm_log_dt[L], ssm_b_re[L], ssm_b_im[L])
    toe, w_pair, v_pair, a16_re, a16_im = _s5_chunk_operators(tb, ssm_c_re[L], ssm_c_im[L], ssm_d[L])

    xs = x_sample.reshape(bd * t_new, D_MODEL)
    qs, ks, vs, us = _inproj_sample(xs, mod_s[1], mod_s[0], n1, w_in_bf, qg, kg)
    qx = _expand_queries(qs.reshape(bd, t_new, ATT_WIDTH))
    ks3, vs3 = ks.reshape(bd, t_new, ATT_WIDTH), vs.reshape(bd, t_new, ATT_WIDTH)
    kt = jnp.transpose(cache_k[L], (0, 2, 3, 1)).reshape(bd, ATT_WIDTH, n_buf)
    vt = jnp.transpose(cache_v[L], (0, 2, 3, 1)).reshape(bd, ATT_WIDTH, n_buf)
    n_first = bd // 2
    shifted, o_s_first, q, k, v, u, kt_p, vt_p = _stream_inproj(
        kt, vt, 0, n_first, qx, ks3, vs3, x_prompt, mod_p[1], mod_p[0], n1, w_in_bf, qg, kg)
    o_att = _attn_prompt(q, k, v)
    y_ssm, f_re, f_im = _ssm_prompt(u, toe, w_pair, v_pair, a16_re, a16_im)
    okt, ovt, o_s_second, y_prompt = _stream_post(
        kt, vt, shifted, n_first, bd - n_first, qx, ks3, vs3, x_prompt, o_att, y_ssm,
        (mod_p[2], mod_p[4], mod_p[3], mod_p[5]), consts, weights)
    o_att_s = jnp.concatenate([o_s_first, o_s_second], axis=0)
    gn = SSM_GROUPS * SSM_STATE
    h0_re = jnp.transpose(state_ssm_re[L], (1, 2, 0)).reshape(gn, bd)
    h0_im = jnp.transpose(state_ssm_im[L], (1, 2, 0)).reshape(gn, bd)
    us_t = jnp.transpose(us.reshape(bd, t_new, SSM_WIDTH), (1, 0, 2))
    y_ssm_t, hs_re, hs_im = _ssm_sample(us_t, h0_re, h0_im, tb, ssm_c_re[L], ssm_c_im[L], ssm_d[L], t_new)
    y_ssm_s = jnp.transpose(y_ssm_t, (1, 0, 2)).reshape(bd * t_new, SSM_WIDTH)
    y_sample = _post_sample(xs, o_att_s.reshape(bd * t_new, ATT_WIDTH), y_ssm_s,
                            (mod_s[2], mod_s[4], mod_s[3], mod_s[5]), consts, weights)

    def from_t(a, nb, keep):
        return jnp.transpose(a.reshape(nb, ATT_HEADS, HEAD_DIM, keep), (0, 3, 1, 2))[None]

    def state_from_t(a):
        return jnp.transpose(a.reshape(SSM_GROUPS, SSM_STATE, bd), (2, 0, 1))[None]

    keep = min(MAX_SPAN, s)
    return (y_prompt, y_sample.reshape(bd, t_new, D_MODEL),
            from_t(kt_p, b, keep), from_t(vt_p, b, keep),
            f_re.reshape(b, SSM_GROUPS, SSM_STATE)[None], f_im.reshape(b, SSM_GROUPS, SSM_STATE)[None],
            from_t(okt, bd, n_buf), from_t(ovt, bd, n_buf),
            state_from_t(hs_re), state_from_t(hs_im))
```

```python
import functools
import math

import jax
import jax.numpy as jnp
import numpy as np
from jax import lax
from jax.experimental import pallas as pl
from jax.experimental.pallas import tpu as pltpu

F32 = jnp.float32
BF16 = jnp.bfloat16

D_MODEL = 1024
HEAD_DIM = 64
ATT_WIDTH = 512
ATT_HEADS = 8
SSM_WIDTH = 512
SSM_CH = 16
SSM_GROUPS = 32
SSM_STATE = 64
DILATIONS = ((128, 1), (512, 4), (2048, 16))
N_BACK = 128
MAX_SPAN = 2048
FFN_HIDDEN = 2816
N_MOD = 6
EPS = 1e-6

LANES = 128
SUBLANES = 8
VMEM_LIMIT = 56 * 1024 * 1024

TOKEN_TILE = 512
SUPER = 2048
QB = 128
TILE_GROUP = 4
Q_SCALE = HEAD_DIM ** -0.5 * math.log2(math.e)
CHUNK = 16
NEG = -1e30


def _cparams(sem=None):
    return pltpu.CompilerParams(dimension_semantics=sem, vmem_limit_bytes=VMEM_LIMIT)


def _const_spec(shape):
    nd = len(shape)
    return pl.BlockSpec(shape, lambda *_: (0,) * nd, pipeline_mode=pl.Buffered(1))


def _sigmoid(x):
    return 1.0 / (1.0 + jnp.exp(-x))


def _split_bf16(a):
    hi = a.astype(BF16)
    lo = (a - hi.astype(F32)).astype(BF16)
    return hi, lo


def _ada_kernel(c_ref, w_ref, b_ref, o_ref):
    c = c_ref[...]
    a = c * _sigmoid(c)
    a_hi, a_lo = _split_bf16(a)
    w_hi, w_lo = _split_bf16(w_ref[...])
    acc = jnp.dot(a_hi, w_hi, preferred_element_type=F32)
    acc += jnp.dot(a_hi, w_lo, preferred_element_type=F32)
    acc += jnp.dot(a_lo, w_hi, preferred_element_type=F32)
    o_ref[...] = acc + b_ref[...]


def _ada(c_all, w_ada, b_ada):
    rows = c_all.shape[0]
    n = w_ada.shape[1]
    tn = 1024
    return pl.pallas_call(
        _ada_kernel,
        grid=(n // tn,),
        in_specs=[pl.BlockSpec((rows, D_MODEL), lambda j: (0, 0)),
                  pl.BlockSpec((D_MODEL, tn), lambda j: (0, j)),
                  pl.BlockSpec((1, tn), lambda j: (0, j))],
        out_specs=pl.BlockSpec((rows, tn), lambda j: (0, j)),
        out_shape=jax.ShapeDtypeStruct((rows, n), F32),
        compiler_params=_cparams(("arbitrary",)),
        name="ada",
    )(c_all, w_ada, b_ada)


def _head_rms(z, gain):
    lane = lax.broadcasted_iota(jnp.int32, (1, LANES), 1)
    lo = lane < HEAD_DIM
    outs = []
    for c in range(z.shape[1] // LANES):
        blk = z[:, c * LANES:(c + 1) * LANES]
        sq = blk * blk
        s_lo = jnp.sum(jnp.where(lo, sq, 0.0), axis=-1, keepdims=True)
        s_hi = jnp.sum(jnp.where(lo, 0.0, sq), axis=-1, keepdims=True)
        inv = jnp.where(lo, lax.rsqrt(s_lo * (1.0 / HEAD_DIM) + EPS), lax.rsqrt(s_hi * (1.0 / HEAD_DIM) + EPS))
        outs.append(blk * inv)
    return jnp.concatenate(outs, axis=1) * gain


def _inproj_body(x_ref, scale_ref, shift_ref, g_ref, w_ref, qg_ref, kg_ref, *out_refs, kt_rule=None):
    q_ref, k_ref, v_ref, u_ref = out_refs[:4]
    x = x_ref[...]
    ms = jnp.mean(x * x, axis=-1, keepdims=True)
    h = x * lax.rsqrt(ms + EPS) * g_ref[...]
    h = h * (1.0 + scale_ref[...]) + shift_ref[...]
    z = jnp.dot(h.astype(BF16), w_ref[...], preferred_element_type=F32)
    q = _head_rms(z[:, 0:ATT_WIDTH], qg_ref[...]) * Q_SCALE
    k = _head_rms(z[:, ATT_WIDTH:2 * ATT_WIDTH], kg_ref[...])
    v = z[:, 2 * ATT_WIDTH:3 * ATT_WIDTH]
    q_ref[...] = q
    k_ref[...] = k
    v_ref[...] = v
    u_ref[...] = z[:, 3 * ATT_WIDTH:]
    if kt_rule is not None:
        per_seq, kt_first = kt_rule
        kt_ref, vt_ref = out_refs[4:]

        @pl.when(pl.program_id(0) % per_seq >= kt_first)
        def _():
            kt_ref[...] = k.T
            vt_ref[...] = v.T


def _inproj_sample(x, scale, shift, g, w_bf, qg, kg):
    t = x.shape[0]
    tm = min(TOKEN_TILE, t)
    tok = pl.BlockSpec((tm, D_MODEL), lambda i: (i, 0))
    out_tok = pl.BlockSpec((tm, ATT_WIDTH), lambda i: (i, 0))
    sd = jax.ShapeDtypeStruct((t, ATT_WIDTH), F32)
    return pl.pallas_call(
        _inproj_body,
        grid=(t // tm,),
        in_specs=[tok, tok, tok, _const_spec((1, D_MODEL)), _const_spec((D_MODEL, 4 * ATT_WIDTH)),
                  _const_spec((1, ATT_WIDTH)), _const_spec((1, ATT_WIDTH))],
        out_specs=[out_tok] * 4,
        out_shape=[sd] * 4,
        compiler_params=_cparams(("arbitrary",)),
        name="inproj_sample",
    )(x, scale, shift, g, w_bf, qg, kg)


ATTN_PHASES = len(DILATIONS) + 1


def _attn_rider(q_ref, kp_ref, kc_ref, vp_ref, vc_ref, bias_ref, o_ref, kbuf, vbuf, acc_ref, m_ref, l_ref, *,
                n_super):
    step = pl.program_id(0)
    phase = step % ATTN_PHASES
    st = (step // ATTN_PHASES) % n_super
    lane = lax.broadcasted_iota(jnp.int32, (1, LANES), 1)
    lo = lane < HEAD_DIM
    nt_contract = (((1,), (1,)), ((), ()))

    def pattern(p, d):
        per_res = SUPER // d // QB

        def group_body(gi, c):
            tiles = []
            for u in range(TILE_GROUP):
                ti = gi * TILE_GROUP + u
                r = ti // per_res
                jt = ti % per_res
                row0 = r + d * QB * jt
                ks = SUPER + row0 - d * QB
                first = jnp.logical_and(st == 0, jt == 0)
                bias = bias_ref[jnp.where(first, 0, 1)]
                q = q_ref[pl.ds(row0, QB, stride=d), :]
                k = kbuf[pl.ds(ks, 2 * QB, stride=d), :].astype(BF16)
                v = vbuf[pl.ds(ks, 2 * QB, stride=d), :].astype(BF16)
                q2 = jnp.concatenate([jnp.where(lo, q, 0.0), jnp.where(lo, 0.0, q)], axis=0).astype(BF16)
                s = lax.dot_general(q2, k, nt_contract, preferred_element_type=F32)
                tiles.append((s, bias, v, row0))
            probs = []
            for s, bias, v, row0 in tiles:
                s = s + jnp.concatenate([bias, bias], axis=0)
                m = jnp.max(s, axis=-1, keepdims=True)
                pr = jnp.exp2(s - m)
                l = jnp.sum(pr, axis=-1, keepdims=True)
                probs.append((pr.astype(BF16), m, l))
            for (s, bias, v, row0), (pr, m, l) in zip(tiles, probs):
                pv = jnp.dot(pr, v, preferred_element_type=F32)
                rows = pl.ds(row0, QB, stride=d)
                acc_ref[p, rows, :] = jnp.where(lo, pv[0:QB], pv[QB:2 * QB])
                m_ref[p, rows, :] = jnp.where(lo, m[0:QB], m[QB:2 * QB])
                l_ref[p, rows, :] = jnp.where(lo, l[0:QB], l[QB:2 * QB])
            return c

        lax.fori_loop(0, SUPER // QB // TILE_GROUP, group_body, 0)

    @pl.when(phase == 0)
    def _():
        kbuf[0:SUPER, :] = kp_ref[...]
        kbuf[SUPER:2 * SUPER, :] = kc_ref[...]
        vbuf[0:SUPER, :] = vp_ref[...]
        vbuf[SUPER:2 * SUPER, :] = vc_ref[...]

    for p, (_, d) in enumerate(DILATIONS):
        pl.when(phase == p)(functools.partial(pattern, p, d))

    @pl.when(phase == ATTN_PHASES - 1)
    def _():
        def merge_body(ci, c):
            rows = pl.ds(pl.multiple_of(ci * QB, QB), QB)
            m1, m2, m3 = m_ref[0, rows, :], m_ref[1, rows, :], m_ref[2, rows, :]
            mm = jnp.maximum(jnp.maximum(m1, m2), m3)
            w1, w2, w3 = jnp.exp2(m1 - mm), jnp.exp2(m2 - mm), jnp.exp2(m3 - mm)
            num = w1 * acc_ref[0, rows, :] + w2 * acc_ref[1, rows, :] + w3 * acc_ref[2, rows, :]
            den = w1 * l_ref[0, rows, :] + w2 * l_ref[1, rows, :] + w3 * l_ref[2, rows, :]
            o_ref[rows, :] = num / den
            return c

        lax.fori_loop(0, SUPER // QB, merge_body, 0)


def _attn_bias():
    qi = np.arange(QB)[:, None]
    ki = np.arange(2 * QB)[None, :]
    dist = qi + QB - ki
    normal = (dist >= 0) & (dist <= N_BACK)
    first = normal & (ki >= QB)
    return jnp.asarray(np.where(np.stack([first, normal]), 0.0, NEG), F32)


SA_HEADS = 4
SA_WIDTH = SA_HEADS * HEAD_DIM
N_SA_IN = 7


def _stream_kernel(*refs, rider, n_rider_in, n_rider_out, n_prev, n_rider_scratch):
    kt_ref, vt_ref, qx_ref, kn_ref, vn_ref, w_ref, hm_ref = refs[:N_SA_IN]
    rider_in = refs[N_SA_IN:N_SA_IN + n_rider_in]
    n_in = N_SA_IN + n_rider_in + n_prev
    okt_ref, ovt_ref, o_ref = refs[n_in:n_in + 3]
    rider_out = refs[n_in + 3:n_in + 3 + n_rider_out]
    pad_ref = refs[n_in + 3 + n_rider_out]
    rider_scratch = refs[len(refs) - n_rider_scratch:] if n_rider_scratch else ()
    n_buf = kt_ref.shape[1]
    t_new = kn_ref.shape[0]
    ext = n_buf + LANES

    def extend(buf_ref, new_ref):
        pad_ref[...] = jnp.zeros_like(pad_ref)
        pad_ref[0:t_new, :] = new_ref[...]
        return jnp.concatenate([buf_ref[...], pad_ref[...].T], axis=1)

    k_ext = extend(kt_ref, kn_ref)
    v_ext = extend(vt_ref, vn_ref)
    w = w_ref[...]
    s = jnp.dot(qx_ref[...], k_ext.astype(BF16), preferred_element_type=F32)
    s = jnp.where(w > 0.0, s, NEG)
    m = jnp.max(s, axis=-1, keepdims=True)
    e = w * jnp.exp2(s - m)
    l = jnp.sum(e, axis=-1, keepdims=True)
    pv = lax.dot_general(e.astype(BF16), v_ext.astype(BF16), (((1,), (1,)), ((), ())),
                         preferred_element_type=F32)
    pv = pv * hm_ref[...] / l
    o = pv[0:t_new]
    for h in range(1, SA_HEADS):
        o = o + pv[h * t_new:(h + 1) * t_new]
    o_ref[...] = o

    rider(*rider_in, *rider_out, *rider_scratch)

    okt_ref[...] = pltpu.roll(k_ext, ext - t_new, axis=1)[:, 0:n_buf]
    ovt_ref[...] = pltpu.roll(v_ext, ext - t_new, axis=1)[:, 0:n_buf]


def _sa_weights(t_new, n_buf):
    ext = n_buf + LANES
    pos = np.arange(ext)[None, :]
    i = np.arange(t_new)[:, None]
    delta = n_buf + i - pos
    is_real = (pos < n_buf + t_new)
    w = np.zeros((t_new, ext), np.float32)
    for window, d in DILATIONS:
        w += ((delta >= 0) & (delta % d == 0) & (delta <= window) & is_real)
    return jnp.asarray(np.tile(w, (SA_HEADS, 1)), F32)


def _head_mask(t_new):
    h = np.repeat(np.arange(SA_HEADS), t_new)[:, None]
    c = np.arange(SA_WIDTH)[None, :] // HEAD_DIM
    return (h == c).astype(np.float32)


SA_NBLK = ATT_WIDTH // SA_WIDTH


def _stream(kt, vt, prev, batch_lo, n_batches, qx, k_new, v_new, rider, rider_args, rider_in_specs,
            rider_out_specs, rider_out_shapes, name, rider_scratch=()):
    bd, _, n_buf = kt.shape
    t_new = k_new.shape[1]
    assert n_buf == MAX_SPAN, "window buffer must cover the longest pattern (no invalid positions)"
    rows = t_new * SA_HEADS
    buf = pl.BlockSpec((None, SA_WIDTH, n_buf), lambda i: (batch_lo + i // SA_NBLK, i % SA_NBLK, 0))
    new = pl.BlockSpec((None, t_new, SA_WIDTH), lambda i: (batch_lo + i // SA_NBLK, 0, i % SA_NBLK))
    o_blk = pl.BlockSpec((None, t_new, SA_WIDTH), lambda i: (i // SA_NBLK, 0, i % SA_NBLK))
    qx_blk = pl.BlockSpec((None, None, rows, SA_WIDTH), lambda i: (batch_lo + i // SA_NBLK, i % SA_NBLK, 0, 0))
    sd_buf = jax.ShapeDtypeStruct((bd, ATT_WIDTH, n_buf), F32)
    in_specs = [buf, buf, qx_blk, new, new, _const_spec((rows, n_buf + LANES)), _const_spec((rows, SA_WIDTH))]
    in_specs += list(rider_in_specs)
    args = [kt, vt, qx, k_new, v_new, _sa_weights(t_new, n_buf), jnp.asarray(_head_mask(t_new), F32)]
    args += list(rider_args)
    aliases = {}
    if prev is not None:
        aliases = {len(args): 0, len(args) + 1: 1}
        in_specs += [pl.BlockSpec(memory_space=pl.ANY)] * 2
        args += list(prev)
    return pl.pallas_call(
        functools.partial(_stream_kernel, rider=rider, n_rider_in=len(rider_in_specs),
                          n_rider_out=len(rider_out_specs), n_prev=len(aliases),
                          n_rider_scratch=len(rider_scratch)),
        grid=(n_batches * SA_NBLK,),
        in_specs=in_specs,
        out_specs=[buf, buf, o_blk] + list(rider_out_specs),
        out_shape=[sd_buf, sd_buf, jax.ShapeDtypeStruct((n_batches, t_new, ATT_WIDTH), F32)] + list(rider_out_shapes),
        input_output_aliases=aliases,
        scratch_shapes=[pltpu.VMEM((LANES, SA_WIDTH), F32)] + list(rider_scratch),
        compiler_params=_cparams(("arbitrary",)),
        name=name,
    )(*args)


def _expand_queries(q):
    bd, t_new, _ = q.shape
    hm = jnp.asarray(_head_mask(t_new), F32)
    qb = q.reshape(bd, t_new, SA_NBLK, SA_WIDTH).transpose(0, 2, 1, 3)
    return (jnp.tile(qb, (1, 1, SA_HEADS, 1)) * hm).astype(BF16)


def _rider_tiling(b, s, steps):
    assert (b * s) % steps == 0
    tm = b * s // steps
    assert tm % LANES == 0 and s % tm == 0
    return tm, s // tm


def _inproj_prompt(x, scale, shift, g, w_bf, qg, kg):
    b, s, _ = x.shape
    tm = TOKEN_TILE
    per_seq = s // tm
    keep = min(MAX_SPAN, s)
    kt_first = per_seq - keep // tm
    tok = lambda w: pl.BlockSpec((tm, w), lambda i: (i, 0))
    mod = pl.BlockSpec((None, 1, D_MODEL), lambda i: (i // per_seq, 0, 0))
    out_t = pl.BlockSpec((None, ATT_WIDTH, tm), lambda i: (i // per_seq, 0, jnp.maximum(i % per_seq - kt_first, 0)))
    sd = jax.ShapeDtypeStruct((b * s, ATT_WIDTH), F32)
    sdt = jax.ShapeDtypeStruct((b, ATT_WIDTH, keep), F32)
    q, k, v, u, kt_p, vt_p = pl.pallas_call(
        functools.partial(_inproj_body, kt_rule=(per_seq, kt_first)),
        grid=(b * per_seq,),
        in_specs=[tok(D_MODEL), mod, mod, _const_spec((1, D_MODEL)), _const_spec((D_MODEL, 4 * ATT_WIDTH)),
                  _const_spec((1, ATT_WIDTH)), _const_spec((1, ATT_WIDTH))],
        out_specs=[tok(ATT_WIDTH)] * 4 + [out_t] * 2,
        out_shape=[sd] * 4 + [sdt] * 2,
        compiler_params=_cparams(("arbitrary",)),
        name="inproj_prompt",
    )(x.reshape(b * s, D_MODEL), scale, shift, g, w_bf, qg, kg)
    r3 = lambda a: a.reshape(b, s, ATT_WIDTH)
    return r3(q), r3(k), r3(v), r3(u), kt_p, vt_p


def _attn_stream_batches(b, s):
    n_super = s // SUPER
    units = b * (ATT_WIDTH // LANES) * n_super
    steps = units * ATTN_PHASES
    assert steps % SA_NBLK == 0
    return steps // SA_NBLK


def _stream_attn(kt, vt, batch_lo, n_batches, qx, k_new, v_new, q, k, v):
    b, s, _ = q.shape
    assert s % SUPER == 0
    n_super = s // SUPER
    n_hp = ATT_WIDTH // LANES
    assert n_batches == _attn_stream_batches(b, s)

    def unit(i):
        un = i // ATTN_PHASES
        return un // (n_hp * n_super), un % n_super, (un // n_super) % n_hp

    def cur(i):
        bi, st, hp = unit(i)
        return bi, st, hp

    def prev(i):
        bi, st, hp = unit(i)
        return bi, jnp.maximum(st - 1, 0), hp

    blk_c = pl.BlockSpec((None, SUPER, LANES), cur)
    blk_p = pl.BlockSpec((None, SUPER, LANES), prev)
    outs = _stream(
        kt, vt, None, batch_lo, n_batches, qx, k_new, v_new,
        functools.partial(_attn_rider, n_super=n_super),
        (q, k, k, v, v, _attn_bias()),
        [blk_c, blk_p, blk_c, blk_p, blk_c, _const_spec((2, QB, 2 * QB))],
        [blk_c], [jax.ShapeDtypeStruct((b, s, ATT_WIDTH), F32)], "stream_attn",
        rider_scratch=[pltpu.VMEM((2 * SUPER, LANES), F32)] * 2 + [pltpu.VMEM((len(DILATIONS), SUPER, LANES), F32)] * 3)
    okt, ovt, o_s, o_att = outs
    return (okt, ovt), o_s, o_att


def _stream_post(kt, vt, prev, batch_lo, n_batches, qx, k_new, v_new, x, o_att, y_ssm, mods, consts, weights):
    b, s, _ = x.shape
    tm, per_seq = _rider_tiling(b, s, n_batches * SA_NBLK)
    tok = lambda w: pl.BlockSpec((tm, w), lambda i: (i, 0))
    mod = pl.BlockSpec((None, 1, D_MODEL), lambda i: (i // per_seq, 0, 0))
    okt, ovt, o_s, y = _stream(
        kt, vt, prev, batch_lo, n_batches, qx, k_new, v_new, _post_body,
        (x.reshape(b * s, D_MODEL), o_att.reshape(b * s, ATT_WIDTH), y_ssm.reshape(b * s, SSM_WIDTH),
         *mods, *consts, *weights),
        [tok(D_MODEL), tok(ATT_WIDTH), tok(SSM_WIDTH), mod, mod, mod, mod]
        + [_const_spec(c.shape) for c in consts] + [_const_spec(w.shape) for w in weights],
        [tok(D_MODEL)], [jax.ShapeDtypeStruct((b * s, D_MODEL), F32)], "stream_post")
    return okt, ovt, o_s, y.reshape(b, s, D_MODEL)


def _s5_tables(a_re, a_im, log_dt, b_re, b_im):
    g, n = a_re.shape
    dt = jnp.exp(log_dt)[:, None]
    x, y = dt * a_re, dt * a_im
    ks = jnp.arange(CHUNK + 1, dtype=F32)
    mag = jnp.exp(x[:, :, None] * ks)
    pw_re, pw_im = mag * jnp.cos(y[:, :, None] * ks), mag * jnp.sin(y[:, :, None] * ks)
    e_re = jnp.expm1(x) * jnp.cos(y) - 2.0 * jnp.sin(0.5 * y) ** 2
    e_im = jnp.exp(x) * jnp.sin(y)
    den = a_re * a_re + a_im * a_im
    f_re = (e_re * a_re + e_im * a_im) / den
    f_im = (e_im * a_re - e_re * a_im) / den
    bb_re = f_re[..., None] * b_re - f_im[..., None] * b_im
    bb_im = f_re[..., None] * b_im + f_im[..., None] * b_re
    return dict(pw_re=pw_re, pw_im=pw_im, bb_re=bb_re, bb_im=bb_im)


def _s5_chunk_operators(tb, c_re, c_im, d_skip):
    pw_re, pw_im, bb_re, bb_im = tb["pw_re"], tb["pw_im"], tb["bb_re"], tb["bb_im"]
    g, n, ch = bb_re.shape
    L = CHUNK
    hp = lax.Precision.HIGHEST
    ct_re, ct_im = c_re.transpose(0, 2, 1), c_im.transpose(0, 2, 1)
    pr, pi = pw_re[:, :, :, None], pw_im[:, :, :, None]
    ca_re = ct_re[:, :, None, :] * pr - ct_im[:, :, None, :] * pi
    ca_im = ct_re[:, :, None, :] * pi + ct_im[:, :, None, :] * pr
    a_cat = jnp.concatenate([ca_re[:, :, :L].reshape(g, n, L * ch), -ca_im[:, :, :L].reshape(g, n, L * ch)], axis=1)
    b_cat = jnp.concatenate([bb_re, bb_im], axis=1)
    base = jnp.einsum("gnd,gnx->gdx", b_cat, a_cat, precision=hp)
    skip = jnp.eye(ch, dtype=F32)[None] * d_skip.reshape(g, ch, 1)
    base = base + jnp.pad(skip, ((0, 0), (0, 0), (0, (L - 1) * ch)))
    toe = jnp.stack([jnp.pad(base, ((0, 0), (0, 0), (s * ch, 0)))[:, :, :L * ch] for s in range(L)], axis=1)
    toe = toe.reshape(g, L * ch, L * ch)
    rv_re = pw_re[:, :, L - 1::-1].transpose(0, 2, 1)[:, :, None, :]
    rv_im = pw_im[:, :, L - 1::-1].transpose(0, 2, 1)[:, :, None, :]
    bt_re, bt_im = bb_re.transpose(0, 2, 1)[:, None], bb_im.transpose(0, 2, 1)[:, None]
    ws_re = (rv_re * bt_re - rv_im * bt_im).reshape(g, L * ch, n)
    ws_im = (rv_re * bt_im + rv_im * bt_re).reshape(g, L * ch, n)
    vr = ca_re[:, :, 1:].reshape(g, n, L * ch)
    vi = (-ca_im[:, :, 1:]).reshape(g, n, L * ch)
    gp = g // 2
    z_w = jnp.zeros((gp, L * ch, n), F32)
    wr, wi = ws_re.reshape(gp, 2, L * ch, n), ws_im.reshape(gp, 2, L * ch, n)
    w_pair = jnp.concatenate([
        jnp.concatenate([wr[:, 0], z_w, wi[:, 0], z_w], axis=2),
        jnp.concatenate([z_w, wr[:, 1], z_w, wi[:, 1]], axis=2)], axis=1)
    z_v = jnp.zeros((gp, n, L * ch), F32)
    vr, vi = vr.reshape(gp, 2, n, L * ch), vi.reshape(gp, 2, n, L * ch)
    v_pair = jnp.concatenate([
        jnp.concatenate([vr[:, 0], z_v], axis=2), jnp.concatenate([z_v, vr[:, 1]], axis=2),
        jnp.concatenate([vi[:, 0], z_v], axis=2), jnp.concatenate([z_v, vi[:, 1]], axis=2)], axis=1)
    a16_re = pw_re[:, :, L].reshape(1, g * n)
    a16_im = pw_im[:, :, L].reshape(1, g * n)
    return toe.astype(BF16), w_pair.astype(BF16), v_pair.astype(BF16), a16_re, a16_im


GROUPS_PER_BLOCK = LANES // SSM_CH
REGROUP_ROWS = 32


def _granule_transpose(arrs):
    n = GROUPS_PER_BLOCK
    gran = lax.broadcasted_iota(jnp.int32, (1, LANES), 1) // SSM_CH
    cur = list(arrs)
    k = n // 2
    while k >= 1:
        keep_low = (gran % (2 * k)) < k
        nxt = list(cur)
        for t in range(n):
            if t % (2 * k) < k:
                a, b = cur[t], cur[t + k]
                nxt[t] = jnp.where(keep_low, a, pltpu.roll(b, k * SSM_CH, axis=1))
                nxt[t + k] = jnp.where(keep_low, pltpu.roll(a, LANES - k * SSM_CH, axis=1), b)
        cur = nxt
        k //= 2
    return cur


def _ssm_a_kernel(u_ref, w_ref, xg_ref, sre_ref, sim_ref, xs_ref):
    nb, s_len, _ = u_ref.shape
    n_chunks = s_len // CHUNK
    half = CHUNK // 2

    for b in range(nb):
        def regroup(ct, c, b=b):
            c0 = pl.multiple_of(ct * REGROUP_ROWS, REGROUP_ROWS)
            rows = pl.ds(b * n_chunks + c0, REGROUP_ROWS)
            for j in range(2):
                z = [u_ref[b, pl.ds(c0 * CHUNK + half * j + tt, REGROUP_ROWS, stride=CHUNK), :]
                     for tt in range(half)]
                for p, xp in enumerate(_granule_transpose(z)):
                    xs_ref[2 * p + j, rows, :] = xp
            return c
        lax.fori_loop(0, n_chunks // REGROUP_ROWS, regroup, 0)

    for col in range(2 * GROUPS_PER_BLOCK):
        xg_ref[:, col * LANES:(col + 1) * LANES] = xs_ref[col].astype(BF16)
    for pp in range(GROUPS_PER_BLOCK // 2):
        xp = xg_ref[:, pp * 4 * LANES:(pp + 1) * 4 * LANES]
        st = jnp.dot(xp, w_ref[pp], preferred_element_type=F32)
        sre_ref[pp] = st[:, 0:LANES]
        sim_ref[pp] = st[:, LANES:2 * LANES]


def _ssm_scan_kernel(sre_ref, sim_ref, are_ref, aim_ref, hre_ref, him_ref, fre_ref, fim_ref, *, nb):
    ncb, rows, _ = sre_ref.shape
    n_chunks = rows // nb
    a_re = [are_ref[cb] for cb in range(ncb)]
    a_im = [aim_ref[cb] for cb in range(ncb)]

    def body(c, carry):
        r = pl.ds(c, nb, stride=n_chunks)
        out = []
        for cb in range(ncb):
            h_re, h_im = carry[2 * cb], carry[2 * cb + 1]
            hre_ref[cb, r, :] = h_re
            him_ref[cb, r, :] = h_im
            out.append(a_re[cb] * h_re - a_im[cb] * h_im + sre_ref[cb, r, :])
            out.append(a_re[cb] * h_im + a_im[cb] * h_re + sim_ref[cb, r, :])
        return tuple(out)

    z = jnp.zeros((nb, LANES), F32)
    fin = lax.fori_loop(0, n_chunks, body, (z,) * (2 * ncb), unroll=4)
    for cb in range(ncb):
        fre_ref[cb] = fin[2 * cb]
        fim_ref[cb] = fin[2 * cb + 1]


def _ssm_c_kernel(xg_ref, hre_ref, him_ref, t_ref, v_ref, y_ref, ys_ref):
    nb, s_len, _ = y_ref.shape
    n_chunks = s_len // CHUNK
    half = CHUNK // 2
    for pp in range(GROUPS_PER_BLOCK // 2):
        hcat = jnp.concatenate([hre_ref[pp], him_ref[pp]], axis=1).astype(BF16)
        inter = jnp.dot(hcat, v_ref[pp], preferred_element_type=F32)
        for e in range(2):
            g = 2 * pp + e
            intra = jnp.dot(xg_ref[:, g * 2 * LANES:(g + 1) * 2 * LANES], t_ref[g],
                            preferred_element_type=F32)
            yg = intra + inter[:, e * 2 * LANES:(e + 1) * 2 * LANES]
            ys_ref[2 * g] = yg[:, 0:LANES]
            ys_ref[2 * g + 1] = yg[:, LANES:2 * LANES]

    for b in range(nb):
        def regroup(ct, c, b=b):
            c0 = pl.multiple_of(ct * REGROUP_ROWS, REGROUP_ROWS)
            rows = pl.ds(b * n_chunks + c0, REGROUP_ROWS)
            for j in range(2):
                yp =[ys_ref[2 * p + j, rows, :] for p in range(GROUPS_PER_BLOCK)]
                for tt, zt in enumerate(_granule_transpose(yp)):
                    y_ref[b, pl.ds(c0 * CHUNK + half * j + tt, REGROUP_ROWS, stride=CHUNK), :] = zt
            return c
        lax.fori_loop(0, n_chunks // REGROUP_ROWS, regroup, 0)


def _ssm_prompt(u, toe, w_pair, v_pair, a16_re, a16_im):
    b, s, _ = u.shape
    rows = b * (s // CHUNK)
    nblk = SSM_WIDTH // LANES
    gw = GROUPS_PER_BLOCK * CHUNK * SSM_CH
    ncb = GROUPS_PER_BLOCK * SSM_STATE // LANES
    ublk = pl.BlockSpec((b, s, LANES), lambda j: (0, 0, j))
    sblk = pl.BlockSpec((ncb, rows, LANES), lambda j: (j, 0, 0))
    sd_h = jax.ShapeDtypeStruct((nblk * ncb, rows, LANES), F32)
    xg, s_re, s_im = pl.pallas_call(
        _ssm_a_kernel,
        grid=(nblk,),
        in_specs=[ublk, pl.BlockSpec((GROUPS_PER_BLOCK // 2, 4 * LANES, 2 * LANES), lambda j: (j, 0, 0))],
        out_specs=[pl.BlockSpec((rows, gw), lambda j: (0, j)), sblk, sblk],
        out_shape=[jax.ShapeDtypeStruct((rows, nblk * gw), BF16), sd_h, sd_h],
        scratch_shapes=[pltpu.VMEM((gw // LANES, rows, LANES), F32)],
        compiler_params=_cparams(("arbitrary",)),
        name="ssm_chunk_states",
    )(u, w_pair)

    ablk = pl.BlockSpec((ncb, 1, LANES), lambda j: (j, 0, 0))
    fblk = pl.BlockSpec((ncb, b, LANES), lambda j: (j, 0, 0))
    sd_f = jax.ShapeDtypeStruct((nblk * ncb, b, LANES), F32)
    a16_re = a16_re.reshape(nblk * ncb, 1, LANES)
    a16_im = a16_im.reshape(nblk * ncb, 1, LANES)
    h_re, h_im, f_re, f_im = pl.pallas_call(
        functools.partial(_ssm_scan_kernel, nb=b),
        grid=(nblk,),
        in_specs=[sblk, sblk, ablk, ablk],
        out_specs=[sblk, sblk, fblk, fblk],
        out_shape=[sd_h, sd_h, sd_f, sd_f],
        compiler_params=_cparams(("arbitrary",)),
        name="ssm_scan",
    )(s_re, s_im, a16_re, a16_im)

    y = pl.pallas_call(
        _ssm_c_kernel,
        grid=(nblk,),
        in_specs=[pl.BlockSpec((rows, gw), lambda j: (0, j)), sblk, sblk,
                  pl.BlockSpec((GROUPS_PER_BLOCK, 2 * LANES, 2 * LANES), lambda j: (j, 0, 0)),
                  pl.BlockSpec((GROUPS_PER_BLOCK // 2, 2 * LANES, 4 * LANES), lambda j: (j, 0, 0))],
        out_specs=ublk,
        out_shape=jax.ShapeDtypeStruct((b, s, SSM_WIDTH), F32),
        scratch_shapes=[pltpu.VMEM((gw // LANES, rows, LANES), F32)],
        compiler_params=_cparams(("arbitrary",)),
        name="ssm_outputs",
    )(xg, h_re, h_im, toe, v_pair)
    to_rows = lambda f: jnp.transpose(f, (1, 0, 2)).reshape(b, nblk * ncb * LANES)
    return y, to_rows(f_re), to_rows(f_im)


def _ssm_sample_kernel(u_ref, hre_ref, him_ref, are_ref, aim_ref, bre_ref, bim_ref, cre_ref, cim_ref, d_ref,
                       y_ref, ore_ref, oim_ref, *, t_new):
    h_re = hre_ref[...].T
    h_im = him_ref[...].T
    a_re, a_im = are_ref[...], aim_ref[...]
    bd = h_re.shape[0]
    for t in range(t_new):
        u = u_ref[t]
        ub = u.astype(BF16)
        n_re = a_re * h_re - a_im * h_im + jnp.dot(ub, bre_ref[...], preferred_element_type=F32)
        n_im = a_re * h_im + a_im * h_re + jnp.dot(ub, bim_ref[...], preferred_element_type=F32)
        h_re, h_im = n_re, n_im
        y = (jnp.dot(h_re.astype(BF16), cre_ref[...], preferred_element_type=F32)
             + jnp.dot(h_im.astype(BF16), cim_ref[...], preferred_element_type=F32) + d_ref[...] * u)
        y_ref[t] = y
    ore_ref[...] = h_re.T
    oim_ref[...] = h_im.T


def _block_diag(m):
    g, r, c = m.shape
    eye = jnp.eye(g, dtype=m.dtype)
    return (eye[:, None, :, None] * m[:, :, None, :]).reshape(g * r, g * c)


def _ssm_sample(u, h0_re_t, h0_im_t, tb, c_re, c_im, d_skip, t_new):
    gn = SSM_GROUPS * SSM_STATE
    a_re = tb["pw_re"][:, :, 1].reshape(1, gn)
    a_im = tb["pw_im"][:, :, 1].reshape(1, gn)
    b_re = _block_diag(tb["bb_re"].transpose(0, 2, 1)).astype(BF16)
    b_im = _block_diag(tb["bb_im"].transpose(0, 2, 1)).astype(BF16)
    cb_re = _block_diag(c_re.transpose(0, 2, 1)).astype(BF16)
    cb_im = _block_diag(-c_im.transpose(0, 2, 1)).astype(BF16)
    bd = u.shape[1]
    full = lambda shape: pl.BlockSpec(shape, lambda i: (0,) * len(shape))
    return pl.pallas_call(
        functools.partial(_ssm_sample_kernel, t_new=t_new),
        grid=(1,),
        in_specs=[full((t_new, bd, SSM_WIDTH)), full((gn, bd)), full((gn, bd)), full((1, gn)), full((1, gn)),
                  full((SSM_WIDTH, gn)), full((SSM_WIDTH, gn)), full((gn, SSM_WIDTH)), full((gn, SSM_WIDTH)),
                  full((1, SSM_WIDTH))],
        out_specs=[full((t_new, bd, SSM_WIDTH)), full((gn, bd)), full((gn, bd))],
        out_shape=[jax.ShapeDtypeStruct((t_new, bd, SSM_WIDTH), F32),
                   jax.ShapeDtypeStruct((gn, bd), F32), jax.ShapeDtypeStruct((gn, bd), F32)],
        compiler_params=_cparams(("arbitrary",)),
        name="ssm_sample",
    )(u, h0_re_t, h0_im_t, a_re, a_im, b_re, b_im, cb_re, cb_im, d_skip.reshape(1, SSM_WIDTH))


def _rms(x, gain):
    return x * lax.rsqrt(jnp.mean(x * x, axis=-1, keepdims=True) + EPS) * gain


def _gelu_tanh(x):
    return 0.5 * x * (1.0 + jnp.tanh(math.sqrt(2.0 / math.pi) * (x + 0.044715 * (x * x * x))))


def _post_body(x_ref, oa_ref, ys_ref, g1_ref, sc2_ref, sh2_ref, g2_ref, n2_ref, ag_ref, sg_ref,
               wglu_ref, wout_ref, wg_ref, wu_ref, wd_ref, o_ref):
    ya = _gelu_tanh(ys_ref[...])
    ya = ya * _sigmoid(jnp.dot(ya.astype(BF16), wglu_ref[...], preferred_element_type=F32))
    merged = jnp.concatenate([_rms(oa_ref[...], ag_ref[...]), _rms(ya, sg_ref[...])], axis=1)
    x1 = x_ref[...] + g1_ref[...] * jnp.dot(merged.astype(BF16), wout_ref[...], preferred_element_type=F32)
    h2 = (_rms(x1, n2_ref[...]) * (1.0 + sc2_ref[...]) + sh2_ref[...]).astype(BF16)
    gate = jnp.dot(h2, wg_ref[...], preferred_element_type=F32)
    up = jnp.dot(h2, wu_ref[...], preferred_element_type=F32)
    act = (gate * _sigmoid(gate) * up).astype(BF16)
    o_ref[...] = x1 + g2_ref[...] * jnp.dot(act, wd_ref[...], preferred_element_type=F32)


def _post_sample(x, o_att, y_ssm, mods, consts, weights):
    t = x.shape[0]
    tm = min(TOKEN_TILE, t)
    tok = lambda w: pl.BlockSpec((tm, w), lambda i: (i, 0))
    mod = tok(D_MODEL)
    return pl.pallas_call(
        _post_body,
        grid=(t // tm,),
        in_specs=[tok(D_MODEL), tok(ATT_WIDTH), tok(SSM_WIDTH), mod, mod, mod, mod]
                 + [_const_spec(c.shape) for c in consts] + [_const_spec(w.shape) for w in weights],
        out_specs=tok(D_MODEL),
        out_shape=jax.ShapeDtypeStruct((t, D_MODEL), F32),
        compiler_params=_cparams(("arbitrary",)),
        name="post_sample",
    )(x, o_att, y_ssm, *mods, *consts, *weights)


def kernel(x_prompt, x_sample, cache_k, cache_v, state_ssm_re, state_ssm_im, c_prompt, c_sample, norm1_g, norm2_g, w_ada, b_ada, w_in, q_gain, k_gain, ssm_a_re, ssm_a_im, ssm_log_dt, ssm_b_re, ssm_b_im, ssm_c_re, ssm_c_im, ssm_d, w_glu, attn_out_g, ssm_out_g, w_out, w_gate, w_up, w_down):
    depth = norm1_g.shape[0]
    assert depth == 1, "one decoder layer"
    b, s, _ = x_prompt.shape
    bd, t_new, _ = x_sample.shape
    n_buf = cache_k.shape[2]
    L = 0

    n_c = b + bd
    pad = (-n_c) % SUBLANES
    c_all = jnp.concatenate([c_prompt, c_sample, jnp.zeros((pad, D_MODEL), F32)], axis=0)
    mod = _ada(c_all, w_ada[L], b_ada[L].reshape(1, -1))
    mod_p = [mod[0:b, i * D_MODEL:(i + 1) * D_MODEL].reshape(b, 1, D_MODEL) for i in range(N_MOD)]
    mod_s = [jnp.repeat(mod[b:n_c, i * D_MODEL:(i + 1) * D_MODEL], t_new, axis=0) for i in range(N_MOD)]

    w_in_bf = w_in[L].astype(BF16)
    n1 = norm1_g[L].reshape(1, D_MODEL)
    qg = jnp.tile(q_gain[L], ATT_HEADS).reshape(1, ATT_WIDTH)
    kg = jnp.tile(k_gain[L], ATT_HEADS).reshape(1, ATT_WIDTH)
    consts = (norm2_g[L].reshape(1, D_MODEL), attn_out_g[L].reshape(1, ATT_WIDTH), ssm_out_g[L].reshape(1, SSM_WIDTH))
    weights = tuple(w[L].astype(BF16) for w in (w_glu, w_out, w_gate, w_up, w_down))

    tb = _s5_tables(ssm_a_re[L], ssm_a_im[L], ssm_log_dt[L], ssm_b_re[L], ssm_b_im[L])
    toe, w_pair, v_pair, a16_re, a16_im = _s5_chunk_operators(tb, ssm_c_re[L], ssm_c_im[L], ssm_d[L])

    xs = x_sample.reshape(bd * t_new, D_MODEL)
    qs, ks, vs, us = _inproj_sample(xs, mod_s[1], mod_s[0], n1, w_in_bf, qg, kg)
    qx = _expand_queries(qs.reshape(bd, t_new, ATT_WIDTH))
    ks3, vs3 = ks.reshape(bd, t_new, ATT_WIDTH), vs.reshape(bd, t_new, ATT_WIDTH)
    kt = jnp.transpose(cache_k[L], (0, 2, 3, 1)).reshape(bd, ATT_WIDTH, n_buf)
    vt = jnp.transpose(cache_v[L], (0, 2, 3, 1)).reshape(bd, ATT_WIDTH, n_buf)
    q, k, v, u, kt_p, vt_p = _inproj_prompt(x_prompt, mod_p[1], mod_p[0], n1, w_in_bf, qg, kg)
    n_first = _attn_stream_batches(b, s)
    assert 0 < n_first < bd
    shifted, o_s_first, o_att = _stream_attn(kt, vt, 0, n_first, qx, ks3, vs3, q, k, v)
    y_ssm, f_re, f_im = _ssm_prompt(u, toe, w_pair, v_pair, a16_re, a16_im)
    okt, ovt, o_s_second, y_prompt = _stream_post(
        kt, vt, shifted, n_first, bd - n_first, qx, ks3, vs3, x_prompt, o_att, y_ssm,
        (mod_p[2], mod_p[4], mod_p[3], mod_p[5]), consts, weights)
    o_att_s = jnp.concatenate([o_s_first, o_s_second], axis=0)
    gn = SSM_GROUPS * SSM_STATE
    h0_re = jnp.transpose(state_ssm_re[L], (1, 2, 0)).reshape(gn, bd)
    h0_im = jnp.transpose(state_ssm_im[L], (1, 2, 0)).reshape(gn, bd)
    us_t = jnp.transpose(us.reshape(bd, t_new, SSM_WIDTH), (1, 0, 2))
    y_ssm_t, hs_re, hs_im = _ssm_sample(us_t, h0_re, h0_im, tb, ssm_c_re[L], ssm_c_im[L], ssm_d[L], t_new)
    y_ssm_s = jnp.transpose(y_ssm_t, (1, 0, 2)).reshape(bd * t_new, SSM_WIDTH)
    y_sample = _post_sample(xs, o_att_s.reshape(bd * t_new, ATT_WIDTH), y_ssm_s,
                            (mod_s[2], mod_s[4], mod_s[3], mod_s[5]), consts, weights)

    def from_t(a, nb, keep):
        return jnp.transpose(a.reshape(nb, ATT_HEADS, HEAD_DIM, keep), (0, 3, 1, 2))[None]

    def state_from_t(a):
        return jnp.transpose(a.reshape(SSM_GROUPS, SSM_STATE, bd), (2, 0, 1))[None]

    keep = min(MAX_SPAN, s)
    return (y_prompt, y_sample.reshape(bd, t_new, D_MODEL),
            from_t(kt_p, b, keep), from_t(vt_p, b, keep),
            f_re.reshape(b, SSM_GROUPS, SSM_STATE)[None], f_im.reshape(b, SSM_GROUPS, SSM_STATE)[None],
            from_t(okt, bd, n_buf), from_t(ovt, bd, n_buf),
            state_from_t(hs_re), state_from_t(hs_im))
```

```python
import functools
import math

import jax
import jax.numpy as jnp
import numpy as np
from jax import lax
from jax.experimental import pallas as pl
from jax.experimental.pallas import tpu as pltpu

F32 = jnp.float32
BF16 = jnp.bfloat16

D_MODEL = 1024
HEAD_DIM = 64
ATT_WIDTH = 512
ATT_HEADS = 8
SSM_WIDTH = 512
SSM_CH = 16
SSM_GROUPS = 32
SSM_STATE = 64
DILATIONS = ((128, 1), (512, 4), (2048, 16))
N_BACK = 128
MAX_SPAN = 2048
FFN_HIDDEN = 2816
N_MOD = 6
EPS = 1e-6

LANES = 128
SUBLANES = 8
VMEM_LIMIT = 56 * 1024 * 1024

TOKEN_TILE = 512
SUPER = 2048
QB = 128
TILE_GROUP = 4
Q_SCALE = HEAD_DIM ** -0.5 * math.log2(math.e)
CHUNK = 16
NEG = -1e30


def _cparams(sem=None):
    return pltpu.CompilerParams(dimension_semantics=sem, vmem_limit_bytes=VMEM_LIMIT)


def _const_spec(shape):
    nd = len(shape)
    return pl.BlockSpec(shape, lambda *_: (0,) * nd, pipeline_mode=pl.Buffered(1))


def _sigmoid(x):
    return 1.0 / (1.0 + jnp.exp(-x))


def _split_bf16(a):
    hi = a.astype(BF16)
    lo = (a - hi.astype(F32)).astype(BF16)
    return hi, lo


def _ada_kernel(c_ref, w_ref, b_ref, o_ref):
    c = c_ref[...]
    a = c * _sigmoid(c)
    a_hi, a_lo = _split_bf16(a)
    w_hi, w_lo = _split_bf16(w_ref[...])
    acc = jnp.dot(a_hi, w_hi, preferred_element_type=F32)
    acc += jnp.dot(a_hi, w_lo, preferred_element_type=F32)
    acc += jnp.dot(a_lo, w_hi, preferred_element_type=F32)
    o_ref[...] = acc + b_ref[...]


def _ada(c_all, w_ada, b_ada):
    rows = c_all.shape[0]
    n = w_ada.shape[1]
    tn = 1024
    return pl.pallas_call(
        _ada_kernel,
        grid=(n // tn,),
        in_specs=[pl.BlockSpec((rows, D_MODEL), lambda j: (0, 0)),
                  pl.BlockSpec((D_MODEL, tn), lambda j: (0, j)),
                  pl.BlockSpec((1, tn), lambda j: (0, j))],
        out_specs=pl.BlockSpec((rows, tn), lambda j: (0, j)),
        out_shape=jax.ShapeDtypeStruct((rows, n), F32),
        compiler_params=_cparams(("arbitrary",)),
        name="ada",
    )(c_all, w_ada, b_ada)


def _head_rms(z, gain):
    lane = lax.broadcasted_iota(jnp.int32, (1, LANES), 1)
    lo = lane < HEAD_DIM
    outs = []
    for c in range(z.shape[1] // LANES):
        blk = z[:, c * LANES:(c + 1) * LANES]
        sq = blk * blk
        s_lo = jnp.sum(jnp.where(lo, sq, 0.0), axis=-1, keepdims=True)
        s_hi = jnp.sum(jnp.where(lo, 0.0, sq), axis=-1, keepdims=True)
        inv = jnp.where(lo, lax.rsqrt(s_lo * (1.0 / HEAD_DIM) + EPS), lax.rsqrt(s_hi * (1.0 / HEAD_DIM) + EPS))
        outs.append(blk * inv)
    return jnp.concatenate(outs, axis=1) * gain


def _inproj_body(x_ref, scale_ref, shift_ref, g_ref, w_ref, qg_ref, kg_ref, *out_refs, kt_rule=None):
    q_ref, k_ref, v_ref, u_ref = out_refs[:4]
    x = x_ref[...]
    ms = jnp.mean(x * x, axis=-1, keepdims=True)
    h = x * lax.rsqrt(ms + EPS) * g_ref[...]
    h = h * (1.0 + scale_ref[...]) + shift_ref[...]
    z = jnp.dot(h.astype(BF16), w_ref[...], preferred_element_type=F32)
    q = _head_rms(z[:, 0:ATT_WIDTH], qg_ref[...]) * Q_SCALE
    k = _head_rms(z[:, ATT_WIDTH:2 * ATT_WIDTH], kg_ref[...])
    v = z[:, 2 * ATT_WIDTH:3 * ATT_WIDTH]
    q_ref[...] = q
    k_ref[...] = k
    v_ref[...] = v
    u_ref[...] = z[:, 3 * ATT_WIDTH:]
    if kt_rule is not None:
        per_seq, kt_first = kt_rule
        kt_ref, vt_ref = out_refs[4:]

        @pl.when(pl.program_id(0) % per_seq >= kt_first)
        def _():
            kt_ref[...] = k.T
            vt_ref[...] = v.T


def _inproj_sample(x, scale, shift, g, w_bf, qg, kg):
    t = x.shape[0]
    tm = min(TOKEN_TILE, t)
    tok = pl.BlockSpec((tm, D_MODEL), lambda i: (i, 0))
    out_tok = pl.BlockSpec((tm, ATT_WIDTH), lambda i: (i, 0))
    sd = jax.ShapeDtypeStruct((t, ATT_WIDTH), F32)
    return pl.pallas_call(
        _inproj_body,
        grid=(t // tm,),
        in_specs=[tok, tok, tok, _const_spec((1, D_MODEL)), _const_spec((D_MODEL, 4 * ATT_WIDTH)),
                  _const_spec((1, ATT_WIDTH)), _const_spec((1, ATT_WIDTH))],
        out_specs=[out_tok] * 4,
        out_shape=[sd] * 4,
        compiler_params=_cparams(("arbitrary",)),
        name="inproj_sample",
    )(x, scale, shift, g, w_bf, qg, kg)


ATTN_PHASES = len(DILATIONS) + 1


def _attn_rider(q_ref, kp_ref, kc_ref, vp_ref, vc_ref, bias_ref, o_ref, kbuf, vbuf, acc_ref, m_ref, l_ref, *,
                n_super, side):
    step = pl.program_id(0)
    phase = step % ATTN_PHASES
    st = (step // ATTN_PHASES) % n_super
    lane = lax.broadcasted_iota(jnp.int32, (1, LANES), 1)
    lo = lane < HEAD_DIM
    nt_contract = (((1,), (1,)), ((), ()))
    n_groups = SUPER // QB // TILE_GROUP

    def pattern(p, d):
        per_res = SUPER // d // QB

        for gi in range(n_groups):
            tiles = []
            for u in range(TILE_GROUP):
                ti = gi * TILE_GROUP + u
                r = ti // per_res
                jt = ti % per_res
                row0 = r + d * QB * jt
                ks = SUPER + row0 - d * QB
                bias = bias_ref[jnp.where(st == 0, 0, 1)] if jt == 0 else bias_ref[1]
                q = q_ref[pl.ds(row0, QB, stride=d), :]
                k = kbuf[pl.ds(ks, 2 * QB, stride=d), :].astype(BF16)
                v = vbuf[pl.ds(ks, 2 * QB, stride=d), :].astype(BF16)
                v = jnp.concatenate([v, jnp.ones_like(v)], axis=1)
                q2 = jnp.concatenate([jnp.where(lo, q, 0.0), jnp.where(lo, 0.0, q)], axis=0).astype(BF16)
                s = lax.dot_general(q2, k, nt_contract, preferred_element_type=F32)
                tiles.append((s, bias, v, row0))
            probs = []
            for s, bias, v, row0 in tiles:
                s = s + jnp.concatenate([bias, bias], axis=0)
                m = jnp.max(s, axis=-1, keepdims=True)
                pr = jnp.exp2(s - m)
                probs.append((pr.astype(BF16), m))
            for (s, bias, v, row0), (pr, m) in zip(tiles, probs):
                pv = jnp.dot(pr, v, preferred_element_type=F32)
                rows = pl.ds(row0, QB, stride=d)
                acc_ref[p, rows, :] = jnp.where(lo, pv[0:QB, 0:LANES], pv[QB:2 * QB, 0:LANES])
                m_ref[p, rows, :] = jnp.where(lo, m[0:QB], m[QB:2 * QB])
                l_ref[p, rows, :] = jnp.where(lo, pv[0:QB, LANES:2 * LANES], pv[QB:2 * QB, LANES:2 * LANES])
            side(gi, n_groups)

    @pl.when(phase == 0)
    def _():
        kbuf[0:SUPER, :] = kp_ref[...]
        kbuf[SUPER:2 * SUPER, :] = kc_ref[...]
        vbuf[0:SUPER, :] = vp_ref[...]
        vbuf[SUPER:2 * SUPER, :] = vc_ref[...]

    for p, (_, d) in enumerate(DILATIONS):
        pl.when(phase == p)(functools.partial(pattern, p, d))

    @pl.when(phase == ATTN_PHASES - 1)
    def _():
        n_merge = SUPER // QB
        for ci in range(n_merge):
            rows = pl.ds(ci * QB, QB)
            m1, m2, m3 = m_ref[0, rows, :], m_ref[1, rows, :], m_ref[2, rows, :]
            mm = jnp.maximum(jnp.maximum(m1, m2), m3)
            w1, w2, w3 = jnp.exp2(m1 - mm), jnp.exp2(m2 - mm), jnp.exp2(m3 - mm)
            num = w1 * acc_ref[0, rows, :] + w2 * acc_ref[1, rows, :] + w3 * acc_ref[2, rows, :]
            den = w1 * l_ref[0, rows, :] + w2 * l_ref[1, rows, :] + w3 * l_ref[2, rows, :]
            o_ref[rows, :] = num / den
            if ci % (n_merge // n_groups) == 0:
                side(ci // (n_merge // n_groups), n_groups)


def _attn_bias():
    qi = np.arange(QB)[:, None]
    ki = np.arange(2 * QB)[None, :]
    dist = qi + QB - ki
    normal = (dist >= 0) & (dist <= N_BACK)
    first = normal & (ki >= QB)
    return jnp.asarray(np.where(np.stack([first, normal]), 0.0, NEG), F32)


SA_HEADS = 4
SA_WIDTH = SA_HEADS * HEAD_DIM
N_SA_IN = 7


def _stream_kernel(*refs, rider, rider_takes_side, n_rider_in, n_rider_out, n_prev, n_rider_scratch):
    kt_ref, vt_ref, qx_ref, kn_ref, vn_ref, w_ref, hm_ref = refs[:N_SA_IN]
    rider_in = refs[N_SA_IN:N_SA_IN + n_rider_in]
    n_in = N_SA_IN + n_rider_in + n_prev
    okt_ref, ovt_ref, o_ref = refs[n_in:n_in + 3]
    rider_out = refs[n_in + 3:n_in + 3 + n_rider_out]
    pad_ref, newt_ref = refs[n_in + 3 + n_rider_out:n_in + 5 + n_rider_out]
    rider_scratch = refs[len(refs) - n_rider_scratch:] if n_rider_scratch else ()
    n_buf = kt_ref.shape[1]
    t_new = kn_ref.shape[0]
    n_tiles = n_buf // LANES

    for i, new_ref in enumerate((kn_ref, vn_ref)):
        pad_ref[...] = jnp.zeros_like(pad_ref)
        pad_ref[0:t_new, :] = new_ref[...]
        newt_ref[i] = pad_ref[...].T

    boundary = {}

    def shift(part, n_parts):
        per = n_tiles // n_parts
        first = part * per
        keep = lax.broadcasted_iota(jnp.int32, (1, LANES), 1) < LANES - t_new
        if part == 0:
            boundary.clear()
        for i, (src_ref, dst_ref) in enumerate(((kt_ref, okt_ref), (vt_ref, ovt_ref))):
            def rotated(j):
                if (i, j) in boundary:
                    return boundary.pop((i, j))
                t = src_ref[:, j * LANES:(j + 1) * LANES] if j < n_tiles else newt_ref[i]
                return pltpu.roll(t, LANES - t_new, axis=1)
            rolled = [rotated(j) for j in range(first, first + per + 1)]
            for j in range(per):
                dst_ref[:, (first + j) * LANES:(first + j + 1) * LANES] = jnp.where(keep, rolled[j], rolled[j + 1])
            if part + 1 < n_parts:
                boundary[(i, first + per)] = rolled[per]

    k_ext = jnp.concatenate([kt_ref[...], newt_ref[0]], axis=1)
    v_ext = jnp.concatenate([vt_ref[...], newt_ref[1]], axis=1)
    w = w_ref[...]
    s = jnp.dot(qx_ref[...], k_ext.astype(BF16), preferred_element_type=F32)
    s = jnp.where(w > 0.0, s, NEG)
    m = jnp.max(s, axis=-1, keepdims=True)
    e = w * jnp.exp2(s - m)
    l = jnp.sum(e, axis=-1, keepdims=True)
    pv = lax.dot_general(e.astype(BF16), v_ext.astype(BF16), (((1,), (1,)), ((), ())),
                         preferred_element_type=F32)
    pv = pv * hm_ref[...] / l
    o = pv[0:t_new]
    for h in range(1, SA_HEADS):
        o = o + pv[h * t_new:(h + 1) * t_new]
    o_ref[...] = o

    if rider_takes_side:
        rider(*rider_in, *rider_out, *rider_scratch, side=shift)
    else:
        shift(0, 1)
        rider(*rider_in, *rider_out, *rider_scratch)


def _sa_weights(t_new, n_buf):
    ext = n_buf + LANES
    pos = np.arange(ext)[None, :]
    i = np.arange(t_new)[:, None]
    delta = n_buf + i - pos
    is_real = (pos < n_buf + t_new)
    w = np.zeros((t_new, ext), np.float32)
    for window, d in DILATIONS:
        w += ((delta >= 0) & (delta % d == 0) & (delta <= window) & is_real)
    return jnp.asarray(np.tile(w, (SA_HEADS, 1)), F32)


def _head_mask(t_new):
    h = np.repeat(np.arange(SA_HEADS), t_new)[:, None]
    c = np.arange(SA_WIDTH)[None, :] // HEAD_DIM
    return (h == c).astype(np.float32)


SA_NBLK = ATT_WIDTH // SA_WIDTH


def _stream(kt, vt, prev, batch_lo, n_batches, qx, k_new, v_new, rider, rider_args, rider_in_specs,
            rider_out_specs, rider_out_shapes, name, rider_scratch=(), rider_takes_side=False):
    bd, _, n_buf = kt.shape
    t_new = k_new.shape[1]
    assert n_buf == MAX_SPAN, "window buffer must cover the longest pattern (no invalid positions)"
    rows = t_new * SA_HEADS
    buf = pl.BlockSpec((None, SA_WIDTH, n_buf), lambda i: (batch_lo + i // SA_NBLK, i % SA_NBLK, 0))
    new = pl.BlockSpec((None, t_new, SA_WIDTH), lambda i: (batch_lo + i // SA_NBLK, 0, i % SA_NBLK))
    o_blk = pl.BlockSpec((None, t_new, SA_WIDTH), lambda i: (i // SA_NBLK, 0, i % SA_NBLK))
    qx_blk = pl.BlockSpec((None, None, rows, SA_WIDTH), lambda i: (batch_lo + i // SA_NBLK, i % SA_NBLK, 0, 0))
    sd_buf = jax.ShapeDtypeStruct((bd, ATT_WIDTH, n_buf), F32)
    in_specs = [buf, buf, qx_blk, new, new, _const_spec((rows, n_buf + LANES)), _const_spec((rows, SA_WIDTH))]
    in_specs += list(rider_in_specs)
    args = [kt, vt, qx, k_new, v_new, _sa_weights(t_new, n_buf), jnp.asarray(_head_mask(t_new), F32)]
    args += list(rider_args)
    aliases = {}
    if prev is not None:
        aliases = {len(args): 0, len(args) + 1: 1}
        in_specs += [pl.BlockSpec(memory_space=pl.ANY)] * 2
        args += list(prev)
    return pl.pallas_call(
        functools.partial(_stream_kernel, rider=rider, rider_takes_side=rider_takes_side,
                          n_rider_in=len(rider_in_specs),
                          n_rider_out=len(rider_out_specs), n_prev=len(aliases),
                          n_rider_scratch=len(rider_scratch)),
        grid=(n_batches * SA_NBLK,),
        in_specs=in_specs,
        out_specs=[buf, buf, o_blk] + list(rider_out_specs),
        out_shape=[sd_buf, sd_buf, jax.ShapeDtypeStruct((n_batches, t_new, ATT_WIDTH), F32)] + list(rider_out_shapes),
        input_output_aliases=aliases,
        scratch_shapes=[pltpu.VMEM((LANES, SA_WIDTH), F32), pltpu.VMEM((2, SA_WIDTH, LANES), F32)]
                       + list(rider_scratch),
        compiler_params=_cparams(("arbitrary",)),
        name=name,
    )(*args)


def _expand_queries(q):
    bd, t_new, _ = q.shape
    hm = jnp.asarray(_head_mask(t_new), F32)
    qb = q.reshape(bd, t_new, SA_NBLK, SA_WIDTH).transpose(0, 2, 1, 3)
    return (jnp.tile(qb, (1, 1, SA_HEADS, 1)) * hm).astype(BF16)


def _rider_tiling(b, s, steps):
    assert (b * s) % steps == 0
    tm = b * s // steps
    assert tm % LANES == 0 and s % tm == 0
    return tm, s // tm


def _inproj_prompt(x, scale, shift, g, w_bf, qg, kg):
    b, s, _ = x.shape
    tm = TOKEN_TILE
    per_seq = s // tm
    keep = min(MAX_SPAN, s)
    kt_first = per_seq - keep // tm
    tok = lambda w: pl.BlockSpec((tm, w), lambda i: (i, 0))
    mod = pl.BlockSpec((None, 1, D_MODEL), lambda i: (i // per_seq, 0, 0))
    out_t = pl.BlockSpec((None, ATT_WIDTH, tm), lambda i: (i // per_seq, 0, jnp.maximum(i % per_seq - kt_first, 0)))
    sd = jax.ShapeDtypeStruct((b * s, ATT_WIDTH), F32)
    sdt = jax.ShapeDtypeStruct((b, ATT_WIDTH, keep), F32)
    q, k, v, u, kt_p, vt_p = pl.pallas_call(
        functools.partial(_inproj_body, kt_rule=(per_seq, kt_first)),
        grid=(b * per_seq,),
        in_specs=[tok(D_MODEL), mod, mod, _const_spec((1, D_MODEL)), _const_spec((D_MODEL, 4 * ATT_WIDTH)),
                  _const_spec((1, ATT_WIDTH)), _const_spec((1, ATT_WIDTH))],
        out_specs=[tok(ATT_WIDTH)] * 4 + [out_t] * 2,
        out_shape=[sd] * 4 + [sdt] * 2,
        compiler_params=_cparams(("arbitrary",)),
        name="inproj_prompt",
    )(x.reshape(b * s, D_MODEL), scale, shift, g, w_bf, qg, kg)
    r3 = lambda a: a.reshape(b, s, ATT_WIDTH)
    return r3(q), r3(k), r3(v), r3(u), kt_p, vt_p


def _attn_stream_batches(b, s):
    n_super = s // SUPER
    units = b * (ATT_WIDTH // LANES) * n_super
    steps = units * ATTN_PHASES
    assert steps % SA_NBLK == 0
    return steps // SA_NBLK


def _stream_attn(kt, vt, batch_lo, n_batches, qx, k_new, v_new, q, k, v):
    b, s, _ = q.shape
    assert s % SUPER == 0
    n_super = s // SUPER
    n_hp = ATT_WIDTH // LANES
    assert n_batches == _attn_stream_batches(b, s)

    def unit(i):
        un = i // ATTN_PHASES
        return un // (n_hp * n_super), un % n_super, (un // n_super) % n_hp

    def cur(i):
        bi, st, hp = unit(i)
        return bi, st, hp

    def prev(i):
        bi, st, hp = unit(i)
        return bi, jnp.maximum(st - 1, 0), hp

    blk_c = pl.BlockSpec((None, SUPER, LANES), cur)
    blk_p = pl.BlockSpec((None, SUPER, LANES), prev)
    outs = _stream(
        kt, vt, None, batch_lo, n_batches, qx, k_new, v_new,
        functools.partial(_attn_rider, n_super=n_super),
        (q, k, k, v, v, _attn_bias()),
        [blk_c, blk_p, blk_c, blk_p, blk_c, _const_spec((2, QB, 2 * QB))],
        [blk_c], [jax.ShapeDtypeStruct((b, s, ATT_WIDTH), F32)], "stream_attn",
        rider_scratch=[pltpu.VMEM((2 * SUPER, LANES), F32)] * 2 + [pltpu.VMEM((len(DILATIONS), SUPER, LANES), F32)] * 3,
        rider_takes_side=True)
    okt, ovt, o_s, o_att = outs
    return (okt, ovt), o_s, o_att


def _stream_post(kt, vt, prev, batch_lo, n_batches, qx, k_new, v_new, x, o_att, y_ssm, mods, consts, weights):
    b, s, _ = x.shape
    tm, per_seq = _rider_tiling(b, s, n_batches * SA_NBLK)
    tok = lambda w: pl.BlockSpec((tm, w), lambda i: (i, 0))
    mod = pl.BlockSpec((None, 1, D_MODEL), lambda i: (i // per_seq, 0, 0))
    okt, ovt, o_s, y = _stream(
        kt, vt, prev, batch_lo, n_batches, qx, k_new, v_new, _post_body,
        (x.reshape(b * s, D_MODEL), o_att.reshape(b * s, ATT_WIDTH), y_ssm.reshape(b * s, SSM_WIDTH),
         *mods, *consts, *weights),
        [tok(D_MODEL), tok(ATT_WIDTH), tok(SSM_WIDTH), mod, mod, mod, mod]
        + [_const_spec(c.shape) for c in consts] + [_const_spec(w.shape) for w in weights],
        [tok(D_MODEL)], [jax.ShapeDtypeStruct((b * s, D_MODEL), F32)], "stream_post")
    return okt, ovt, o_s, y.reshape(b, s, D_MODEL)


def _s5_tables(a_re, a_im, log_dt, b_re, b_im):
    g, n = a_re.shape
    dt = jnp.exp(log_dt)[:, None]
    x, y = dt * a_re, dt * a_im
    ks = jnp.arange(CHUNK + 1, dtype=F32)
    mag = jnp.exp(x[:, :, None] * ks)
    pw_re, pw_im = mag * jnp.cos(y[:, :, None] * ks), mag * jnp.sin(y[:, :, None] * ks)
    e_re = jnp.expm1(x) * jnp.cos(y) - 2.0 * jnp.sin(0.5 * y) ** 2
    e_im = jnp.exp(x) * jnp.sin(y)
    den = a_re * a_re + a_im * a_im
    f_re = (e_re * a_re + e_im * a_im) / den
    f_im = (e_im * a_re - e_re * a_im) / den
    bb_re = f_re[..., None] * b_re - f_im[..., None] * b_im
    bb_im = f_re[..., None] * b_im + f_im[..., None] * b_re
    return dict(pw_re=pw_re, pw_im=pw_im, bb_re=bb_re, bb_im=bb_im)


def _s5_chunk_operators(tb, c_re, c_im, d_skip):
    pw_re, pw_im, bb_re, bb_im = tb["pw_re"], tb["pw_im"], tb["bb_re"], tb["bb_im"]
    g, n, ch = bb_re.shape
    L = CHUNK
    hp = lax.Precision.HIGHEST
    ct_re, ct_im = c_re.transpose(0, 2, 1), c_im.transpose(0, 2, 1)
    pr, pi = pw_re[:, :, :, None], pw_im[:, :, :, None]
    ca_re = ct_re[:, :, None, :] * pr - ct_im[:, :, None, :] * pi
    ca_im = ct_re[:, :, None, :] * pi + ct_im[:, :, None, :] * pr
    a_cat = jnp.concatenate([ca_re[:, :, :L].reshape(g, n, L * ch), -ca_im[:, :, :L].reshape(g, n, L * ch)], axis=1)
    b_cat = jnp.concatenate([bb_re, bb_im], axis=1)
    base = jnp.einsum("gnd,gnx->gdx", b_cat, a_cat, precision=hp)
    skip = jnp.eye(ch, dtype=F32)[None] * d_skip.reshape(g, ch, 1)
    base = base + jnp.pad(skip, ((0, 0), (0, 0), (0, (L - 1) * ch)))
    toe = jnp.stack([jnp.pad(base, ((0, 0), (0, 0), (s * ch, 0)))[:, :, :L * ch] for s in range(L)], axis=1)
    toe = toe.reshape(g, L * ch, L * ch)
    rv_re = pw_re[:, :, L - 1::-1].transpose(0, 2, 1)[:, :, None, :]
    rv_im = pw_im[:, :, L - 1::-1].transpose(0, 2, 1)[:, :, None, :]
    bt_re, bt_im = bb_re.transpose(0, 2, 1)[:, None], bb_im.transpose(0, 2, 1)[:, None]
    ws_re = (rv_re * bt_re - rv_im * bt_im).reshape(g, L * ch, n)
    ws_im = (rv_re * bt_im + rv_im * bt_re).reshape(g, L * ch, n)
    vr = ca_re[:, :, 1:].reshape(g, n, L * ch)
    vi = (-ca_im[:, :, 1:]).reshape(g, n, L * ch)
    gp = g // 2
    z_w = jnp.zeros((gp, L * ch, n), F32)
    wr, wi = ws_re.reshape(gp, 2, L * ch, n), ws_im.reshape(gp, 2, L * ch, n)
    w_pair = jnp.concatenate([
        jnp.concatenate([wr[:, 0], z_w, wi[:, 0], z_w], axis=2),
        jnp.concatenate([z_w, wr[:, 1], z_w, wi[:, 1]], axis=2)], axis=1)
    z_v = jnp.zeros((gp, n, L * ch), F32)
    vr, vi = vr.reshape(gp, 2, n, L * ch), vi.reshape(gp, 2, n, L * ch)
    v_pair = jnp.concatenate([
        jnp.concatenate([vr[:, 0], z_v], axis=2), jnp.concatenate([z_v, vr[:, 1]], axis=2),
        jnp.concatenate([vi[:, 0], z_v], axis=2), jnp.concatenate([z_v, vi[:, 1]], axis=2)], axis=1)
    a16_re = pw_re[:, :, L].reshape(1, g * n)
    a16_im = pw_im[:, :, L].reshape(1, g * n)
    return toe.astype(BF16), w_pair.astype(BF16), v_pair.astype(BF16), a16_re, a16_im


GROUPS_PER_BLOCK = LANES // SSM_CH
REGROUP_ROWS = 32


def _granule_transpose(arrs):
    n = GROUPS_PER_BLOCK
    gran = lax.broadcasted_iota(jnp.int32, (1, LANES), 1) // SSM_CH
    cur = list(arrs)
    k = n // 2
    while k >= 1:
        keep_low = (gran % (2 * k)) < k
        nxt = list(cur)
        for t in range(n):
            if t % (2 * k) < k:
                a, b = cur[t], cur[t + k]
                nxt[t] = jnp.where(keep_low, a, pltpu.roll(b, k * SSM_CH, axis=1))
                nxt[t + k] = jnp.where(keep_low, pltpu.roll(a, LANES - k * SSM_CH, axis=1), b)
        cur = nxt
        k //= 2
    return cur


def _ssm_a_kernel(u_ref, w_ref, xg_ref, sre_ref, sim_ref, xs_ref):
    nb, s_len, _ = u_ref.shape
    n_chunks = s_len // CHUNK
    half = CHUNK // 2

    for b in range(nb):
        def regroup(ct, c, b=b):
            c0 = pl.multiple_of(ct * REGROUP_ROWS, REGROUP_ROWS)
            rows = pl.ds(b * n_chunks + c0, REGROUP_ROWS)
            for j in range(2):
                z = [u_ref[b, pl.ds(c0 * CHUNK + half * j + tt, REGROUP_ROWS, stride=CHUNK), :]
                     for tt in range(half)]
                for p, xp in enumerate(_granule_transpose(z)):
                    xs_ref[2 * p + j, rows, :] = xp
            return c
        lax.fori_loop(0, n_chunks // REGROUP_ROWS, regroup, 0)

    for col in range(2 * GROUPS_PER_BLOCK):
        xg_ref[:, col * LANES:(col + 1) * LANES] = xs_ref[col].astype(BF16)
    for pp in range(GROUPS_PER_BLOCK // 2):
        xp = xg_ref[:, pp * 4 * LANES:(pp + 1) * 4 * LANES]
        st = jnp.dot(xp, w_ref[pp], preferred_element_type=F32)
        sre_ref[pp] = st[:, 0:LANES]
        sim_ref[pp] = st[:, LANES:2 * LANES]


def _ssm_scan_kernel(sre_ref, sim_ref, are_ref, aim_ref, hre_ref, him_ref, fre_ref, fim_ref, *, nb):
    ncb, rows, _ = sre_ref.shape
    n_chunks = rows // nb
    a_re = [are_ref[cb] for cb in range(ncb)]
    a_im = [aim_ref[cb] for cb in range(ncb)]

    def body(c, carry):
        r = pl.ds(c, nb, stride=n_chunks)
        out = []
        for cb in range(ncb):
            h_re, h_im = carry[2 * cb], carry[2 * cb + 1]
            hre_ref[cb, r, :] = h_re
            him_ref[cb, r, :] = h_im
            out.append(a_re[cb] * h_re - a_im[cb] * h_im + sre_ref[cb, r, :])
            out.append(a_re[cb] * h_im + a_im[cb] * h_re + sim_ref[cb, r, :])
        return tuple(out)

    z = jnp.zeros((nb, LANES), F32)
    fin = lax.fori_loop(0, n_chunks, body, (z,) * (2 * ncb), unroll=4)
    for cb in range(ncb):
        fre_ref[cb] = fin[2 * cb]
        fim_ref[cb] = fin[2 * cb + 1]


def _ssm_c_kernel(xg_ref, hre_ref, him_ref, t_ref, v_ref, y_ref, ys_ref):
    nb, s_len, _ = y_ref.shape
    n_chunks = s_len // CHUNK
    half = CHUNK // 2
    for pp in range(GROUPS_PER_BLOCK // 2):
        hcat = jnp.concatenate([hre_ref[pp], him_ref[pp]], axis=1).astype(BF16)
        inter = jnp.dot(hcat, v_ref[pp], preferred_element_type=F32)
        for e in range(2):
            g = 2 * pp + e
            intra = jnp.dot(xg_ref[:, g * 2 * LANES:(g + 1) * 2 * LANES], t_ref[g],
                            preferred_element_type=F32)
            yg = intra + inter[:, e * 2 * LANES:(e + 1) * 2 * LANES]
            ys_ref[2 * g] = yg[:, 0:LANES]
            ys_ref[2 * g + 1] = yg[:, LANES:2 * LANES]

    for b in range(nb):
        def regroup(ct, c, b=b):
            c0 = pl.multiple_of(ct * REGROUP_ROWS, REGROUP_ROWS)
            rows = pl.ds(b * n_chunks + c0, REGROUP_ROWS)
            for j in range(2):
                yp =[ys_ref[2 * p + j, rows, :] for p in range(GROUPS_PER_BLOCK)]
                for tt, zt in enumerate(_granule_transpose(yp)):
                    y_ref[b, pl.ds(c0 * CHUNK + half * j + tt, REGROUP_ROWS, stride=CHUNK), :] = zt
            return c
        lax.fori_loop(0, n_chunks // REGROUP_ROWS, regroup, 0)


def _ssm_prompt(u, toe, w_pair, v_pair, a16_re, a16_im):
    b, s, _ = u.shape
    rows = b * (s // CHUNK)
    nblk = SSM_WIDTH // LANES
    gw = GROUPS_PER_BLOCK * CHUNK * SSM_CH
    ncb = GROUPS_PER_BLOCK * SSM_STATE // LANES
    ublk = pl.BlockSpec((b, s, LANES), lambda j: (0, 0, j))
    sblk = pl.BlockSpec((ncb, rows, LANES), lambda j: (j, 0, 0))
    sd_h = jax.ShapeDtypeStruct((nblk * ncb, rows, LANES), F32)
    xg, s_re, s_im = pl.pallas_call(
        _ssm_a_kernel,
        grid=(nblk,),
        in_specs=[ublk, pl.BlockSpec((GROUPS_PER_BLOCK // 2, 4 * LANES, 2 * LANES), lambda j: (j, 0, 0))],
        out_specs=[pl.BlockSpec((rows, gw), lambda j: (0, j)), sblk, sblk],
        out_shape=[jax.ShapeDtypeStruct((rows, nblk * gw), BF16), sd_h, sd_h],
        scratch_shapes=[pltpu.VMEM((gw // LANES, rows, LANES), F32)],
        compiler_params=_cparams(("arbitrary",)),
        name="ssm_chunk_states",
    )(u, w_pair)

    ablk = pl.BlockSpec((ncb, 1, LANES), lambda j: (j, 0, 0))
    fblk = pl.BlockSpec((ncb, b, LANES), lambda j: (j, 0, 0))
    sd_f = jax.ShapeDtypeStruct((nblk * ncb, b, LANES), F32)
    a16_re = a16_re.reshape(nblk * ncb, 1, LANES)
    a16_im = a16_im.reshape(nblk * ncb, 1, LANES)
    h_re, h_im, f_re, f_im = pl.pallas_call(
        functools.partial(_ssm_scan_kernel, nb=b),
        grid=(nblk,),
        in_specs=[sblk, sblk, ablk, ablk],
        out_specs=[sblk, sblk, fblk, fblk],
        out_shape=[sd_h, sd_h, sd_f, sd_f],
        compiler_params=_cparams(("arbitrary",)),
        name="ssm_scan",
    )(s_re, s_im, a16_re, a16_im)

    y = pl.pallas_call(
        _ssm_c_kernel,
        grid=(nblk,),
        in_specs=[pl.BlockSpec((rows, gw), lambda j: (0, j)), sblk, sblk,
                  pl.BlockSpec((GROUPS_PER_BLOCK, 2 * LANES, 2 * LANES), lambda j: (j, 0, 0)),
                  pl.BlockSpec((GROUPS_PER_BLOCK // 2, 2 * LANES, 4 * LANES), lambda j: (j, 0, 0))],
        out_specs=ublk,
        out_shape=jax.ShapeDtypeStruct((b, s, SSM_WIDTH), F32),
        scratch_shapes=[pltpu.VMEM((gw // LANES, rows, LANES), F32)],
        compiler_params=_cparams(("arbitrary",)),
        name="ssm_outputs",
    )(xg, h_re, h_im, toe, v_pair)
    to_rows = lambda f: jnp.transpose(f, (1, 0, 2)).reshape(b, nblk * ncb * LANES)
    return y, to_rows(f_re), to_rows(f_im)


def _ssm_sample_kernel(u_ref, hre_ref, him_ref, are_ref, aim_ref, bre_ref, bim_ref, cre_ref, cim_ref, d_ref,
                       y_ref, ore_ref, oim_ref, *, t_new):
    h_re = hre_ref[...].T
    h_im = him_ref[...].T
    a_re, a_im = are_ref[...], aim_ref[...]
    bd = h_re.shape[0]
    for t in range(t_new):
        u = u_ref[t]
        ub = u.astype(BF16)
        n_re = a_re * h_re - a_im * h_im + jnp.dot(ub, bre_ref[...], preferred_element_type=F32)
        n_im = a_re * h_im + a_im * h_re + jnp.dot(ub, bim_ref[...], preferred_element_type=F32)
        h_re, h_im = n_re, n_im
        y = (jnp.dot(h_re.astype(BF16), cre_ref[...], preferred_element_type=F32)
             + jnp.dot(h_im.astype(BF16), cim_ref[...], preferred_element_type=F32) + d_ref[...] * u)
        y_ref[t] = y
    ore_ref[...] = h_re.T
    oim_ref[...] = h_im.T


def _block_diag(m):
    g, r, c = m.shape
    eye = jnp.eye(g, dtype=m.dtype)
    return (eye[:, None, :, None] * m[:, :, None, :]).reshape(g * r, g * c)


def _ssm_sample(u, h0_re_t, h0_im_t, tb, c_re, c_im, d_skip, t_new):
    gn = SSM_GROUPS * SSM_STATE
    a_re = tb["pw_re"][:, :, 1].reshape(1, gn)
    a_im = tb["pw_im"][:, :, 1].reshape(1, gn)
    b_re = _block_diag(tb["bb_re"].transpose(0, 2, 1)).astype(BF16)
    b_im = _block_diag(tb["bb_im"].transpose(0, 2, 1)).astype(BF16)
    cb_re = _block_diag(c_re.transpose(0, 2, 1)).astype(BF16)
    cb_im = _block_diag(-c_im.transpose(0, 2, 1)).astype(BF16)
    bd = u.shape[1]
    full = lambda shape: pl.BlockSpec(shape, lambda i: (0,) * len(shape))
    return pl.pallas_call(
        functools.partial(_ssm_sample_kernel, t_new=t_new),
        grid=(1,),
        in_specs=[full((t_new, bd, SSM_WIDTH)), full((gn, bd)), full((gn, bd)), full((1, gn)), full((1, gn)),
                  full((SSM_WIDTH, gn)), full((SSM_WIDTH, gn)), full((gn, SSM_WIDTH)), full((gn, SSM_WIDTH)),
                  full((1, SSM_WIDTH))],
        out_specs=[full((t_new, bd, SSM_WIDTH)), full((gn, bd)), full((gn, bd))],
        out_shape=[jax.ShapeDtypeStruct((t_new, bd, SSM_WIDTH), F32),
                   jax.ShapeDtypeStruct((gn, bd), F32), jax.ShapeDtypeStruct((gn, bd), F32)],
        compiler_params=_cparams(("arbitrary",)),
        name="ssm_sample",
    )(u, h0_re_t, h0_im_t, a_re, a_im, b_re, b_im, cb_re, cb_im, d_skip.reshape(1, SSM_WIDTH))


def _rms(x, gain):
    return x * lax.rsqrt(jnp.mean(x * x, axis=-1, keepdims=True) + EPS) * gain


def _gelu_tanh(x):
    return 0.5 * x * (1.0 + jnp.tanh(math.sqrt(2.0 / math.pi) * (x + 0.044715 * (x * x * x))))


def _post_body(x_ref, oa_ref, ys_ref, g1_ref, sc2_ref, sh2_ref, g2_ref, n2_ref, ag_ref, sg_ref,
               wglu_ref, wout_ref, wg_ref, wu_ref, wd_ref, o_ref):
    ya = _gelu_tanh(ys_ref[...])
    ya = ya * _sigmoid(jnp.dot(ya.astype(BF16), wglu_ref[...], preferred_element_type=F32))
    merged = jnp.concatenate([_rms(oa_ref[...], ag_ref[...]), _rms(ya, sg_ref[...])], axis=1)
    x1 = x_ref[...] + g1_ref[...] * jnp.dot(merged.astype(BF16), wout_ref[...], preferred_element_type=F32)
    h2 = (_rms(x1, n2_ref[...]) * (1.0 + sc2_ref[...]) + sh2_ref[...]).astype(BF16)
    gate = jnp.dot(h2, wg_ref[...], preferred_element_type=F32)
    up = jnp.dot(h2, wu_ref[...], preferred_element_type=F32)
    act = (gate * _sigmoid(gate) * up).astype(BF16)
    o_ref[...] = x1 + g2_ref[...] * jnp.dot(act, wd_ref[...], preferred_element_type=F32)


def _post_sample(x, o_att, y_ssm, mods, consts, weights):
    t = x.shape[0]
    tm = min(TOKEN_TILE, t)
    tok = lambda w: pl.BlockSpec((tm, w), lambda i: (i, 0))
    mod = tok(D_MODEL)
    return pl.pallas_call(
        _post_body,
        grid=(t // tm,),
        in_specs=[tok(D_MODEL), tok(ATT_WIDTH), tok(SSM_WIDTH), mod, mod, mod, mod]
                 + [_const_spec(c.shape) for c in consts] + [_const_spec(w.shape) for w in weights],
        out_specs=tok(D_MODEL),
        out_shape=jax.ShapeDtypeStruct((t, D_MODEL), F32),
        compiler_params=_cparams(("arbitrary",)),
        name="post_sample",
    )(x, o_att, y_ssm, *mods, *consts, *weights)


def kernel(x_prompt, x_sample, cache_k, cache_v, state_ssm_re, state_ssm_im, c_prompt, c_sample, norm1_g, norm2_g, w_ada, b_ada, w_in, q_gain, k_gain, ssm_a_re, ssm_a_im, ssm_log_dt, ssm_b_re, ssm_b_im, ssm_c_re, ssm_c_im, ssm_d, w_glu, attn_out_g, ssm_out_g, w_out, w_gate, w_up, w_down):
    depth = norm1_g.shape[0]
    assert depth == 1, "one decoder layer"
    b, s, _ = x_prompt.shape
    bd, t_new, _ = x_sample.shape
    n_buf = cache_k.shape[2]
    L = 0

    n_c = b + bd
    pad = (-n_c) % SUBLANES
    c_all = jnp.concatenate([c_prompt, c_sample, jnp.zeros((pad, D_MODEL), F32)], axis=0)
    mod = _ada(c_all, w_ada[L], b_ada[L].reshape(1, -1))
    mod_p = [mod[0:b, i * D_MODEL:(i + 1) * D_MODEL].reshape(b, 1, D_MODEL) for i in range(N_MOD)]
    mod_s = [jnp.repeat(mod[b:n_c, i * D_MODEL:(i + 1) * D_MODEL], t_new, axis=0) for i in range(N_MOD)]

    w_in_bf = w_in[L].astype(BF16)
    n1 = norm1_g[L].reshape(1, D_MODEL)
    qg = jnp.tile(q_gain[L], ATT_HEADS).reshape(1, ATT_WIDTH)
    kg = jnp.tile(k_gain[L], ATT_HEADS).reshape(1, ATT_WIDTH)
    consts = (norm2_g[L].reshape(1, D_MODEL), attn_out_g[L].reshape(1, ATT_WIDTH), ssm_out_g[L].reshape(1, SSM_WIDTH))
    weights = tuple(w[L].astype(BF16) for w in (w_glu, w_out, w_gate, w_up, w_down))

    tb = _s5_tables(ssm_a_re[L], ssm_a_im[L], ssm_log_dt[L], ssm_b_re[L], ssm_b_im[L])
    toe, w_pair, v_pair, a16_re, a16_im = _s5_chunk_operators(tb, ssm_c_re[L], ssm_c_im[L], ssm_d[L])

    xs = x_sample.reshape(bd * t_new, D_MODEL)
    qs, ks, vs, us = _inproj_sample(xs, mod_s[1], mod_s[0], n1, w_in_bf, qg, kg)
    qx = _expand_queries(qs.reshape(bd, t_new, ATT_WIDTH))
    ks3, vs3 = ks.reshape(bd, t_new, ATT_WIDTH), vs.reshape(bd, t_new, ATT_WIDTH)
    kt = jnp.transpose(cache_k[L], (0, 2, 3, 1)).reshape(bd, ATT_WIDTH, n_buf)
    vt = jnp.transpose(cache_v[L], (0, 2, 3, 1)).reshape(bd, ATT_WIDTH, n_buf)
    q, k, v, u, kt_p, vt_p = _inproj_prompt(x_prompt, mod_p[1], mod_p[0], n1, w_in_bf, qg, kg)
    n_first = _attn_stream_batches(b, s)
    assert 0 < n_first < bd
    shifted, o_s_first, o_att = _stream_attn(kt, vt, 0, n_first, qx, ks3, vs3, q, k, v)
    y_ssm, f_re, f_im = _ssm_prompt(u, toe, w_pair, v_pair, a16_re, a16_im)
    okt, ovt, o_s_second, y_prompt = _stream_post(
        kt, vt, shifted, n_first, bd - n_first, qx, ks3, vs3, x_prompt, o_att, y_ssm,
        (mod_p[2], mod_p[4], mod_p[3], mod_p[5]), consts, weights)
    o_att_s = jnp.concatenate([o_s_first, o_s_second], axis=0)
    gn = SSM_GROUPS * SSM_STATE
    h0_re = jnp.transpose(state_ssm_re[L], (1, 2, 0)).reshape(gn, bd)
    h0_im = jnp.transpose(state_ssm_im[L], (1, 2, 0)).reshape(gn, bd)
    us_t = jnp.transpose(us.reshape(bd, t_new, SSM_WIDTH), (1, 0, 2))
    y_ssm_t, hs_re, hs_im = _ssm_sample(us_t, h0_re, h0_im, tb, ssm_c_re[L], ssm_c_im[L], ssm_d[L], t_new)
    y_ssm_s = jnp.transpose(y_ssm_t, (1, 0, 2)).reshape(bd * t_new, SSM_WIDTH)
    y_sample = _post_sample(xs, o_att_s.reshape(bd * t_new, ATT_WIDTH), y_ssm_s,
                            (mod_s[2], mod_s[4], mod_s[3], mod_s[5]), consts, weights)

    def from_t(a, nb, keep):
        return jnp.transpose(a.reshape(nb, ATT_HEADS, HEAD_DIM, keep), (0, 3, 1, 2))[None]

    def state_from_t(a):
        return jnp.transpose(a.reshape(SSM_GROUPS, SSM_STATE, bd), (2, 0, 1))[None]

    keep = min(MAX_SPAN, s)
    return (y_prompt, y_sample.reshape(bd, t_new, D_MODEL),
            from_t(kt_p, b, keep), from_t(vt_p, b, keep),
            f_re.reshape(b, SSM_GROUPS, SSM_STATE)[None], f_im.reshape(b, SSM_GROUPS, SSM_STATE)[None],
            from_t(okt, bd, n_buf), from_t(ovt, bd, n_buf),
            state_from_t(hs_re), state_from_t(hs_im))
```

```python
import functools
import math

import jax
import jax.numpy as jnp
import numpy as np
from jax import lax
from jax.experimental import pallas as pl
from jax.experimental.pallas import tpu as pltpu

F32 = jnp.float32
BF16 = jnp.bfloat16

D_MODEL = 1024
HEAD_DIM = 64
ATT_WIDTH = 512
ATT_HEADS = 8
SSM_WIDTH = 512
SSM_CH = 16
SSM_GROUPS = 32
SSM_STATE = 64
DILATIONS = ((128, 1), (512, 4), (2048, 16))
N_BACK = 128
MAX_SPAN = 2048
FFN_HIDDEN = 2816
N_MOD = 6
EPS = 1e-6

LANES = 128
SUBLANES = 8
VMEM_LIMIT = 56 * 1024 * 1024

TOKEN_TILE = 512
SUPER = 2048
QB = 128
TILE_GROUP = 4
Q_SCALE = HEAD_DIM ** -0.5 * math.log2(math.e)
CHUNK = 16
NEG = -1e30


def _cparams(sem=None):
    return pltpu.CompilerParams(dimension_semantics=sem, vmem_limit_bytes=VMEM_LIMIT)


def _const_spec(shape):
    nd = len(shape)
    return pl.BlockSpec(shape, lambda *_: (0,) * nd, pipeline_mode=pl.Buffered(1))


def _sigmoid(x):
    return 1.0 / (1.0 + jnp.exp(-x))


def _split_bf16(a):
    hi = a.astype(BF16)
    lo = (a - hi.astype(F32)).astype(BF16)
    return hi, lo


def _ada_kernel(c_ref, w_ref, b_ref, o_ref):
    c = c_ref[...]
    a = c * _sigmoid(c)
    a_hi, a_lo = _split_bf16(a)
    w_hi, w_lo = _split_bf16(w_ref[...])
    acc = jnp.dot(a_hi, w_hi, preferred_element_type=F32)
    acc += jnp.dot(a_hi, w_lo, preferred_element_type=F32)
    acc += jnp.dot(a_lo, w_hi, preferred_element_type=F32)
    o_ref[...] = acc + b_ref[...]


def _ada(c_all, w_ada, b_ada):
    rows = c_all.shape[0]
    n = w_ada.shape[1]
    tn = 1024
    return pl.pallas_call(
        _ada_kernel,
        grid=(n // tn,),
        in_specs=[pl.BlockSpec((rows, D_MODEL), lambda j: (0, 0)),
                  pl.BlockSpec((D_MODEL, tn), lambda j: (0, j)),
                  pl.BlockSpec((1, tn), lambda j: (0, j))],
        out_specs=pl.BlockSpec((rows, tn), lambda j: (0, j)),
        out_shape=jax.ShapeDtypeStruct((rows, n), F32),
        compiler_params=_cparams(("arbitrary",)),
        name="ada",
    )(c_all, w_ada, b_ada)


def _head_rms(z, gain):
    lane = lax.broadcasted_iota(jnp.int32, (1, LANES), 1)
    lo = lane < HEAD_DIM
    outs = []
    for c in range(z.shape[1] // LANES):
        blk = z[:, c * LANES:(c + 1) * LANES]
        sq = blk * blk
        s_lo = jnp.sum(jnp.where(lo, sq, 0.0), axis=-1, keepdims=True)
        s_hi = jnp.sum(jnp.where(lo, 0.0, sq), axis=-1, keepdims=True)
        inv = jnp.where(lo, lax.rsqrt(s_lo * (1.0 / HEAD_DIM) + EPS), lax.rsqrt(s_hi * (1.0 / HEAD_DIM) + EPS))
        outs.append(blk * inv)
    return jnp.concatenate(outs, axis=1) * gain


def _inproj_body(x_ref, scale_ref, shift_ref, g_ref, w_ref, qg_ref, kg_ref, *out_refs, kt_rule=None):
    q_ref, k_ref, v_ref, u_ref = out_refs[:4]
    x = x_ref[...]
    ms = jnp.mean(x * x, axis=-1, keepdims=True)
    h = x * lax.rsqrt(ms + EPS) * g_ref[...]
    h = h * (1.0 + scale_ref[...]) + shift_ref[...]
    z = jnp.dot(h.astype(BF16), w_ref[...], preferred_element_type=F32)
    q = _head_rms(z[:, 0:ATT_WIDTH], qg_ref[...]) * Q_SCALE
    k = _head_rms(z[:, ATT_WIDTH:2 * ATT_WIDTH], kg_ref[...])
    v = z[:, 2 * ATT_WIDTH:3 * ATT_WIDTH]
    q_ref[...] = q
    k_ref[...] = k
    v_ref[...] = v
    u_ref[...] = z[:, 3 * ATT_WIDTH:]
    if kt_rule is not None:
        per_seq, kt_first = kt_rule
        kt_ref, vt_ref = out_refs[4:]

        @pl.when(pl.program_id(0) % per_seq >= kt_first)
        def _():
            kt_ref[...] = k.T
            vt_ref[...] = v.T


def _inproj_sample(x, scale, shift, g, w_bf, qg, kg):
    t = x.shape[0]
    tm = min(TOKEN_TILE, t)
    tok = pl.BlockSpec((tm, D_MODEL), lambda i: (i, 0))
    out_tok = pl.BlockSpec((tm, ATT_WIDTH), lambda i: (i, 0))
    sd = jax.ShapeDtypeStruct((t, ATT_WIDTH), F32)
    return pl.pallas_call(
        _inproj_body,
        grid=(t // tm,),
        in_specs=[tok, tok, tok, _const_spec((1, D_MODEL)), _const_spec((D_MODEL, 4 * ATT_WIDTH)),
                  _const_spec((1, ATT_WIDTH)), _const_spec((1, ATT_WIDTH))],
        out_specs=[out_tok] * 4,
        out_shape=[sd] * 4,
        compiler_params=_cparams(("arbitrary",)),
        name="inproj_sample",
    )(x, scale, shift, g, w_bf, qg, kg)


ATTN_PHASES = len(DILATIONS) + 1


def _attn_rider(q_ref, kc_ref, vc_ref, bias_ref, o_ref, kbuf, vbuf, acc_ref, m_ref, l_ref, *, n_super, side):
    step = pl.program_id(0)
    phase = step % ATTN_PHASES
    st = (step // ATTN_PHASES) % n_super
    lane = lax.broadcasted_iota(jnp.int32, (1, LANES), 1)
    lo = lane < HEAD_DIM
    nt_contract = (((1,), (1,)), ((), ()))
    n_groups = SUPER // QB // TILE_GROUP

    def pattern(p, d):
        per_res = SUPER // d // QB
        if p == 0:
            for buf, cur_ref in ((kbuf, kc_ref), (vbuf, vc_ref)):
                @pl.when(st == 0)
                def _(buf=buf):
                    buf[0:SUPER, :] = jnp.zeros((SUPER, LANES), F32)

                @pl.when(st > 0)
                def _(buf=buf):
                    buf[0:SUPER, :] = buf[SUPER:2 * SUPER, :]

                buf[SUPER:2 * SUPER, :] = cur_ref[...]

        for gi in range(n_groups):
            tiles = []
            for u in range(TILE_GROUP):
                ti = gi * TILE_GROUP + u
                r = ti // per_res
                jt = ti % per_res
                row0 = r + d * QB * jt
                ks = SUPER + row0 - d * QB
                bias = bias_ref[jnp.where(st == 0, 0, 1)] if jt == 0 else bias_ref[1]
                q = q_ref[pl.ds(row0, QB, stride=d), :]
                k = kbuf[pl.ds(ks, 2 * QB, stride=d), :].astype(BF16)
                v = vbuf[pl.ds(ks, 2 * QB, stride=d), :].astype(BF16)
                v = jnp.concatenate([v, jnp.ones_like(v)], axis=1)
                q2 = jnp.concatenate([jnp.where(lo, q, 0.0), jnp.where(lo, 0.0, q)], axis=0).astype(BF16)
                s = lax.dot_general(q2, k, nt_contract, preferred_element_type=F32)
                tiles.append((s, bias, v, row0))
            probs = []
            for s, bias, v, row0 in tiles:
                s = s + jnp.concatenate([bias, bias], axis=0)
                m = jnp.max(s, axis=-1, keepdims=True)
                pr = jnp.exp2(s - m)
                probs.append((pr.astype(BF16), m))
            for (s, bias, v, row0), (pr, m) in zip(tiles, probs):
                pv = jnp.dot(pr, v, preferred_element_type=F32)
                rows = pl.ds(row0, QB, stride=d)
                acc_ref[p, rows, :] = jnp.where(lo, pv[0:QB, 0:LANES], pv[QB:2 * QB, 0:LANES])
                m_ref[p, rows, :] = jnp.where(lo, m[0:QB], m[QB:2 * QB])
                l_ref[p, rows, :] = jnp.where(lo, pv[0:QB, LANES:2 * LANES], pv[QB:2 * QB, LANES:2 * LANES])
            side(gi, n_groups)

    for p, (_, d) in enumerate(DILATIONS):
        pl.when(phase == p)(functools.partial(pattern, p, d))

    @pl.when(phase == ATTN_PHASES - 1)
    def _():
        n_merge = SUPER // QB
        for ci in range(n_merge):
            rows = pl.ds(ci * QB, QB)
            m1, m2, m3 = m_ref[0, rows, :], m_ref[1, rows, :], m_ref[2, rows, :]
            mm = jnp.maximum(jnp.maximum(m1, m2), m3)
            w1, w2, w3 = jnp.exp2(m1 - mm), jnp.exp2(m2 - mm), jnp.exp2(m3 - mm)
            num = w1 * acc_ref[0, rows, :] + w2 * acc_ref[1, rows, :] + w3 * acc_ref[2, rows, :]
            den = w1 * l_ref[0, rows, :] + w2 * l_ref[1, rows, :] + w3 * l_ref[2, rows, :]
            o_ref[rows, :] = num / den
            if ci % (n_merge // n_groups) == 0:
                side(ci // (n_merge // n_groups), n_groups)


def _attn_bias():
    qi = np.arange(QB)[:, None]
    ki = np.arange(2 * QB)[None, :]
    dist = qi + QB - ki
    normal = (dist >= 0) & (dist <= N_BACK)
    first = normal & (ki >= QB)
    return jnp.asarray(np.where(np.stack([first, normal]), 0.0, NEG), F32)


SA_HEADS = 4
SA_WIDTH = SA_HEADS * HEAD_DIM
N_SA_IN = 7


def _stream_kernel(*refs, rider, rider_takes_side, n_rider_in, n_rider_out, n_prev, n_rider_scratch):
    kt_ref, vt_ref, qx_ref, kn_ref, vn_ref, w_ref, hm_ref = refs[:N_SA_IN]
    rider_in = refs[N_SA_IN:N_SA_IN + n_rider_in]
    n_in = N_SA_IN + n_rider_in + n_prev
    okt_ref, ovt_ref, o_ref = refs[n_in:n_in + 3]
    rider_out = refs[n_in + 3:n_in + 3 + n_rider_out]
    pad_ref, newt_ref = refs[n_in + 3 + n_rider_out:n_in + 5 + n_rider_out]
    rider_scratch = refs[len(refs) - n_rider_scratch:] if n_rider_scratch else ()
    n_buf = kt_ref.shape[1]
    t_new = kn_ref.shape[0]
    n_tiles = n_buf // LANES

    def sample_attention():
        for i, new_ref in enumerate((kn_ref, vn_ref)):
            pad_ref[...] = jnp.zeros_like(pad_ref)
            pad_ref[0:t_new, :] = new_ref[...]
            newt_ref[i] = pad_ref[...].T
        k_ext = jnp.concatenate([kt_ref[...], newt_ref[0]], axis=1)
        v_ext = jnp.concatenate([vt_ref[...], newt_ref[1]], axis=1)
        w = w_ref[...]
        s = jnp.dot(qx_ref[...], k_ext.astype(BF16), preferred_element_type=F32)
        s = jnp.where(w > 0.0, s, NEG)
        m = jnp.max(s, axis=-1, keepdims=True)
        e = w * jnp.exp2(s - m)
        l = jnp.sum(e, axis=-1, keepdims=True)
        pv = lax.dot_general(e.astype(BF16), v_ext.astype(BF16), (((1,), (1,)), ((), ())),
                             preferred_element_type=F32)
        pv = pv * hm_ref[...] / l
        o = pv[0:t_new]
        for h in range(1, SA_HEADS):
            o = o + pv[h * t_new:(h + 1) * t_new]
        o_ref[...] = o

    boundary = {}

    def shift(part, n_parts):
        per = n_tiles // n_parts
        first = part * per
        keep = lax.broadcasted_iota(jnp.int32, (1, LANES), 1) < LANES - t_new
        if part == 0:
            boundary.clear()
        for i, (src_ref, dst_ref) in enumerate(((kt_ref, okt_ref), (vt_ref, ovt_ref))):
            def rotated(j):
                if (i, j) in boundary:
                    return boundary.pop((i, j))
                t = src_ref[:, j * LANES:(j + 1) * LANES] if j < n_tiles else newt_ref[i]
                return pltpu.roll(t, LANES - t_new, axis=1)
            rolled = [rotated(j) for j in range(first, first + per + 1)]
            for j in range(per):
                dst_ref[:, (first + j) * LANES:(first + j + 1) * LANES] = jnp.where(keep, rolled[j], rolled[j + 1])
            if part + 1 < n_parts:
                boundary[(i, first + per)] = rolled[per]

    sample_attention()
    if rider_takes_side:
        rider(*rider_in, *rider_out, *rider_scratch, side=shift)
    else:
        shift(0, 1)
        rider(*rider_in, *rider_out, *rider_scratch)


def _sa_weights(t_new, n_buf):
    ext = n_buf + LANES
    pos = np.arange(ext)[None, :]
    i = np.arange(t_new)[:, None]
    delta = n_buf + i - pos
    is_real = (pos < n_buf + t_new)
    w = np.zeros((t_new, ext), np.float32)
    for window, d in DILATIONS:
        w += ((delta >= 0) & (delta % d == 0) & (delta <= window) & is_real)
    return jnp.asarray(np.tile(w, (SA_HEADS, 1)), F32)


def _head_mask(t_new):
    h = np.repeat(np.arange(SA_HEADS), t_new)[:, None]
    c = np.arange(SA_WIDTH)[None, :] // HEAD_DIM
    return (h == c).astype(np.float32)


SA_NBLK = ATT_WIDTH // SA_WIDTH


def _stream(kt, vt, prev, batch_lo, n_batches, qx, k_new, v_new, rider, rider_args, rider_in_specs,
            rider_out_specs, rider_out_shapes, name, rider_scratch=(), rider_takes_side=False):
    bd, _, n_buf = kt.shape
    t_new = k_new.shape[1]
    assert n_buf == MAX_SPAN, "window buffer must cover the longest pattern (no invalid positions)"
    rows = t_new * SA_HEADS
    buf = pl.BlockSpec((None, SA_WIDTH, n_buf), lambda i: (batch_lo + i // SA_NBLK, i % SA_NBLK, 0))
    new = pl.BlockSpec((None, t_new, SA_WIDTH), lambda i: (batch_lo + i // SA_NBLK, 0, i % SA_NBLK))
    o_blk = pl.BlockSpec((None, t_new, SA_WIDTH), lambda i: (i // SA_NBLK, 0, i % SA_NBLK))
    qx_blk = pl.BlockSpec((None, None, rows, SA_WIDTH), lambda i: (batch_lo + i // SA_NBLK, i % SA_NBLK, 0, 0))
    sd_buf = jax.ShapeDtypeStruct((bd, ATT_WIDTH, n_buf), F32)
    in_specs = [buf, buf, qx_blk, new, new, _const_spec((rows, n_buf + LANES)), _const_spec((rows, SA_WIDTH))]
    in_specs += list(rider_in_specs)
    args = [kt, vt, qx, k_new, v_new, _sa_weights(t_new, n_buf), jnp.asarray(_head_mask(t_new), F32)]
    args += list(rider_args)
    aliases = {}
    if prev is not None:
        aliases = {len(args): 0, len(args) + 1: 1}
        in_specs += [pl.BlockSpec(memory_space=pl.ANY)] * 2
        args += list(prev)
    return pl.pallas_call(
        functools.partial(_stream_kernel, rider=rider, rider_takes_side=rider_takes_side,
                          n_rider_in=len(rider_in_specs),
                          n_rider_out=len(rider_out_specs), n_prev=len(aliases),
                          n_rider_scratch=len(rider_scratch)),
        grid=(n_batches * SA_NBLK,),
        in_specs=in_specs,
        out_specs=[buf, buf, o_blk] + list(rider_out_specs),
        out_shape=[sd_buf, sd_buf, jax.ShapeDtypeStruct((n_batches, t_new, ATT_WIDTH), F32)] + list(rider_out_shapes),
        input_output_aliases=aliases,
        scratch_shapes=[pltpu.VMEM((LANES, SA_WIDTH), F32), pltpu.VMEM((2, SA_WIDTH, LANES), F32)]
                       + list(rider_scratch),
        compiler_params=_cparams(("arbitrary",)),
        name=name,
    )(*args)


def _expand_queries(q):
    bd, t_new, _ = q.shape
    hm = jnp.asarray(_head_mask(t_new), F32)
    qb = q.reshape(bd, t_new, SA_NBLK, SA_WIDTH).transpose(0, 2, 1, 3)
    return (jnp.tile(qb, (1, 1, SA_HEADS, 1)) * hm).astype(BF16)


def _rider_tiling(b, s, steps):
    assert (b * s) % steps == 0
    tm = b * s // steps
    assert tm % LANES == 0 and s % tm == 0
    return tm, s // tm


def _inproj_prompt(x, scale, shift, g, w_bf, qg, kg):
    b, s, _ = x.shape
    tm = TOKEN_TILE
    per_seq = s // tm
    keep = min(MAX_SPAN, s)
    kt_first = per_seq - keep // tm
    tok = lambda w: pl.BlockSpec((tm, w), lambda i: (i, 0))
    mod = pl.BlockSpec((None, 1, D_MODEL), lambda i: (i // per_seq, 0, 0))
    out_t = pl.BlockSpec((None, ATT_WIDTH, tm), lambda i: (i // per_seq, 0, jnp.maximum(i % per_seq - kt_first, 0)))
    sd = jax.ShapeDtypeStruct((b * s, ATT_WIDTH), F32)
    sdt = jax.ShapeDtypeStruct((b, ATT_WIDTH, keep), F32)
    q, k, v, u, kt_p, vt_p = pl.pallas_call(
        functools.partial(_inproj_body, kt_rule=(per_seq, kt_first)),
        grid=(b * per_seq,),
        in_specs=[tok(D_MODEL), mod, mod, _const_spec((1, D_MODEL)), _const_spec((D_MODEL, 4 * ATT_WIDTH)),
                  _const_spec((1, ATT_WIDTH)), _const_spec((1, ATT_WIDTH))],
        out_specs=[tok(ATT_WIDTH)] * 4 + [out_t] * 2,
        out_shape=[sd] * 4 + [sdt] * 2,
        compiler_params=_cparams(("arbitrary",)),
        name="inproj_prompt",
    )(x.reshape(b * s, D_MODEL), scale, shift, g, w_bf, qg, kg)
    r3 = lambda a: a.reshape(b, s, ATT_WIDTH)
    return r3(q), r3(k), r3(v), r3(u), kt_p, vt_p


def _attn_stream_batches(b, s):
    n_super = s // SUPER
    units = b * (ATT_WIDTH // LANES) * n_super
    steps = units * ATTN_PHASES
    assert steps % SA_NBLK == 0
    return steps // SA_NBLK


def _stream_attn(kt, vt, batch_lo, n_batches, qx, k_new, v_new, q, k, v):
    b, s, _ = q.shape
    assert s % SUPER == 0
    n_super = s // SUPER
    n_hp = ATT_WIDTH // LANES
    assert n_batches == _attn_stream_batches(b, s)

    n_units = b * n_hp * n_super

    def unit_block(lead):
        def index(i):
            un = jnp.minimum((i + lead) // ATTN_PHASES, n_units - 1)
            return un // (n_hp * n_super), un % n_super, (un // n_super) % n_hp
        return pl.BlockSpec((None, SUPER, LANES), index)

    outs = _stream(
        kt, vt, None, batch_lo, n_batches, qx, k_new, v_new,
        functools.partial(_attn_rider, n_super=n_super),
        (q, k, v, _attn_bias()),
        [unit_block(1), unit_block(ATTN_PHASES - 1), unit_block(ATTN_PHASES - 1), _const_spec((2, QB, 2 * QB))],
        [unit_block(0)], [jax.ShapeDtypeStruct((b, s, ATT_WIDTH), F32)], "stream_attn",
        rider_scratch=[pltpu.VMEM((2 * SUPER, LANES), F32)] * 2 + [pltpu.VMEM((len(DILATIONS), SUPER, LANES), F32)] * 3,
        rider_takes_side=True)
    okt, ovt, o_s, o_att = outs
    return (okt, ovt), o_s, o_att


def _stream_post(kt, vt, prev, batch_lo, n_batches, qx, k_new, v_new, x, o_att, y_ssm, mods, consts, weights):
    b, s, _ = x.shape
    tm, per_seq = _rider_tiling(b, s, n_batches * SA_NBLK)
    tok = lambda w: pl.BlockSpec((tm, w), lambda i: (i, 0))
    mod = pl.BlockSpec((None, 1, D_MODEL), lambda i: (i // per_seq, 0, 0))
    okt, ovt, o_s, y = _stream(
        kt, vt, prev, batch_lo, n_batches, qx, k_new, v_new, _post_body,
        (x.reshape(b * s, D_MODEL), o_att.reshape(b * s, ATT_WIDTH), y_ssm.reshape(b * s, SSM_WIDTH),
         *mods, *consts, *weights),
        [tok(D_MODEL), tok(ATT_WIDTH), tok(SSM_WIDTH), mod, mod, mod, mod]
        + [_const_spec(c.shape) for c in consts] + [_const_spec(w.shape) for w in weights],
        [tok(D_MODEL)], [jax.ShapeDtypeStruct((b * s, D_MODEL), F32)], "stream_post")
    return okt, ovt, o_s, y.reshape(b, s, D_MODEL)


def _s5_tables(a_re, a_im, log_dt, b_re, b_im):
    g, n = a_re.shape
    dt = jnp.exp(log_dt)[:, None]
    x, y = dt * a_re, dt * a_im
    ks = jnp.arange(CHUNK + 1, dtype=F32)
    mag = jnp.exp(x[:, :, None] * ks)
    pw_re, pw_im = mag * jnp.cos(y[:, :, None] * ks), mag * jnp.sin(y[:, :, None] * ks)
    e_re = jnp.expm1(x) * jnp.cos(y) - 2.0 * jnp.sin(0.5 * y) ** 2
    e_im = jnp.exp(x) * jnp.sin(y)
    den = a_re * a_re + a_im * a_im
    f_re = (e_re * a_re + e_im * a_im) / den
    f_im = (e_im * a_re - e_re * a_im) / den
    bb_re = f_re[..., None] * b_re - f_im[..., None] * b_im
    bb_im = f_re[..., None] * b_im + f_im[..., None] * b_re
    return dict(pw_re=pw_re, pw_im=pw_im, bb_re=bb_re, bb_im=bb_im)


def _s5_chunk_operators(tb, c_re, c_im, d_skip):
    pw_re, pw_im, bb_re, bb_im = tb["pw_re"], tb["pw_im"], tb["bb_re"], tb["bb_im"]
    g, n, ch = bb_re.shape
    L = CHUNK
    hp = lax.Precision.HIGHEST
    ct_re, ct_im = c_re.transpose(0, 2, 1), c_im.transpose(0, 2, 1)
    pr, pi = pw_re[:, :, :, None], pw_im[:, :, :, None]
    ca_re = ct_re[:, :, None, :] * pr - ct_im[:, :, None, :] * pi
    ca_im = ct_re[:, :, None, :] * pi + ct_im[:, :, None, :] * pr
    a_cat = jnp.concatenate([ca_re[:, :, :L].reshape(g, n, L * ch), -ca_im[:, :, :L].reshape(g, n, L * ch)], axis=1)
    b_cat = jnp.concatenate([bb_re, bb_im], axis=1)
    base = jnp.einsum("gnd,gnx->gdx", b_cat, a_cat, precision=hp)
    skip = jnp.eye(ch, dtype=F32)[None] * d_skip.reshape(g, ch, 1)
    base = base + jnp.pad(skip, ((0, 0), (0, 0), (0, (L - 1) * ch)))
    toe = jnp.stack([jnp.pad(base, ((0, 0), (0, 0), (s * ch, 0)))[:, :, :L * ch] for s in range(L)], axis=1)
    toe = toe.reshape(g, L * ch, L * ch)
    rv_re = pw_re[:, :, L - 1::-1].transpose(0, 2, 1)[:, :, None, :]
    rv_im = pw_im[:, :, L - 1::-1].transpose(0, 2, 1)[:, :, None, :]
    bt_re, bt_im = bb_re.transpose(0, 2, 1)[:, None], bb_im.transpose(0, 2, 1)[:, None]
    ws_re = (rv_re * bt_re - rv_im * bt_im).reshape(g, L * ch, n)
    ws_im = (rv_re * bt_im + rv_im * bt_re).reshape(g, L * ch, n)
    vr = ca_re[:, :, 1:].reshape(g, n, L * ch)
    vi = (-ca_im[:, :, 1:]).reshape(g, n, L * ch)
    gp = g // 2
    z_w = jnp.zeros((gp, L * ch, n), F32)
    wr, wi = ws_re.reshape(gp, 2, L * ch, n), ws_im.reshape(gp, 2, L * ch, n)
    w_pair = jnp.concatenate([
        jnp.concatenate([wr[:, 0], z_w, wi[:, 0], z_w], axis=2),
        jnp.concatenate([z_w, wr[:, 1], z_w, wi[:, 1]], axis=2)], axis=1)
    z_v = jnp.zeros((gp, n, L * ch), F32)
    vr, vi = vr.reshape(gp, 2, n, L * ch), vi.reshape(gp, 2, n, L * ch)
    v_pair = jnp.concatenate([
        jnp.concatenate([vr[:, 0], z_v], axis=2), jnp.concatenate([z_v, vr[:, 1]], axis=2),
        jnp.concatenate([vi[:, 0], z_v], axis=2), jnp.concatenate([z_v, vi[:, 1]], axis=2)], axis=1)
    a16_re = pw_re[:, :, L].reshape(1, g * n)
    a16_im = pw_im[:, :, L].reshape(1, g * n)
    return toe.astype(BF16), w_pair.astype(BF16), v_pair.astype(BF16), a16_re, a16_im


GROUPS_PER_BLOCK = LANES // SSM_CH
REGROUP_ROWS = 32


def _granule_transpose(arrs):
    n = GROUPS_PER_BLOCK
    gran = lax.broadcasted_iota(jnp.int32, (1, LANES), 1) // SSM_CH
    cur = list(arrs)
    k = n // 2
    while k >= 1:
        keep_low = (gran % (2 * k)) < k
        nxt = list(cur)
        for t in range(n):
            if t % (2 * k) < k:
                a, b = cur[t], cur[t + k]
                nxt[t] = jnp.where(keep_low, a, pltpu.roll(b, k * SSM_CH, axis=1))
                nxt[t + k] = jnp.where(keep_low, pltpu.roll(a, LANES - k * SSM_CH, axis=1), b)
        cur = nxt
        k //= 2
    return cur


def _ssm_a_kernel(u_ref, w_ref, xg_ref, sre_ref, sim_ref, xs_ref):
    nb, s_len, _ = u_ref.shape
    n_chunks = s_len // CHUNK
    half = CHUNK // 2

    for b in range(nb):
        def regroup(ct, c, b=b):
            c0 = pl.multiple_of(ct * REGROUP_ROWS, REGROUP_ROWS)
            rows = pl.ds(b * n_chunks + c0, REGROUP_ROWS)
            for j in range(2):
                z = [u_ref[b, pl.ds(c0 * CHUNK + half * j + tt, REGROUP_ROWS, stride=CHUNK), :]
                     for tt in range(half)]
                for p, xp in enumerate(_granule_transpose(z)):
                    xs_ref[2 * p + j, rows, :] = xp
            return c
        lax.fori_loop(0, n_chunks // REGROUP_ROWS, regroup, 0)

    for col in range(2 * GROUPS_PER_BLOCK):
        xg_ref[:, col * LANES:(col + 1) * LANES] = xs_ref[col].astype(BF16)
    for pp in range(GROUPS_PER_BLOCK // 2):
        xp = xg_ref[:, pp * 4 * LANES:(pp + 1) * 4 * LANES]
        st = jnp.dot(xp, w_ref[pp], preferred_element_type=F32)
        sre_ref[pp] = st[:, 0:LANES]
        sim_ref[pp] = st[:, LANES:2 * LANES]


def _ssm_scan_kernel(sre_ref, sim_ref, are_ref, aim_ref, hre_ref, him_ref, fre_ref, fim_ref, *, nb):
    ncb, rows, _ = sre_ref.shape
    n_chunks = rows // nb
    a_re = [are_ref[cb] for cb in range(ncb)]
    a_im = [aim_ref[cb] for cb in range(ncb)]

    def body(c, carry):
        r = pl.ds(c, nb, stride=n_chunks)
        out = []
        for cb in range(ncb):
            h_re, h_im = carry[2 * cb], carry[2 * cb + 1]
            hre_ref[cb, r, :] = h_re
            him_ref[cb, r, :] = h_im
            out.append(a_re[cb] * h_re - a_im[cb] * h_im + sre_ref[cb, r, :])
            out.append(a_re[cb] * h_im + a_im[cb] * h_re + sim_ref[cb, r, :])
        return tuple(out)

    z = jnp.zeros((nb, LANES), F32)
    fin = lax.fori_loop(0, n_chunks, body, (z,) * (2 * ncb), unroll=4)
    for cb in range(ncb):
        fre_ref[cb] = fin[2 * cb]
        fim_ref[cb] = fin[2 * cb + 1]


def _ssm_c_kernel(xg_ref, hre_ref, him_ref, t_ref, v_ref, y_ref, ys_ref):
    nb, s_len, _ = y_ref.shape
    n_chunks = s_len // CHUNK
    half = CHUNK // 2
    for pp in range(GROUPS_PER_BLOCK // 2):
        hcat = jnp.concatenate([hre_ref[pp], him_ref[pp]], axis=1).astype(BF16)
        inter = jnp.dot(hcat, v_ref[pp], preferred_element_type=F32)
        for e in range(2):
            g = 2 * pp + e
            intra = jnp.dot(xg_ref[:, g * 2 * LANES:(g + 1) * 2 * LANES], t_ref[g],
                            preferred_element_type=F32)
            yg = intra + inter[:, e * 2 * LANES:(e + 1) * 2 * LANES]
            ys_ref[2 * g] = yg[:, 0:LANES]
            ys_ref[2 * g + 1] = yg[:, LANES:2 * LANES]

    for b in range(nb):
        def regroup(ct, c, b=b):
            c0 = pl.multiple_of(ct * REGROUP_ROWS, REGROUP_ROWS)
            rows = pl.ds(b * n_chunks + c0, REGROUP_ROWS)
            for j in range(2):
                yp =[ys_ref[2 * p + j, rows, :] for p in range(GROUPS_PER_BLOCK)]
                for tt, zt in enumerate(_granule_transpose(yp)):
                    y_ref[b, pl.ds(c0 * CHUNK + half * j + tt, REGROUP_ROWS, stride=CHUNK), :] = zt
            return c
        lax.fori_loop(0, n_chunks // REGROUP_ROWS, regroup, 0)


def _ssm_prompt(u, toe, w_pair, v_pair, a16_re, a16_im):
    b, s, _ = u.shape
    rows = b * (s // CHUNK)
    nblk = SSM_WIDTH // LANES
    gw = GROUPS_PER_BLOCK * CHUNK * SSM_CH
    ncb = GROUPS_PER_BLOCK * SSM_STATE // LANES
    ublk = pl.BlockSpec((b, s, LANES), lambda j: (0, 0, j))
    sblk = pl.BlockSpec((ncb, rows, LANES), lambda j: (j, 0, 0))
    sd_h = jax.ShapeDtypeStruct((nblk * ncb, rows, LANES), F32)
    xg, s_re, s_im = pl.pallas_call(
        _ssm_a_kernel,
        grid=(nblk,),
        in_specs=[ublk, pl.BlockSpec((GROUPS_PER_BLOCK // 2, 4 * LANES, 2 * LANES), lambda j: (j, 0, 0))],
        out_specs=[pl.BlockSpec((rows, gw), lambda j: (0, j)), sblk, sblk],
        out_shape=[jax.ShapeDtypeStruct((rows, nblk * gw), BF16), sd_h, sd_h],
        scratch_shapes=[pltpu.VMEM((gw // LANES, rows, LANES), F32)],
        compiler_params=_cparams(("arbitrary",)),
        name="ssm_chunk_states",
    )(u, w_pair)

    ablk = pl.BlockSpec((ncb, 1, LANES), lambda j: (j, 0, 0))
    fblk = pl.BlockSpec((ncb, b, LANES), lambda j: (j, 0, 0))
    sd_f = jax.ShapeDtypeStruct((nblk * ncb, b, LANES), F32)
    a16_re = a16_re.reshape(nblk * ncb, 1, LANES)
    a16_im = a16_im.reshape(nblk * ncb, 1, LANES)
    h_re, h_im, f_re, f_im = pl.pallas_call(
        functools.partial(_ssm_scan_kernel, nb=b),
        grid=(nblk,),
        in_specs=[sblk, sblk, ablk, ablk],
        out_specs=[sblk, sblk, fblk, fblk],
        out_shape=[sd_h, sd_h, sd_f, sd_f],
        compiler_params=_cparams(("arbitrary",)),
        name="ssm_scan",
    )(s_re, s_im, a16_re, a16_im)

    y = pl.pallas_call(
        _ssm_c_kernel,
        grid=(nblk,),
        in_specs=[pl.BlockSpec((rows, gw), lambda j: (0, j)), sblk, sblk,
                  pl.BlockSpec((GROUPS_PER_BLOCK, 2 * LANES, 2 * LANES), lambda j: (j, 0, 0)),
                  pl.BlockSpec((GROUPS_PER_BLOCK // 2, 2 * LANES, 4 * LANES), lambda j: (j, 0, 0))],
        out_specs=ublk,
        out_shape=jax.ShapeDtypeStruct((b, s, SSM_WIDTH), F32),
        scratch_shapes=[pltpu.VMEM((gw // LANES, rows, LANES), F32)],
        compiler_params=_cparams(("arbitrary",)),
        name="ssm_outputs",
    )(xg, h_re, h_im, toe, v_pair)
    to_rows = lambda f: jnp.transpose(f, (1, 0, 2)).reshape(b, nblk * ncb * LANES)
    return y, to_rows(f_re), to_rows(f_im)


def _ssm_sample_kernel(u_ref, hre_ref, him_ref, are_ref, aim_ref, bre_ref, bim_ref, cre_ref, cim_ref, d_ref,
                       y_ref, ore_ref, oim_ref, *, t_new):
    h_re = hre_ref[...].T
    h_im = him_ref[...].T
    a_re, a_im = are_ref[...], aim_ref[...]
    bd = h_re.shape[0]
    for t in range(t_new):
        u = u_ref[t]
        ub = u.astype(BF16)
        n_re = a_re * h_re - a_im * h_im + jnp.dot(ub, bre_ref[...], preferred_element_type=F32)
        n_im = a_re * h_im + a_im * h_re + jnp.dot(ub, bim_ref[...], preferred_element_type=F32)
        h_re, h_im = n_re, n_im
        y = (jnp.dot(h_re.astype(BF16), cre_ref[...], preferred_element_type=F32)
             + jnp.dot(h_im.astype(BF16), cim_ref[...], preferred_element_type=F32) + d_ref[...] * u)
        y_ref[t] = y
    ore_ref[...] = h_re.T
    oim_ref[...] = h_im.T


def _block_diag(m):
    g, r, c = m.shape
    spread = jnp.asarray(np.tile(np.eye(c, dtype=np.float32), (1, g)), BF16)
    tiled = jnp.dot(m.reshape(g * r, c).astype(BF16), spread, preferred_element_type=F32)
    keep = np.arange(g * r)[:, None] // r == np.arange(g * c)[None, :] // c
    return jnp.where(jnp.asarray(keep), tiled, 0.0).astype(BF16)


def _ssm_sample(u, h0_re_t, h0_im_t, tb, c_re, c_im, d_skip, t_new):
    gn = SSM_GROUPS * SSM_STATE
    a_re = tb["pw_re"][:, :, 1].reshape(1, gn)
    a_im = tb["pw_im"][:, :, 1].reshape(1, gn)
    b_re = _block_diag(tb["bb_re"].transpose(0, 2, 1))
    b_im = _block_diag(tb["bb_im"].transpose(0, 2, 1))
    cb_re = _block_diag(c_re.transpose(0, 2, 1))
    cb_im = _block_diag(-c_im.transpose(0, 2, 1))
    bd = u.shape[1]
    full = lambda shape: pl.BlockSpec(shape, lambda i: (0,) * len(shape))
    return pl.pallas_call(
        functools.partial(_ssm_sample_kernel, t_new=t_new),
        grid=(1,),
        in_specs=[full((t_new, bd, SSM_WIDTH)), full((gn, bd)), full((gn, bd)), full((1, gn)), full((1, gn)),
                  full((SSM_WIDTH, gn)), full((SSM_WIDTH, gn)), full((gn, SSM_WIDTH)), full((gn, SSM_WIDTH)),
                  full((1, SSM_WIDTH))],
        out_specs=[full((t_new, bd, SSM_WIDTH)), full((gn, bd)), full((gn, bd))],
        out_shape=[jax.ShapeDtypeStruct((t_new, bd, SSM_WIDTH), F32),
                   jax.ShapeDtypeStruct((gn, bd), F32), jax.ShapeDtypeStruct((gn, bd), F32)],
        compiler_params=_cparams(("arbitrary",)),
        name="ssm_sample",
    )(u, h0_re_t, h0_im_t, a_re, a_im, b_re, b_im, cb_re, cb_im, d_skip.reshape(1, SSM_WIDTH))


def _rms(x, gain):
    return x * lax.rsqrt(jnp.mean(x * x, axis=-1, keepdims=True) + EPS) * gain


def _gelu_tanh(x):
    return 0.5 * x * (1.0 + jnp.tanh(math.sqrt(2.0 / math.pi) * (x + 0.044715 * (x * x * x))))


def _post_body(x_ref, oa_ref, ys_ref, g1_ref, sc2_ref, sh2_ref, g2_ref, n2_ref, ag_ref, sg_ref,
               wglu_ref, wout_ref, wg_ref, wu_ref, wd_ref, o_ref):
    ya = _gelu_tanh(ys_ref[...])
    ya = ya * _sigmoid(jnp.dot(ya.astype(BF16), wglu_ref[...], preferred_element_type=F32))
    merged = jnp.concatenate([_rms(oa_ref[...], ag_ref[...]), _rms(ya, sg_ref[...])], axis=1)
    x1 = x_ref[...] + g1_ref[...] * jnp.dot(merged.astype(BF16), wout_ref[...], preferred_element_type=F32)
    h2 = (_rms(x1, n2_ref[...]) * (1.0 + sc2_ref[...]) + sh2_ref[...]).astype(BF16)
    gate = jnp.dot(h2, wg_ref[...], preferred_element_type=F32)
    up = jnp.dot(h2, wu_ref[...], preferred_element_type=F32)
    act = (gate * _sigmoid(gate) * up).astype(BF16)
    o_ref[...] = x1 + g2_ref[...] * jnp.dot(act, wd_ref[...], preferred_element_type=F32)


def _post_sample(x, o_att, y_ssm, mods, consts, weights):
    t = x.shape[0]
    tm = min(TOKEN_TILE, t)
    tok = lambda w: pl.BlockSpec((tm, w), lambda i: (i, 0))
    mod = tok(D_MODEL)
    return pl.pallas_call(
        _post_body,
        grid=(t // tm,),
        in_specs=[tok(D_MODEL), tok(ATT_WIDTH), tok(SSM_WIDTH), mod, mod, mod, mod]
                 + [_const_spec(c.shape) for c in consts] + [_const_spec(w.shape) for w in weights],
        out_specs=tok(D_MODEL),
        out_shape=jax.ShapeDtypeStruct((t, D_MODEL), F32),
        compiler_params=_cparams(("arbitrary",)),
        name="post_sample",
    )(x, o_att, y_ssm, *mods, *consts, *weights)


def kernel(x_prompt, x_sample, cache_k, cache_v, state_ssm_re, state_ssm_im, c_prompt, c_sample, norm1_g, norm2_g, w_ada, b_ada, w_in, q_gain, k_gain, ssm_a_re, ssm_a_im, ssm_log_dt, ssm_b_re, ssm_b_im, ssm_c_re, ssm_c_im, ssm_d, w_glu, attn_out_g, ssm_out_g, w_out, w_gate, w_up, w_down):
    depth = norm1_g.shape[0]
    assert depth == 1, "one decoder layer"
    b, s, _ = x_prompt.shape
    bd, t_new, _ = x_sample.shape
    n_buf = cache_k.shape[2]
    L = 0

    n_c = b + bd
    pad = (-n_c) % SUBLANES
    c_all = jnp.concatenate([c_prompt, c_sample, jnp.zeros((pad, D_MODEL), F32)], axis=0)
    mod = _ada(c_all, w_ada[L], b_ada[L].reshape(1, -1))
    mod_p = [mod[0:b, i * D_MODEL:(i + 1) * D_MODEL].reshape(b, 1, D_MODEL) for i in range(N_MOD)]
    mod_s = [jnp.repeat(mod[b:n_c, i * D_MODEL:(i + 1) * D_MODEL], t_new, axis=0) for i in range(N_MOD)]

    w_in_bf = w_in[L].astype(BF16)
    n1 = norm1_g[L].reshape(1, D_MODEL)
    qg = jnp.tile(q_gain[L], ATT_HEADS).reshape(1, ATT_WIDTH)
    kg = jnp.tile(k_gain[L], ATT_HEADS).reshape(1, ATT_WIDTH)
    consts = (norm2_g[L].reshape(1, D_MODEL), attn_out_g[L].reshape(1, ATT_WIDTH), ssm_out_g[L].reshape(1, SSM_WIDTH))
    weights = tuple(w[L].astype(BF16) for w in (w_glu, w_out, w_gate, w_up, w_down))

    tb = _s5_tables(ssm_a_re[L], ssm_a_im[L], ssm_log_dt[L], ssm_b_re[L], ssm_b_im[L])
    toe, w_pair, v_pair, a16_re, a16_im = _s5_chunk_operators(tb, ssm_c_re[L], ssm_c_im[L], ssm_d[L])

    xs = x_sample.reshape(bd * t_new, D_MODEL)
    qs, ks, vs, us = _inproj_sample(xs, mod_s[1], mod_s[0], n1, w_in_bf, qg, kg)
    qx = _expand_queries(qs.reshape(bd, t_new, ATT_WIDTH))
    ks3, vs3 = ks.reshape(bd, t_new, ATT_WIDTH), vs.reshape(bd, t_new, ATT_WIDTH)
    kt = jnp.transpose(cache_k[L], (0, 2, 3, 1)).reshape(bd, ATT_WIDTH, n_buf)
    vt = jnp.transpose(cache_v[L], (0, 2, 3, 1)).reshape(bd, ATT_WIDTH, n_buf)
    q, k, v, u, kt_p, vt_p = _inproj_prompt(x_prompt, mod_p[1], mod_p[0], n1, w_in_bf, qg, kg)
    n_first = _attn_stream_batches(b, s)
    assert 0 < n_first < bd
    shifted, o_s_first, o_att = _stream_attn(kt, vt, 0, n_first, qx, ks3, vs3, q, k, v)
    y_ssm, f_re, f_im = _ssm_prompt(u, toe, w_pair, v_pair, a16_re, a16_im)
    okt, ovt, o_s_second, y_prompt = _stream_post(
        kt, vt, shifted, n_first, bd - n_first, qx, ks3, vs3, x_prompt, o_att, y_ssm,
        (mod_p[2], mod_p[4], mod_p[3], mod_p[5]), consts, weights)
    o_att_s = jnp.concatenate([o_s_first, o_s_second], axis=0)
    gn = SSM_GROUPS * SSM_STATE
    h0_re = jnp.transpose(state_ssm_re[L], (1, 2, 0)).reshape(gn, bd)
    h0_im = jnp.transpose(state_ssm_im[L], (1, 2, 0)).reshape(gn, bd)
    us_t = jnp.transpose(us.reshape(bd, t_new, SSM_WIDTH), (1, 0, 2))
    y_ssm_t, hs_re, hs_im = _ssm_sample(us_t, h0_re, h0_im, tb, ssm_c_re[L], ssm_c_im[L], ssm_d[L], t_new)
    y_ssm_s = jnp.transpose(y_ssm_t, (1, 0, 2)).reshape(bd * t_new, SSM_WIDTH)
    y_sample = _post_sample(xs, o_att_s.reshape(bd * t_new, ATT_WIDTH), y_ssm_s,
                            (mod_s[2], mod_s[4], mod_s[3], mod_s[5]), consts, weights)

    def from_t(a, nb, keep):
        return jnp.transpose(a.reshape(nb, ATT_HEADS, HEAD_DIM, keep), (0, 3, 1, 2))[None]

    def state_from_t(a):
        return jnp.transpose(a.reshape(SSM_GROUPS, SSM_STATE, bd), (2, 0, 1))[None]

    keep = min(MAX_SPAN, s)
    return (y_prompt, y_sample.reshape(bd, t_new, D_MODEL),
            from_t(kt_p, b, keep), from_t(vt_p, b, keep),
            f_re.reshape(b, SSM_GROUPS, SSM_STATE)[None], f_im.reshape(b, SSM_GROUPS, SSM_STATE)[None],
            from_t(okt, bd, n_buf), from_t(ovt, bd, n_buf),
            state_from_t(hs_re), state_from_t(hs_im))
```

```python
import functools
import math

import jax
import jax.numpy as jnp
import numpy as np
from jax import lax
from jax.experimental import pallas as pl
from jax.experimental.pallas import tpu as pltpu

F32 = jnp.float32
BF16 = jnp.bfloat16

D_MODEL = 1024
HEAD_DIM = 64
ATT_WIDTH = 512
ATT_HEADS = 8
SSM_WIDTH = 512
SSM_CH = 16
SSM_GROUPS = 32
SSM_STATE = 64
DILATIONS = ((128, 1), (512, 4), (2048, 16))
N_BACK = 128
MAX_SPAN = 2048
FFN_HIDDEN = 2816
N_MOD = 6
EPS = 1e-6

LANES = 128
SUBLANES = 8
VMEM_LIMIT = 56 * 1024 * 1024

TOKEN_TILE = 512
SUPER = 2048
QB = 128
TILE_GROUP = 4
Q_SCALE = HEAD_DIM ** -0.5 * math.log2(math.e)
CHUNK = 16
NEG = -1e30


def _cparams(sem=None):
    return pltpu.CompilerParams(dimension_semantics=sem, vmem_limit_bytes=VMEM_LIMIT)


def _const_spec(shape):
    nd = len(shape)
    return pl.BlockSpec(shape, lambda *_: (0,) * nd, pipeline_mode=pl.Buffered(1))


def _sigmoid(x):
    return 1.0 / (1.0 + jnp.exp(-x))


def _split_bf16(a):
    hi = a.astype(BF16)
    lo = (a - hi.astype(F32)).astype(BF16)
    return hi, lo


def _ada_kernel(c_ref, w_ref, b_ref, o_ref):
    c = c_ref[...]
    a = c * _sigmoid(c)
    a_hi, a_lo = _split_bf16(a)
    w_hi, w_lo = _split_bf16(w_ref[...])
    acc = jnp.dot(a_hi, w_hi, preferred_element_type=F32)
    acc += jnp.dot(a_hi, w_lo, preferred_element_type=F32)
    acc += jnp.dot(a_lo, w_hi, preferred_element_type=F32)
    o_ref[...] = acc + b_ref[...]


def _ada(c_all, w_ada, b_ada):
    rows = c_all.shape[0]
    n = w_ada.shape[1]
    tn = 1024
    return pl.pallas_call(
        _ada_kernel,
        grid=(n // tn,),
        in_specs=[pl.BlockSpec((rows, D_MODEL), lambda j: (0, 0)),
                  pl.BlockSpec((D_MODEL, tn), lambda j: (0, j)),
                  pl.BlockSpec((1, tn), lambda j: (0, j))],
        out_specs=pl.BlockSpec((rows, tn), lambda j: (0, j)),
        out_shape=jax.ShapeDtypeStruct((rows, n), F32),
        compiler_params=_cparams(("arbitrary",)),
        name="ada",
    )(c_all, w_ada, b_ada)


def _head_rms(z, gain):
    lane = lax.broadcasted_iota(jnp.int32, (1, LANES), 1)
    lo = lane < HEAD_DIM
    outs = []
    for c in range(z.shape[1] // LANES):
        blk = z[:, c * LANES:(c + 1) * LANES]
        sq = blk * blk
        s_lo = jnp.sum(jnp.where(lo, sq, 0.0), axis=-1, keepdims=True)
        s_hi = jnp.sum(jnp.where(lo, 0.0, sq), axis=-1, keepdims=True)
        inv = jnp.where(lo, lax.rsqrt(s_lo * (1.0 / HEAD_DIM) + EPS), lax.rsqrt(s_hi * (1.0 / HEAD_DIM) + EPS))
        outs.append(blk * inv)
    return jnp.concatenate(outs, axis=1) * gain


def _inproj_body(x_ref, scale_ref, shift_ref, g_ref, w_ref, qg_ref, kg_ref, *out_refs, kt_rule=None):
    q_ref, k_ref, v_ref, u_ref = out_refs[:4]
    x = x_ref[...]
    ms = jnp.mean(x * x, axis=-1, keepdims=True)
    h = x * lax.rsqrt(ms + EPS) * g_ref[...]
    h = h * (1.0 + scale_ref[...]) + shift_ref[...]
    z = jnp.dot(h.astype(BF16), w_ref[...], preferred_element_type=F32)
    q = _head_rms(z[:, 0:ATT_WIDTH], qg_ref[...]) * Q_SCALE
    k = _head_rms(z[:, ATT_WIDTH:2 * ATT_WIDTH], kg_ref[...])
    v = z[:, 2 * ATT_WIDTH:3 * ATT_WIDTH]
    q_ref[...] = q
    k_ref[...] = k
    v_ref[...] = v
    u_ref[...] = z[:, 3 * ATT_WIDTH:]
    if kt_rule is not None:
        per_seq, kt_first = kt_rule
        kt_ref, vt_ref = out_refs[4:]

        @pl.when(pl.program_id(0) % per_seq >= kt_first)
        def _():
            kt_ref[...] = k.T
            vt_ref[...] = v.T


def _inproj_sample(x, scale, shift, g, w_bf, qg, kg):
    t = x.shape[0]
    tm = min(TOKEN_TILE, t)
    tok = pl.BlockSpec((tm, D_MODEL), lambda i: (i, 0))
    out_tok = pl.BlockSpec((tm, ATT_WIDTH), lambda i: (i, 0))
    sd = jax.ShapeDtypeStruct((t, ATT_WIDTH), F32)
    return pl.pallas_call(
        _inproj_body,
        grid=(t // tm,),
        in_specs=[tok, tok, tok, _const_spec((1, D_MODEL)), _const_spec((D_MODEL, 4 * ATT_WIDTH)),
                  _const_spec((1, ATT_WIDTH)), _const_spec((1, ATT_WIDTH))],
        out_specs=[out_tok] * 4,
        out_shape=[sd] * 4,
        compiler_params=_cparams(("arbitrary",)),
        name="inproj_sample",
    )(x, scale, shift, g, w_bf, qg, kg)


ATTN_PHASES = len(DILATIONS) + 1


def _attn_rider(q_ref, kc_ref, vc_ref, bias_ref, o_ref, kbuf, vbuf, acc_ref, m_ref, l_ref, *, n_super, side):
    step = pl.program_id(0)
    phase = step % ATTN_PHASES
    st = (step // ATTN_PHASES) % n_super
    lane = lax.broadcasted_iota(jnp.int32, (1, LANES), 1)
    lo = lane < HEAD_DIM
    nt_contract = (((1,), (1,)), ((), ()))
    n_groups = SUPER // QB // TILE_GROUP

    def pattern(p, d):
        per_res = SUPER // d // QB
        if p == 0:
            for buf, cur_ref in ((kbuf, kc_ref), (vbuf, vc_ref)):
                @pl.when(st == 0)
                def _(buf=buf):
                    buf[0:SUPER, :] = jnp.zeros((SUPER, LANES), F32)

                @pl.when(st > 0)
                def _(buf=buf):
                    buf[0:SUPER, :] = buf[SUPER:2 * SUPER, :]

                buf[SUPER:2 * SUPER, :] = cur_ref[...]

        for gi in range(n_groups):
            tiles = []
            for u in range(TILE_GROUP):
                ti = gi * TILE_GROUP + u
                r = ti // per_res
                jt = ti % per_res
                row0 = r + d * QB * jt
                ks = SUPER + row0 - d * QB
                bias = bias_ref[jnp.where(st == 0, 0, 1)] if jt == 0 else bias_ref[1]
                q = q_ref[pl.ds(row0, QB, stride=d), :]
                k = kbuf[pl.ds(ks, 2 * QB, stride=d), :].astype(BF16)
                v = vbuf[pl.ds(ks, 2 * QB, stride=d), :].astype(BF16)
                v = jnp.concatenate([v, jnp.ones_like(v)], axis=1)
                q2 = jnp.concatenate([jnp.where(lo, q, 0.0), jnp.where(lo, 0.0, q)], axis=0).astype(BF16)
                s = lax.dot_general(q2, k, nt_contract, preferred_element_type=F32)
                tiles.append((s, bias, v, row0))
            probs = []
            for s, bias, v, row0 in tiles:
                s = s + jnp.concatenate([bias, bias], axis=0)
                m = jnp.max(s, axis=-1, keepdims=True)
                pr = jnp.exp2(s - m)
                probs.append((pr.astype(BF16), m))
            for (s, bias, v, row0), (pr, m) in zip(tiles, probs):
                pv = jnp.dot(pr, v, preferred_element_type=F32)
                rows = pl.ds(row0, QB, stride=d)
                acc_ref[p, rows, :] = jnp.where(lo, pv[0:QB, 0:LANES], pv[QB:2 * QB, 0:LANES])
                m_ref[p, rows, :] = jnp.where(lo, m[0:QB], m[QB:2 * QB])
                l_ref[p, rows, :] = jnp.where(lo, pv[0:QB, LANES:2 * LANES], pv[QB:2 * QB, LANES:2 * LANES])
            side(gi, n_groups)

    for p, (_, d) in enumerate(DILATIONS):
        pl.when(phase == p)(functools.partial(pattern, p, d))

    @pl.when(phase == ATTN_PHASES - 1)
    def _():
        n_merge = SUPER // QB
        for ci in range(n_merge):
            rows = pl.ds(ci * QB, QB)
            m1, m2, m3 = m_ref[0, rows, :], m_ref[1, rows, :], m_ref[2, rows, :]
            mm = jnp.maximum(jnp.maximum(m1, m2), m3)
            w1, w2, w3 = jnp.exp2(m1 - mm), jnp.exp2(m2 - mm), jnp.exp2(m3 - mm)
            num = w1 * acc_ref[0, rows, :] + w2 * acc_ref[1, rows, :] + w3 * acc_ref[2, rows, :]
            den = w1 * l_ref[0, rows, :] + w2 * l_ref[1, rows, :] + w3 * l_ref[2, rows, :]
            o_ref[rows, :] = num / den
            if ci % (n_merge // n_groups) == 0:
                side(ci // (n_merge // n_groups), n_groups)


def _attn_bias():
    qi = np.arange(QB)[:, None]
    ki = np.arange(2 * QB)[None, :]
    dist = qi + QB - ki
    normal = (dist >= 0) & (dist <= N_BACK)
    first = normal & (ki >= QB)
    return jnp.asarray(np.where(np.stack([first, normal]), 0.0, NEG), F32)


SA_HEADS = 4
SA_WIDTH = SA_HEADS * HEAD_DIM
N_SA_IN = 7


RING = 3


def _stream_kernel(*refs, rider, rider_takes_side, batch_lo, n_rider_in, n_rider_out, n_prev, n_rider_scratch):
    kt_hbm, vt_hbm, qx_ref, kn_ref, vn_ref, w_ref, hm_ref = refs[:N_SA_IN]
    rider_in = refs[N_SA_IN:N_SA_IN + n_rider_in]
    n_in = N_SA_IN + n_rider_in + n_prev
    okt_ref, ovt_ref, o_ref = refs[n_in:n_in + 3]
    rider_out = refs[n_in + 3:n_in + 3 + n_rider_out]
    pad_ref, newt_ref, kring, vring, sems = refs[n_in + 3 + n_rider_out:n_in + 8 + n_rider_out]
    rider_scratch = refs[len(refs) - n_rider_scratch:] if n_rider_scratch else ()
    n_buf = kring.shape[2]
    t_new = kn_ref.shape[0]
    n_tiles = n_buf // LANES

    step = pl.program_id(0)
    n_steps = pl.num_programs(0)

    def block_copy(i, s):
        hbm, ring = ((kt_hbm, kring), (vt_hbm, vring))[i]
        src = hbm.at[batch_lo + s // SA_NBLK, pl.ds((s % SA_NBLK) * SA_WIDTH, SA_WIDTH), :]
        return pltpu.make_async_copy(src, ring.at[s % RING], sems.at[i, s % RING])

    @pl.when(step == 0)
    def _():
        for s in range(RING - 1):
            for i in range(2):
                block_copy(i, s).start()

    @pl.when(step + RING - 1 < n_steps)
    def _():
        for i in range(2):
            block_copy(i, step + RING - 1).start()

    for i in range(2):
        block_copy(i, step).wait()
    kt_ref = kring.at[step % RING]
    vt_ref = vring.at[step % RING]

    def sample_attention():
        for i, new_ref in enumerate((kn_ref, vn_ref)):
            pad_ref[...] = jnp.zeros_like(pad_ref)
            pad_ref[0:t_new, :] = new_ref[...]
            newt_ref[i] = pad_ref[...].T
        k_ext = jnp.concatenate([kt_ref[...], newt_ref[0]], axis=1)
        v_ext = jnp.concatenate([vt_ref[...], newt_ref[1]], axis=1)
        w = w_ref[...]
        s = jnp.dot(qx_ref[...], k_ext.astype(BF16), preferred_element_type=F32)
        s = jnp.where(w > 0.0, s, NEG)
        m = jnp.max(s, axis=-1, keepdims=True)
        e = w * jnp.exp2(s - m)
        l = jnp.sum(e, axis=-1, keepdims=True)
        pv = lax.dot_general(e.astype(BF16), v_ext.astype(BF16), (((1,), (1,)), ((), ())),
                             preferred_element_type=F32)
        pv = pv * hm_ref[...] / l
        o = pv[0:t_new]
        for h in range(1, SA_HEADS):
            o = o + pv[h * t_new:(h + 1) * t_new]
        o_ref[...] = o

    boundary = {}

    def shift(part, n_parts):
        per = n_tiles // n_parts
        first = part * per
        keep = lax.broadcasted_iota(jnp.int32, (1, LANES), 1) < LANES - t_new
        if part == 0:
            boundary.clear()
        for i, (src_ref, dst_ref) in enumerate(((kt_ref, okt_ref), (vt_ref, ovt_ref))):
            def rotated(j):
                if (i, j) in boundary:
                    return boundary.pop((i, j))
                t = src_ref[:, j * LANES:(j + 1) * LANES] if j < n_tiles else newt_ref[i]
                return pltpu.roll(t, LANES - t_new, axis=1)
            rolled = [rotated(j) for j in range(first, first + per + 1)]
            for j in range(per):
                dst_ref[:, (first + j) * LANES:(first + j + 1) * LANES] = jnp.where(keep, rolled[j], rolled[j + 1])
            if part + 1 < n_parts:
                boundary[(i, first + per)] = rolled[per]

    sample_attention()
    if rider_takes_side:
        rider(*rider_in, *rider_out, *rider_scratch, side=shift)
    else:
        shift(0, 1)
        rider(*rider_in, *rider_out, *rider_scratch)


def _sa_weights(t_new, n_buf):
    ext = n_buf + LANES
    pos = np.arange(ext)[None, :]
    i = np.arange(t_new)[:, None]
    delta = n_buf + i - pos
    is_real = (pos < n_buf + t_new)
    w = np.zeros((t_new, ext), np.float32)
    for window, d in DILATIONS:
        w += ((delta >= 0) & (delta % d == 0) & (delta <= window) & is_real)
    return jnp.asarray(np.tile(w, (SA_HEADS, 1)), F32)


def _head_mask(t_new):
    h = np.repeat(np.arange(SA_HEADS), t_new)[:, None]
    c = np.arange(SA_WIDTH)[None, :] // HEAD_DIM
    return (h == c).astype(np.float32)


SA_NBLK = ATT_WIDTH // SA_WIDTH


def _stream(kt, vt, prev, batch_lo, n_batches, qx, k_new, v_new, rider, rider_args, rider_in_specs,
            rider_out_specs, rider_out_shapes, name, rider_scratch=(), rider_takes_side=False):
    bd, _, n_buf = kt.shape
    t_new = k_new.shape[1]
    assert n_buf == MAX_SPAN, "window buffer must cover the longest pattern (no invalid positions)"
    rows = t_new * SA_HEADS
    buf = pl.BlockSpec((None, SA_WIDTH, n_buf), lambda i: (batch_lo + i // SA_NBLK, i % SA_NBLK, 0))
    new = pl.BlockSpec((None, t_new, SA_WIDTH), lambda i: (batch_lo + i // SA_NBLK, 0, i % SA_NBLK))
    o_blk = pl.BlockSpec((None, t_new, SA_WIDTH), lambda i: (i // SA_NBLK, 0, i % SA_NBLK))
    qx_blk = pl.BlockSpec((None, None, rows, SA_WIDTH), lambda i: (batch_lo + i // SA_NBLK, i % SA_NBLK, 0, 0))
    sd_buf = jax.ShapeDtypeStruct((bd, ATT_WIDTH, n_buf), F32)
    hbm = pl.BlockSpec(memory_space=pl.ANY)
    in_specs = [hbm, hbm, qx_blk, new, new, _const_spec((rows, n_buf + LANES)), _const_spec((rows, SA_WIDTH))]
    in_specs += list(rider_in_specs)
    args = [kt, vt, qx, k_new, v_new, _sa_weights(t_new, n_buf), jnp.asarray(_head_mask(t_new), F32)]
    args += list(rider_args)
    aliases = {}
    if prev is not None:
        aliases = {len(args): 0, len(args) + 1: 1}
        in_specs += [pl.BlockSpec(memory_space=pl.ANY)] * 2
        args += list(prev)
    return pl.pallas_call(
        functools.partial(_stream_kernel, rider=rider, rider_takes_side=rider_takes_side, batch_lo=batch_lo,
                          n_rider_in=len(rider_in_specs),
                          n_rider_out=len(rider_out_specs), n_prev=len(aliases),
                          n_rider_scratch=len(rider_scratch)),
        grid=(n_batches * SA_NBLK,),
        in_specs=in_specs,
        out_specs=[buf, buf, o_blk] + list(rider_out_specs),
        out_shape=[sd_buf, sd_buf, jax.ShapeDtypeStruct((n_batches, t_new, ATT_WIDTH), F32)] + list(rider_out_shapes),
        input_output_aliases=aliases,
        scratch_shapes=[pltpu.VMEM((LANES, SA_WIDTH), F32), pltpu.VMEM((2, SA_WIDTH, LANES), F32),
                        pltpu.VMEM((RING, SA_WIDTH, n_buf), F32), pltpu.VMEM((RING, SA_WIDTH, n_buf), F32),
                        pltpu.SemaphoreType.DMA((2, RING))] + list(rider_scratch),
        compiler_params=_cparams(("arbitrary",)),
        name=name,
    )(*args)


def _expand_queries(q):
    bd, t_new, _ = q.shape
    hm = jnp.asarray(_head_mask(t_new), F32)
    qb = q.reshape(bd, t_new, SA_NBLK, SA_WIDTH).transpose(0, 2, 1, 3)
    return (jnp.tile(qb, (1, 1, SA_HEADS, 1)) * hm).astype(BF16)


def _rider_tiling(b, s, steps):
    assert (b * s) % steps == 0
    tm = b * s // steps
    assert tm % LANES == 0 and s % tm == 0
    return tm, s // tm


def _inproj_prompt(x, scale, shift, g, w_bf, qg, kg):
    b, s, _ = x.shape
    tm = TOKEN_TILE
    per_seq = s // tm
    keep = min(MAX_SPAN, s)
    kt_first = per_seq - keep // tm
    tok = lambda w: pl.BlockSpec((tm, w), lambda i: (i, 0))
    mod = pl.BlockSpec((None, 1, D_MODEL), lambda i: (i // per_seq, 0, 0))
    out_t = pl.BlockSpec((None, ATT_WIDTH, tm), lambda i: (i // per_seq, 0, jnp.maximum(i % per_seq - kt_first, 0)))
    sd = jax.ShapeDtypeStruct((b * s, ATT_WIDTH), F32)
    sdt = jax.ShapeDtypeStruct((b, ATT_WIDTH, keep), F32)
    q, k, v, u, kt_p, vt_p = pl.pallas_call(
        functools.partial(_inproj_body, kt_rule=(per_seq, kt_first)),
        grid=(b * per_seq,),
        in_specs=[tok(D_MODEL), mod, mod, _const_spec((1, D_MODEL)), _const_spec((D_MODEL, 4 * ATT_WIDTH)),
                  _const_spec((1, ATT_WIDTH)), _const_spec((1, ATT_WIDTH))],
        out_specs=[tok(ATT_WIDTH)] * 4 + [out_t] * 2,
        out_shape=[sd] * 4 + [sdt] * 2,
        compiler_params=_cparams(("arbitrary",)),
        name="inproj_prompt",
    )(x.reshape(b * s, D_MODEL), scale, shift, g, w_bf, qg, kg)
    r3 = lambda a: a.reshape(b, s, ATT_WIDTH)
    return r3(q), r3(k), r3(v), r3(u), kt_p, vt_p


def _attn_stream_batches(b, s):
    n_super = s // SUPER
    units = b * (ATT_WIDTH // LANES) * n_super
    steps = units * ATTN_PHASES
    assert steps % SA_NBLK == 0
    return steps // SA_NBLK


def _stream_attn(kt, vt, batch_lo, n_batches, qx, k_new, v_new, q, k, v):
    b, s, _ = q.shape
    assert s % SUPER == 0
    n_super = s // SUPER
    n_hp = ATT_WIDTH // LANES
    assert n_batches == _attn_stream_batches(b, s)

    n_units = b * n_hp * n_super

    def unit_block(lead):
        def index(i):
            un = jnp.minimum((i + lead) // ATTN_PHASES, n_units - 1)
            return un // (n_hp * n_super), un % n_super, (un // n_super) % n_hp
        return pl.BlockSpec((None, SUPER, LANES), index)

    outs = _stream(
        kt, vt, None, batch_lo, n_batches, qx, k_new, v_new,
        functools.partial(_attn_rider, n_super=n_super),
        (q, k, v, _attn_bias()),
        [unit_block(1), unit_block(ATTN_PHASES - 1), unit_block(ATTN_PHASES - 1), _const_spec((2, QB, 2 * QB))],
        [unit_block(0)], [jax.ShapeDtypeStruct((b, s, ATT_WIDTH), F32)], "stream_attn",
        rider_scratch=[pltpu.VMEM((2 * SUPER, LANES), F32)] * 2 + [pltpu.VMEM((len(DILATIONS), SUPER, LANES), F32)] * 3,
        rider_takes_side=True)
    okt, ovt, o_s, o_att = outs
    return (okt, ovt), o_s, o_att


def _stream_post(kt, vt, prev, batch_lo, n_batches, qx, k_new, v_new, x, o_att, y_ssm, mods, consts, weights):
    b, s, _ = x.shape
    tm, per_seq = _rider_tiling(b, s, n_batches * SA_NBLK)
    tok = lambda w: pl.BlockSpec((tm, w), lambda i: (i, 0))
    mod = pl.BlockSpec((None, 1, D_MODEL), lambda i: (i // per_seq, 0, 0))
    okt, ovt, o_s, y = _stream(
        kt, vt, prev, batch_lo, n_batches, qx, k_new, v_new, _post_body,
        (x.reshape(b * s, D_MODEL), o_att.reshape(b * s, ATT_WIDTH), y_ssm.reshape(b * s, SSM_WIDTH),
         *mods, *consts, *weights),
        [tok(D_MODEL), tok(ATT_WIDTH), tok(SSM_WIDTH), mod, mod, mod, mod]
        + [_const_spec(c.shape) for c in consts] + [_const_spec(w.shape) for w in weights],
        [tok(D_MODEL)], [jax.ShapeDtypeStruct((b * s, D_MODEL), F32)], "stream_post")
    return okt, ovt, o_s, y.reshape(b, s, D_MODEL)


def _s5_tables(a_re, a_im, log_dt, b_re, b_im):
    g, n = a_re.shape
    dt = jnp.exp(log_dt)[:, None]
    x, y = dt * a_re, dt * a_im
    ks = jnp.arange(CHUNK + 1, dtype=F32)
    mag = jnp.exp(x[:, :, None] * ks)
    pw_re, pw_im = mag * jnp.cos(y[:, :, None] * ks), mag * jnp.sin(y[:, :, None] * ks)
    e_re = jnp.expm1(x) * jnp.cos(y) - 2.0 * jnp.sin(0.5 * y) ** 2
    e_im = jnp.exp(x) * jnp.sin(y)
    den = a_re * a_re + a_im * a_im
    f_re = (e_re * a_re + e_im * a_im) / den
    f_im = (e_im * a_re - e_re * a_im) / den
    bb_re = f_re[..., None] * b_re - f_im[..., None] * b_im
    bb_im = f_re[..., None] * b_im + f_im[..., None] * b_re
    return dict(pw_re=pw_re, pw_im=pw_im, bb_re=bb_re, bb_im=bb_im)


def _s5_chunk_operators(tb, c_re, c_im, d_skip):
    pw_re, pw_im, bb_re, bb_im = tb["pw_re"], tb["pw_im"], tb["bb_re"], tb["bb_im"]
    g, n, ch = bb_re.shape
    L = CHUNK
    hp = lax.Precision.HIGHEST
    ct_re, ct_im = c_re.transpose(0, 2, 1), c_im.transpose(0, 2, 1)
    pr, pi = pw_re[:, :, :, None], pw_im[:, :, :, None]
    ca_re = ct_re[:, :, None, :] * pr - ct_im[:, :, None, :] * pi
    ca_im = ct_re[:, :, None, :] * pi + ct_im[:, :, None, :] * pr
    a_cat = jnp.concatenate([ca_re[:, :, :L].reshape(g, n, L * ch), -ca_im[:, :, :L].reshape(g, n, L * ch)], axis=1)
    b_cat = jnp.concatenate([bb_re, bb_im], axis=1)
    base = jnp.einsum("gnd,gnx->gdx", b_cat, a_cat, precision=hp)
    skip = jnp.eye(ch, dtype=F32)[None] * d_skip.reshape(g, ch, 1)
    base = base + jnp.pad(skip, ((0, 0), (0, 0), (0, (L - 1) * ch)))
    toe = jnp.stack([jnp.pad(base, ((0, 0), (0, 0), (s * ch, 0)))[:, :, :L * ch] for s in range(L)], axis=1)
    toe = toe.reshape(g, L * ch, L * ch)
    rv_re = pw_re[:, :, L - 1::-1].transpose(0, 2, 1)[:, :, None, :]
    rv_im = pw_im[:, :, L - 1::-1].transpose(0, 2, 1)[:, :, None, :]
    bt_re, bt_im = bb_re.transpose(0, 2, 1)[:, None], bb_im.transpose(0, 2, 1)[:, None]
    ws_re = (rv_re * bt_re - rv_im * bt_im).reshape(g, L * ch, n)
    ws_im = (rv_re * bt_im + rv_im * bt_re).reshape(g, L * ch, n)
    vr = ca_re[:, :, 1:].reshape(g, n, L * ch)
    vi = (-ca_im[:, :, 1:]).reshape(g, n, L * ch)
    gp = g // 2
    z_w = jnp.zeros((gp, L * ch, n), F32)
    wr, wi = ws_re.reshape(gp, 2, L * ch, n), ws_im.reshape(gp, 2, L * ch, n)
    w_pair = jnp.concatenate([
        jnp.concatenate([wr[:, 0], z_w, wi[:, 0], z_w], axis=2),
        jnp.concatenate([z_w, wr[:, 1], z_w, wi[:, 1]], axis=2)], axis=1)
    z_v = jnp.zeros((gp, n, L * ch), F32)
    vr, vi = vr.reshape(gp, 2, n, L * ch), vi.reshape(gp, 2, n, L * ch)
    v_pair = jnp.concatenate([
        jnp.concatenate([vr[:, 0], z_v], axis=2), jnp.concatenate([z_v, vr[:, 1]], axis=2),
        jnp.concatenate([vi[:, 0], z_v], axis=2), jnp.concatenate([z_v, vi[:, 1]], axis=2)], axis=1)
    a16_re = pw_re[:, :, L].reshape(1, g * n)
    a16_im = pw_im[:, :, L].reshape(1, g * n)
    return toe.astype(BF16), w_pair.astype(BF16), v_pair.astype(BF16), a16_re, a16_im


GROUPS_PER_BLOCK = LANES // SSM_CH
REGROUP_ROWS = 32


def _granule_transpose(arrs):
    n = GROUPS_PER_BLOCK
    gran = lax.broadcasted_iota(jnp.int32, (1, LANES), 1) // SSM_CH
    cur = list(arrs)
    k = n // 2
    while k >= 1:
        keep_low = (gran % (2 * k)) < k
        nxt = list(cur)
        for t in range(n):
            if t % (2 * k) < k:
                a, b = cur[t], cur[t + k]
                nxt[t] = jnp.where(keep_low, a, pltpu.roll(b, k * SSM_CH, axis=1))
                nxt[t + k] = jnp.where(keep_low, pltpu.roll(a, LANES - k * SSM_CH, axis=1), b)
        cur = nxt
        k //= 2
    return cur


def _ssm_a_kernel(u_ref, w_ref, xg_ref, sre_ref, sim_ref, xs_ref):
    nb, s_len, _ = u_ref.shape
    n_chunks = s_len // CHUNK
    half = CHUNK // 2

    for b in range(nb):
        def regroup(ct, c, b=b):
            c0 = pl.multiple_of(ct * REGROUP_ROWS, REGROUP_ROWS)
            rows = pl.ds(b * n_chunks + c0, REGROUP_ROWS)
            for j in range(2):
                z = [u_ref[b, pl.ds(c0 * CHUNK + half * j + tt, REGROUP_ROWS, stride=CHUNK), :]
                     for tt in range(half)]
                for p, xp in enumerate(_granule_transpose(z)):
                    xs_ref[2 * p + j, rows, :] = xp
            return c
        lax.fori_loop(0, n_chunks // REGROUP_ROWS, regroup, 0)

    for col in range(2 * GROUPS_PER_BLOCK):
        xg_ref[:, col * LANES:(col + 1) * LANES] = xs_ref[col].astype(BF16)
    for pp in range(GROUPS_PER_BLOCK // 2):
        xp = xg_ref[:, pp * 4 * LANES:(pp + 1) * 4 * LANES]
        st = jnp.dot(xp, w_ref[pp], preferred_element_type=F32)
        sre_ref[pp] = st[:, 0:LANES]
        sim_ref[pp] = st[:, LANES:2 * LANES]


def _ssm_scan_kernel(sre_ref, sim_ref, are_ref, aim_ref, hre_ref, him_ref, fre_ref, fim_ref, *, nb):
    ncb, rows, _ = sre_ref.shape
    n_chunks = rows // nb
    a_re = [are_ref[cb] for cb in range(ncb)]
    a_im = [aim_ref[cb] for cb in range(ncb)]

    def body(c, carry):
        r = pl.ds(c, nb, stride=n_chunks)
        out = []
        for cb in range(ncb):
            h_re, h_im = carry[2 * cb], carry[2 * cb + 1]
            hre_ref[cb, r, :] = h_re
            him_ref[cb, r, :] = h_im
            out.append(a_re[cb] * h_re - a_im[cb] * h_im + sre_ref[cb, r, :])
            out.append(a_re[cb] * h_im + a_im[cb] * h_re + sim_ref[cb, r, :])
        return tuple(out)

    z = jnp.zeros((nb, LANES), F32)
    fin = lax.fori_loop(0, n_chunks, body, (z,) * (2 * ncb), unroll=4)
    for cb in range(ncb):
        fre_ref[cb] = fin[2 * cb]
        fim_ref[cb] = fin[2 * cb + 1]


def _ssm_c_kernel(xg_ref, hre_ref, him_ref, t_ref, v_ref, y_ref, ys_ref):
    nb, s_len, _ = y_ref.shape
    n_chunks = s_len // CHUNK
    half = CHUNK // 2
    for pp in range(GROUPS_PER_BLOCK // 2):
        hcat = jnp.concatenate([hre_ref[pp], him_ref[pp]], axis=1).astype(BF16)
        inter = jnp.dot(hcat, v_ref[pp], preferred_element_type=F32)
        for e in range(2):
            g = 2 * pp + e
            intra = jnp.dot(xg_ref[:, g * 2 * LANES:(g + 1) * 2 * LANES], t_ref[g],
                            preferred_element_type=F32)
            yg = intra + inter[:, e * 2 * LANES:(e + 1) * 2 * LANES]
            ys_ref[2 * g] = yg[:, 0:LANES]
            ys_ref[2 * g + 1] = yg[:, LANES:2 * LANES]

    for b in range(nb):
        def regroup(ct, c, b=b):
            c0 = pl.multiple_of(ct * REGROUP_ROWS, REGROUP_ROWS)
            rows = pl.ds(b * n_chunks + c0, REGROUP_ROWS)
            for j in range(2):
                yp =[ys_ref[2 * p + j, rows, :] for p in range(GROUPS_PER_BLOCK)]
                for tt, zt in enumerate(_granule_transpose(yp)):
                    y_ref[b, pl.ds(c0 * CHUNK + half * j + tt, REGROUP_ROWS, stride=CHUNK), :] = zt
            return c
        lax.fori_loop(0, n_chunks // REGROUP_ROWS, regroup, 0)


def _ssm_prompt(u, toe, w_pair, v_pair, a16_re, a16_im):
    b, s, _ = u.shape
    rows = b * (s // CHUNK)
    nblk = SSM_WIDTH // LANES
    gw = GROUPS_PER_BLOCK * CHUNK * SSM_CH
    ncb = GROUPS_PER_BLOCK * SSM_STATE // LANES
    ublk = pl.BlockSpec((b, s, LANES), lambda j: (0, 0, j))
    sblk = pl.BlockSpec((ncb, rows, LANES), lambda j: (j, 0, 0))
    sd_h = jax.ShapeDtypeStruct((nblk * ncb, rows, LANES), F32)
    xg, s_re, s_im = pl.pallas_call(
        _ssm_a_kernel,
        grid=(nblk,),
        in_specs=[ublk, pl.BlockSpec((GROUPS_PER_BLOCK // 2, 4 * LANES, 2 * LANES), lambda j: (j, 0, 0))],
        out_specs=[pl.BlockSpec((rows, gw), lambda j: (0, j)), sblk, sblk],
        out_shape=[jax.ShapeDtypeStruct((rows, nblk * gw), BF16), sd_h, sd_h],
        scratch_shapes=[pltpu.VMEM((gw // LANES, rows, LANES), F32)],
        compiler_params=_cparams(("arbitrary",)),
        name="ssm_chunk_states",
    )(u, w_pair)

    ablk = pl.BlockSpec((ncb, 1, LANES), lambda j: (j, 0, 0))
    fblk = pl.BlockSpec((ncb, b, LANES), lambda j: (j, 0, 0))
    sd_f = jax.ShapeDtypeStruct((nblk * ncb, b, LANES), F32)
    a16_re = a16_re.reshape(nblk * ncb, 1, LANES)
    a16_im = a16_im.reshape(nblk * ncb, 1, LANES)
    h_re, h_im, f_re, f_im = pl.pallas_call(
        functools.partial(_ssm_scan_kernel, nb=b),
        grid=(nblk,),
        in_specs=[sblk, sblk, ablk, ablk],
        out_specs=[sblk, sblk, fblk, fblk],
        out_shape=[sd_h, sd_h, sd_f, sd_f],
        compiler_params=_cparams(("arbitrary",)),
        name="ssm_scan",
    )(s_re, s_im, a16_re, a16_im)

    y = pl.pallas_call(
        _ssm_c_kernel,
        grid=(nblk,),
        in_specs=[pl.BlockSpec((rows, gw), lambda j: (0, j)), sblk, sblk,
                  pl.BlockSpec((GROUPS_PER_BLOCK, 2 * LANES, 2 * LANES), lambda j: (j, 0, 0)),
                  pl.BlockSpec((GROUPS_PER_BLOCK // 2, 2 * LANES, 4 * LANES), lambda j: (j, 0, 0))],
        out_specs=ublk,
        out_shape=jax.ShapeDtypeStruct((b, s, SSM_WIDTH), F32),
        scratch_shapes=[pltpu.VMEM((gw // LANES, rows, LANES), F32)],
        compiler_params=_cparams(("arbitrary",)),
        name="ssm_outputs",
    )(xg, h_re, h_im, toe, v_pair)
    to_rows = lambda f: jnp.transpose(f, (1, 0, 2)).reshape(b, nblk * ncb * LANES)
    return y, to_rows(f_re), to_rows(f_im)


def _ssm_sample_kernel(u_ref, hre_ref, him_ref, are_ref, aim_ref, bre_ref, bim_ref, cre_ref, cim_ref, d_ref,
                       y_ref, ore_ref, oim_ref, *, t_new):
    h_re = hre_ref[...].T
    h_im = him_ref[...].T
    a_re, a_im = are_ref[...], aim_ref[...]
    bd = h_re.shape[0]
    for t in range(t_new):
        u = u_ref[t]
        ub = u.astype(BF16)
        n_re = a_re * h_re - a_im * h_im + jnp.dot(ub, bre_ref[...], preferred_element_type=F32)
        n_im = a_re * h_im + a_im * h_re + jnp.dot(ub, bim_ref[...], preferred_element_type=F32)
        h_re, h_im = n_re, n_im
        y = (jnp.dot(h_re.astype(BF16), cre_ref[...], preferred_element_type=F32)
             + jnp.dot(h_im.astype(BF16), cim_ref[...], preferred_element_type=F32) + d_ref[...] * u)
        y_ref[t] = y
    ore_ref[...] = h_re.T
    oim_ref[...] = h_im.T


def _block_diag(m):
    g, r, c = m.shape
    spread = jnp.asarray(np.tile(np.eye(c, dtype=np.float32), (1, g)), BF16)
    tiled = jnp.dot(m.reshape(g * r, c).astype(BF16), spread, preferred_element_type=F32)
    keep = np.arange(g * r)[:, None] // r == np.arange(g * c)[None, :] // c
    return jnp.where(jnp.asarray(keep), tiled, 0.0).astype(BF16)


def _ssm_sample(u, h0_re_t, h0_im_t, tb, c_re, c_im, d_skip, t_new):
    gn = SSM_GROUPS * SSM_STATE
    a_re = tb["pw_re"][:, :, 1].reshape(1, gn)
    a_im = tb["pw_im"][:, :, 1].reshape(1, gn)
    b_re = _block_diag(tb["bb_re"].transpose(0, 2, 1))
    b_im = _block_diag(tb["bb_im"].transpose(0, 2, 1))
    cb_re = _block_diag(c_re.transpose(0, 2, 1))
    cb_im = _block_diag(-c_im.transpose(0, 2, 1))
    bd = u.shape[1]
    full = lambda shape: pl.BlockSpec(shape, lambda i: (0,) * len(shape))
    return pl.pallas_call(
        functools.partial(_ssm_sample_kernel, t_new=t_new),
        grid=(1,),
        in_specs=[full((t_new, bd, SSM_WIDTH)), full((gn, bd)), full((gn, bd)), full((1, gn)), full((1, gn)),
                  full((SSM_WIDTH, gn)), full((SSM_WIDTH, gn)), full((gn, SSM_WIDTH)), full((gn, SSM_WIDTH)),
                  full((1, SSM_WIDTH))],
        out_specs=[full((t_new, bd, SSM_WIDTH)), full((gn, bd)), full((gn, bd))],
        out_shape=[jax.ShapeDtypeStruct((t_new, bd, SSM_WIDTH), F32),
                   jax.ShapeDtypeStruct((gn, bd), F32), jax.ShapeDtypeStruct((gn, bd), F32)],
        compiler_params=_cparams(("arbitrary",)),
        name="ssm_sample",
    )(u, h0_re_t, h0_im_t, a_re, a_im, b_re, b_im, cb_re, cb_im, d_skip.reshape(1, SSM_WIDTH))


def _rms(x, gain):
    return x * lax.rsqrt(jnp.mean(x * x, axis=-1, keepdims=True) + EPS) * gain


def _gelu_tanh(x):
    return 0.5 * x * (1.0 + jnp.tanh(math.sqrt(2.0 / math.pi) * (x + 0.044715 * (x * x * x))))


def _post_body(x_ref, oa_ref, ys_ref, g1_ref, sc2_ref, sh2_ref, g2_ref, n2_ref, ag_ref, sg_ref,
               wglu_ref, wout_ref, wg_ref, wu_ref, wd_ref, o_ref):
    ya = _gelu_tanh(ys_ref[...])
    ya = ya * _sigmoid(jnp.dot(ya.astype(BF16), wglu_ref[...], preferred_element_type=F32))
    merged = jnp.concatenate([_rms(oa_ref[...], ag_ref[...]), _rms(ya, sg_ref[...])], axis=1)
    x1 = x_ref[...] + g1_ref[...] * jnp.dot(merged.astype(BF16), wout_ref[...], preferred_element_type=F32)
    h2 = (_rms(x1, n2_ref[...]) * (1.0 + sc2_ref[...]) + sh2_ref[...]).astype(BF16)
    gate = jnp.dot(h2, wg_ref[...], preferred_element_type=F32)
    up = jnp.dot(h2, wu_ref[...], preferred_element_type=F32)
    act = (gate * _sigmoid(gate) * up).astype(BF16)
    o_ref[...] = x1 + g2_ref[...] * jnp.dot(act, wd_ref[...], preferred_element_type=F32)


def _post_sample(x, o_att, y_ssm, mods, consts, weights):
    t = x.shape[0]
    tm = min(TOKEN_TILE, t)
    tok = lambda w: pl.BlockSpec((tm, w), lambda i: (i, 0))
    mod = tok(D_MODEL)
    return pl.pallas_call(
        _post_body,
        grid=(t // tm,),
        in_specs=[tok(D_MODEL), tok(ATT_WIDTH), tok(SSM_WIDTH), mod, mod, mod, mod]
                 + [_const_spec(c.shape) for c in consts] + [_const_spec(w.shape) for w in weights],
        out_specs=tok(D_MODEL),
        out_shape=jax.ShapeDtypeStruct((t, D_MODEL), F32),
        compiler_params=_cparams(("arbitrary",)),
        name="post_sample",
    )(x, o_att, y_ssm, *mods, *consts, *weights)


def kernel(x_prompt, x_sample, cache_k, cache_v, state_ssm_re, state_ssm_im, c_prompt, c_sample, norm1_g, norm2_g, w_ada, b_ada, w_in, q_gain, k_gain, ssm_a_re, ssm_a_im, ssm_log_dt, ssm_b_re, ssm_b_im, ssm_c_re, ssm_c_im, ssm_d, w_glu, attn_out_g, ssm_out_g, w_out, w_gate, w_up, w_down):
    depth = norm1_g.shape[0]
    assert depth == 1, "one decoder layer"
    b, s, _ = x_prompt.shape
    bd, t_new, _ = x_sample.shape
    n_buf = cache_k.shape[2]
    L = 0

    n_c = b + bd
    pad = (-n_c) % SUBLANES
    c_all = jnp.concatenate([c_prompt, c_sample, jnp.zeros((pad, D_MODEL), F32)], axis=0)
    mod = _ada(c_all, w_ada[L], b_ada[L].reshape(1, -1))
    mod_p = [mod[0:b, i * D_MODEL:(i + 1) * D_MODEL].reshape(b, 1, D_MODEL) for i in range(N_MOD)]
    mod_s = [jnp.repeat(mod[b:n_c, i * D_MODEL:(i + 1) * D_MODEL], t_new, axis=0) for i in range(N_MOD)]

    w_in_bf = w_in[L].astype(BF16)
    n1 = norm1_g[L].reshape(1, D_MODEL)
    qg = jnp.tile(q_gain[L], ATT_HEADS).reshape(1, ATT_WIDTH)
    kg = jnp.tile(k_gain[L], ATT_HEADS).reshape(1, ATT_WIDTH)
    consts = (norm2_g[L].reshape(1, D_MODEL), attn_out_g[L].reshape(1, ATT_WIDTH), ssm_out_g[L].reshape(1, SSM_WIDTH))
    weights = tuple(w[L].astype(BF16) for w in (w_glu, w_out, w_gate, w_up, w_down))

    tb = _s5_tables(ssm_a_re[L], ssm_a_im[L], ssm_log_dt[L], ssm_b_re[L], ssm_b_im[L])
    toe, w_pair, v_pair, a16_re, a16_im = _s5_chunk_operators(tb, ssm_c_re[L], ssm_c_im[L], ssm_d[L])

    xs = x_sample.reshape(bd * t_new, D_MODEL)
    qs, ks, vs, us = _inproj_sample(xs, mod_s[1], mod_s[0], n1, w_in_bf, qg, kg)
    qx = _expand_queries(qs.reshape(bd, t_new, ATT_WIDTH))
    ks3, vs3 = ks.reshape(bd, t_new, ATT_WIDTH), vs.reshape(bd, t_new, ATT_WIDTH)
    kt = jnp.transpose(cache_k[L], (0, 2, 3, 1)).reshape(bd, ATT_WIDTH, n_buf)
    vt = jnp.transpose(cache_v[L], (0, 2, 3, 1)).reshape(bd, ATT_WIDTH, n_buf)
    q, k, v, u, kt_p, vt_p = _inproj_prompt(x_prompt, mod_p[1], mod_p[0], n1, w_in_bf, qg, kg)
    n_first = _attn_stream_batches(b, s)
    assert 0 < n_first < bd
    shifted, o_s_first, o_att = _stream_attn(kt, vt, 0, n_first, qx, ks3, vs3, q, k, v)
    y_ssm, f_re, f_im = _ssm_prompt(u, toe, w_pair, v_pair, a16_re, a16_im)
    okt, ovt, o_s_second, y_prompt = _stream_post(
        kt, vt, shifted, n_first, bd - n_first, qx, ks3, vs3, x_prompt, o_att, y_ssm,
        (mod_p[2], mod_p[4], mod_p[3], mod_p[5]), consts, weights)
    o_att_s = jnp.concatenate([o_s_first, o_s_second], axis=0)
    gn = SSM_GROUPS * SSM_STATE
    h0_re = jnp.transpose(state_ssm_re[L], (1, 2, 0)).reshape(gn, bd)
    h0_im = jnp.transpose(state_ssm_im[L], (1, 2, 0)).reshape(gn, bd)
    us_t = jnp.transpose(us.reshape(bd, t_new, SSM_WIDTH), (1, 0, 2))
    y_ssm_t, hs_re, hs_im = _ssm_sample(us_t, h0_re, h0_im, tb, ssm_c_re[L], ssm_c_im[L], ssm_d[L], t_new)
    y_ssm_s = jnp.transpose(y_ssm_t, (1, 0, 2)).reshape(bd * t_new, SSM_WIDTH)
    y_sample = _post_sample(xs, o_att_s.reshape(bd * t_new, ATT_WIDTH), y_ssm_s,
                            (mod_s[2], mod_s[4], mod_s[3], mod_s[5]), consts, weights)

    def from_t(a, nb, keep):
        return jnp.transpose(a.reshape(nb, ATT_HEADS, HEAD_DIM, keep), (0, 3, 1, 2))[None]

    def state_from_t(a):
        return jnp.transpose(a.reshape(SSM_GROUPS, SSM_STATE, bd), (2, 0, 1))[None]

    keep = min(MAX_SPAN, s)
    return (y_prompt, y_sample.reshape(bd, t_new, D_MODEL),
            from_t(kt_p, b, keep), from_t(vt_p, b, keep),
            f_re.reshape(b, SSM_GROUPS, SSM_STATE)[None], f_im.reshape(b, SSM_GROUPS, SSM_STATE)[None],
            from_t(okt, bd, n_buf), from_t(ovt, bd, n_buf),
            state_from_t(hs_re), state_from_t(hs_im))
```

```python
import functools
import math

import jax
import jax.numpy as jnp
import numpy as np
from jax import lax
from jax.experimental import pallas as pl
from jax.experimental.pallas import tpu as pltpu

F32 = jnp.float32
BF16 = jnp.bfloat16

D_MODEL = 1024
HEAD_DIM = 64
ATT_WIDTH = 512
ATT_HEADS = 8
SSM_WIDTH = 512
SSM_CH = 16
SSM_GROUPS = 32
SSM_STATE = 64
DILATIONS = ((128, 1), (512, 4), (2048, 16))
N_BACK = 128
MAX_SPAN = 2048
FFN_HIDDEN = 2816
N_MOD = 6
MOD_SHIFT1, MOD_SCALE1, MOD_GATE1, MOD_SHIFT2, MOD_SCALE2, MOD_GATE2 = range(N_MOD)
EPS = 1e-6

LANES = 128
SUBLANES = 8
VMEM_LIMIT = 56 * 1024 * 1024

TOKEN_TILE = 512
SUPER = 2048
QB = 128
TILE_GROUP = 4
Q_SCALE = HEAD_DIM ** -0.5 * math.log2(math.e)
CHUNK = 16
NEG = -1e30


def _cparams(sem=None):
    return pltpu.CompilerParams(dimension_semantics=sem, vmem_limit_bytes=VMEM_LIMIT)


def _const_spec(shape):
    nd = len(shape)
    return pl.BlockSpec(shape, lambda *_: (0,) * nd, pipeline_mode=pl.Buffered(1))


def _sigmoid(x):
    return 1.0 / (1.0 + jnp.exp(-x))


def _split_bf16(a):
    hi = a.astype(BF16)
    lo = (a - hi.astype(F32)).astype(BF16)
    return hi, lo


def _ada_kernel(c_ref, w_ref, b_ref, o_ref):
    c = c_ref[...]
    a = c * _sigmoid(c)
    a_hi, a_lo = _split_bf16(a)
    w_hi, w_lo = _split_bf16(w_ref[...])
    acc = jnp.dot(a_hi, w_hi, preferred_element_type=F32)
    acc += jnp.dot(a_hi, w_lo, preferred_element_type=F32)
    acc += jnp.dot(a_lo, w_hi, preferred_element_type=F32)
    o_ref[...] = acc + b_ref[...]


def _ada(c_all, w_ada, b_ada):
    rows = c_all.shape[0]
    n = w_ada.shape[1]
    tn = 1024
    return pl.pallas_call(
        _ada_kernel,
        grid=(n // tn,),
        in_specs=[pl.BlockSpec((rows, D_MODEL), lambda j: (0, 0)),
                  pl.BlockSpec((D_MODEL, tn), lambda j: (0, j)),
                  pl.BlockSpec((1, tn), lambda j: (0, j))],
        out_specs=pl.BlockSpec((rows, tn), lambda j: (0, j)),
        out_shape=jax.ShapeDtypeStruct((rows, n), F32),
        compiler_params=_cparams(("arbitrary",)),
        name="ada",
    )(c_all, w_ada, b_ada)


def _head_rms(z, gain):
    lane = lax.broadcasted_iota(jnp.int32, (1, LANES), 1)
    lo = lane < HEAD_DIM
    outs = []
    for c in range(z.shape[1] // LANES):
        blk = z[:, c * LANES:(c + 1) * LANES]
        sq = blk * blk
        s_lo = jnp.sum(jnp.where(lo, sq, 0.0), axis=-1, keepdims=True)
        s_hi = jnp.sum(jnp.where(lo, 0.0, sq), axis=-1, keepdims=True)
        inv = jnp.where(lo, lax.rsqrt(s_lo * (1.0 / HEAD_DIM) + EPS), lax.rsqrt(s_hi * (1.0 / HEAD_DIM) + EPS))
        outs.append(blk * inv)
    return jnp.concatenate(outs, axis=1) * gain


def _inproj_body(x_ref, scale_ref, shift_ref, g_ref, w_ref, qg_ref, kg_ref, *out_refs, kt_rule=None):
    q_ref, k_ref, v_ref, u_ref = out_refs[:4]
    x = x_ref[...]
    ms = jnp.mean(x * x, axis=-1, keepdims=True)
    h = x * lax.rsqrt(ms + EPS) * g_ref[...]
    h = h * (1.0 + scale_ref[...]) + shift_ref[...]
    z = jnp.dot(h.astype(BF16), w_ref[...], preferred_element_type=F32)
    q = _head_rms(z[:, 0:ATT_WIDTH], qg_ref[...]) * Q_SCALE
    k = _head_rms(z[:, ATT_WIDTH:2 * ATT_WIDTH], kg_ref[...])
    v = z[:, 2 * ATT_WIDTH:3 * ATT_WIDTH]
    q_ref[...] = q
    k_ref[...] = k
    v_ref[...] = v
    u_ref[...] = z[:, 3 * ATT_WIDTH:]
    if kt_rule is not None:
        per_seq, kt_first = kt_rule
        kt_ref, vt_ref = out_refs[4:]

        @pl.when(pl.program_id(0) % per_seq >= kt_first)
        def _():
            kt_ref[...] = k.T
            vt_ref[...] = v.T


def _sample_mod_spec(bd, col):
    return pl.BlockSpec((bd, D_MODEL), lambda t: (0, col))


def _inproj_sample(x, mod, bd, g, w_bf, qg, kg):
    n = x.shape[0]
    tok = pl.BlockSpec((bd, D_MODEL), lambda t: (t, 0))
    out_tok = pl.BlockSpec((bd, ATT_WIDTH), lambda t: (t, 0))
    sd = jax.ShapeDtypeStruct((n, ATT_WIDTH), F32)
    return pl.pallas_call(
        _inproj_body,
        grid=(n // bd,),
        in_specs=[tok, _sample_mod_spec(bd, MOD_SCALE1), _sample_mod_spec(bd, MOD_SHIFT1), _const_spec((1, D_MODEL)),
                  _const_spec((D_MODEL, 4 * ATT_WIDTH)), _const_spec((1, ATT_WIDTH)), _const_spec((1, ATT_WIDTH))],
        out_specs=[out_tok] * 4,
        out_shape=[sd] * 4,
        compiler_params=_cparams(("arbitrary",)),
        name="inproj_sample",
    )(x, mod, mod, g, w_bf, qg, kg)


ATTN_PHASES = len(DILATIONS) + 1


def _attn_rider(q_ref, kc_ref, vc_ref, bias_ref, o_ref, kbuf, vbuf, acc_ref, m_ref, l_ref, *, n_super, side):
    step = pl.program_id(0)
    phase = step % ATTN_PHASES
    st = (step // ATTN_PHASES) % n_super
    lane = lax.broadcasted_iota(jnp.int32, (1, LANES), 1)
    lo = lane < HEAD_DIM
    nt_contract = (((1,), (1,)), ((), ()))
    n_groups = SUPER // QB // TILE_GROUP

    def pattern(p, d):
        per_res = SUPER // d // QB
        if p == 0:
            for buf, cur_ref in ((kbuf, kc_ref), (vbuf, vc_ref)):
                @pl.when(st == 0)
                def _(buf=buf):
                    buf[0:SUPER, :] = jnp.zeros((SUPER, LANES), F32)

                @pl.when(st > 0)
                def _(buf=buf):
                    buf[0:SUPER, :] = buf[SUPER:2 * SUPER, :]

                buf[SUPER:2 * SUPER, :] = cur_ref[...]

        for gi in range(n_groups):
            tiles = []
            for u in range(TILE_GROUP):
                ti = gi * TILE_GROUP + u
                r = ti // per_res
                jt = ti % per_res
                row0 = r + d * QB * jt
                ks = SUPER + row0 - d * QB
                bias = bias_ref[jnp.where(st == 0, 0, 1)] if jt == 0 else bias_ref[1]
                q = q_ref[pl.ds(row0, QB, stride=d), :]
                k = kbuf[pl.ds(ks, 2 * QB, stride=d), :].astype(BF16)
                v = vbuf[pl.ds(ks, 2 * QB, stride=d), :].astype(BF16)
                v = jnp.concatenate([v, jnp.ones_like(v)], axis=1)
                q2 = jnp.concatenate([jnp.where(lo, q, 0.0), jnp.where(lo, 0.0, q)], axis=0).astype(BF16)
                s = lax.dot_general(q2, k, nt_contract, preferred_element_type=F32)
                tiles.append((s, bias, v, row0))
            probs = []
            for s, bias, v, row0 in tiles:
                s = s + jnp.concatenate([bias, bias], axis=0)
                m = jnp.max(s, axis=-1, keepdims=True)
                pr = jnp.exp2(s - m)
                probs.append((pr.astype(BF16), m))
            for (s, bias, v, row0), (pr, m) in zip(tiles, probs):
                pv = jnp.dot(pr, v, preferred_element_type=F32)
                rows = pl.ds(row0, QB, stride=d)
                acc_ref[p, rows, :] = jnp.where(lo, pv[0:QB, 0:LANES], pv[QB:2 * QB, 0:LANES])
                m_ref[p, rows, :] = jnp.where(lo, m[0:QB], m[QB:2 * QB])
                l_ref[p, rows, :] = jnp.where(lo, pv[0:QB, LANES:2 * LANES], pv[QB:2 * QB, LANES:2 * LANES])
            side(gi, n_groups)

    for p, (_, d) in enumerate(DILATIONS):
        pl.when(phase == p)(functools.partial(pattern, p, d))

    @pl.when(phase == ATTN_PHASES - 1)
    def _():
        n_merge = SUPER // QB
        for ci in range(n_merge):
            rows = pl.ds(ci * QB, QB)
            m1, m2, m3 = m_ref[0, rows, :], m_ref[1, rows, :], m_ref[2, rows, :]
            mm = jnp.maximum(jnp.maximum(m1, m2), m3)
            w1, w2, w3 = jnp.exp2(m1 - mm), jnp.exp2(m2 - mm), jnp.exp2(m3 - mm)
            num = w1 * acc_ref[0, rows, :] + w2 * acc_ref[1, rows, :] + w3 * acc_ref[2, rows, :]
            den = w1 * l_ref[0, rows, :] + w2 * l_ref[1, rows, :] + w3 * l_ref[2, rows, :]
            o_ref[rows, :] = num / den
            if ci % (n_merge // n_groups) == 0:
                side(ci // (n_merge // n_groups), n_groups)


def _attn_bias():
    qi = np.arange(QB)[:, None]
    ki = np.arange(2 * QB)[None, :]
    dist = qi + QB - ki
    normal = (dist >= 0) & (dist <= N_BACK)
    first = normal & (ki >= QB)
    return jnp.asarray(np.where(np.stack([first, normal]), 0.0, NEG), F32)


SA_HEADS = 4
SA_WIDTH = SA_HEADS * HEAD_DIM
N_SA_IN = 7


RING = 3


def _stream_kernel(*refs, rider, rider_takes_side, batch_lo, n_rider_in, n_rider_out, n_prev, n_rider_scratch):
    kt_hbm, vt_hbm, qx_ref, kn_ref, vn_ref, w_ref, hm_ref = refs[:N_SA_IN]
    rider_in = refs[N_SA_IN:N_SA_IN + n_rider_in]
    n_in = N_SA_IN + n_rider_in + n_prev
    okt_ref, ovt_ref, o_ref = refs[n_in:n_in + 3]
    rider_out = refs[n_in + 3:n_in + 3 + n_rider_out]
    pad_ref, newt_ref, kring, vring, sems = refs[n_in + 3 + n_rider_out:n_in + 8 + n_rider_out]
    rider_scratch = refs[len(refs) - n_rider_scratch:] if n_rider_scratch else ()
    n_buf = kring.shape[2]
    t_new = kn_ref.shape[0]
    n_tiles = n_buf // LANES

    step = pl.program_id(0)
    n_steps = pl.num_programs(0)

    def block_copy(i, s):
        hbm, ring = ((kt_hbm, kring), (vt_hbm, vring))[i]
        src = hbm.at[batch_lo + s // SA_NBLK, pl.ds((s % SA_NBLK) * SA_WIDTH, SA_WIDTH), :]
        return pltpu.make_async_copy(src, ring.at[s % RING], sems.at[i, s % RING])

    @pl.when(step == 0)
    def _():
        for s in range(RING - 1):
            for i in range(2):
                block_copy(i, s).start()

    @pl.when(step + RING - 1 < n_steps)
    def _():
        for i in range(2):
            block_copy(i, step + RING - 1).start()

    for i in range(2):
        block_copy(i, step).wait()
    kt_ref = kring.at[step % RING]
    vt_ref = vring.at[step % RING]

    def sample_attention():
        for i, new_ref in enumerate((kn_ref, vn_ref)):
            pad_ref[...] = jnp.zeros_like(pad_ref)
            pad_ref[0:t_new, :] = new_ref[...]
            newt_ref[i] = pad_ref[...].T
        k_ext = jnp.concatenate([kt_ref[...], newt_ref[0]], axis=1)
        v_ext = jnp.concatenate([vt_ref[...], newt_ref[1]], axis=1)
        w = w_ref[...]
        s = jnp.dot(qx_ref[...], k_ext.astype(BF16), preferred_element_type=F32)
        s = jnp.where(w > 0.0, s, NEG)
        m = jnp.max(s, axis=-1, keepdims=True)
        e = w * jnp.exp2(s - m)
        l = jnp.sum(e, axis=-1, keepdims=True)
        pv = lax.dot_general(e.astype(BF16), v_ext.astype(BF16), (((1,), (1,)), ((), ())),
                             preferred_element_type=F32)
        pv = pv * hm_ref[...] / l
        o = pv[0:t_new]
        for h in range(1, SA_HEADS):
            o = o + pv[h * t_new:(h + 1) * t_new]
        o_ref[...] = o

    boundary = {}

    def shift(part, n_parts):
        per = n_tiles // n_parts
        first = part * per
        keep = lax.broadcasted_iota(jnp.int32, (1, LANES), 1) < LANES - t_new
        if part == 0:
            boundary.clear()
        for i, (src_ref, dst_ref) in enumerate(((kt_ref, okt_ref), (vt_ref, ovt_ref))):
            def rotated(j):
                if (i, j) in boundary:
                    return boundary.pop((i, j))
                t = src_ref[:, j * LANES:(j + 1) * LANES] if j < n_tiles else newt_ref[i]
                return pltpu.roll(t, LANES - t_new, axis=1)
            rolled = [rotated(j) for j in range(first, first + per + 1)]
            for j in range(per):
                dst_ref[:, (first + j) * LANES:(first + j + 1) * LANES] = jnp.where(keep, rolled[j], rolled[j + 1])
            if part + 1 < n_parts:
                boundary[(i, first + per)] = rolled[per]

    sample_attention()
    if rider_takes_side:
        rider(*rider_in, *rider_out, *rider_scratch, side=shift)
    else:
        shift(0, 1)
        rider(*rider_in, *rider_out, *rider_scratch)


def _sa_weights(t_new, n_buf):
    ext = n_buf + LANES
    pos = np.arange(ext)[None, :]
    i = np.arange(t_new)[:, None]
    delta = n_buf + i - pos
    is_real = (pos < n_buf + t_new)
    w = np.zeros((t_new, ext), np.float32)
    for window, d in DILATIONS:
        w += ((delta >= 0) & (delta % d == 0) & (delta <= window) & is_real)
    return jnp.asarray(np.tile(w, (SA_HEADS, 1)), F32)


def _head_mask(t_new):
    h = np.repeat(np.arange(SA_HEADS), t_new)[:, None]
    c = np.arange(SA_WIDTH)[None, :] // HEAD_DIM
    return (h == c).astype(np.float32)


SA_NBLK = ATT_WIDTH // SA_WIDTH


def _stream(kt, vt, prev, batch_lo, n_batches, qx, k_new, v_new, rider, rider_args, rider_in_specs,
            rider_out_specs, rider_out_shapes, name, rider_scratch=(), rider_takes_side=False):
    bd, _, n_buf = kt.shape
    t_new = k_new.shape[1]
    assert n_buf == MAX_SPAN, "window buffer must cover the longest pattern (no invalid positions)"
    rows = t_new * SA_HEADS
    buf = pl.BlockSpec((None, SA_WIDTH, n_buf), lambda i: (batch_lo + i // SA_NBLK, i % SA_NBLK, 0))
    new = pl.BlockSpec((None, t_new, SA_WIDTH), lambda i: (batch_lo + i // SA_NBLK, 0, i % SA_NBLK))
    o_blk = pl.BlockSpec((None, t_new, SA_WIDTH), lambda i: (i // SA_NBLK, 0, i % SA_NBLK))
    qx_blk = pl.BlockSpec((None, None, rows, SA_WIDTH), lambda i: (batch_lo + i // SA_NBLK, i % SA_NBLK, 0, 0))
    sd_buf = jax.ShapeDtypeStruct((bd, ATT_WIDTH, n_buf), F32)
    hbm = pl.BlockSpec(memory_space=pl.ANY)
    in_specs = [hbm, hbm, qx_blk, new, new, _const_spec((rows, n_buf + LANES)), _const_spec((rows, SA_WIDTH))]
    in_specs += list(rider_in_specs)
    args = [kt, vt, qx, k_new, v_new, _sa_weights(t_new, n_buf), jnp.asarray(_head_mask(t_new), F32)]
    args += list(rider_args)
    aliases = {}
    if prev is not None:
        aliases = {len(args): 0, len(args) + 1: 1}
        in_specs += [pl.BlockSpec(memory_space=pl.ANY)] * 2
        args += list(prev)
    return pl.pallas_call(
        functools.partial(_stream_kernel, rider=rider, rider_takes_side=rider_takes_side, batch_lo=batch_lo,
                          n_rider_in=len(rider_in_specs),
                          n_rider_out=len(rider_out_specs), n_prev=len(aliases),
                          n_rider_scratch=len(rider_scratch)),
        grid=(n_batches * SA_NBLK,),
        in_specs=in_specs,
        out_specs=[buf, buf, o_blk] + list(rider_out_specs),
        out_shape=[sd_buf, sd_buf, jax.ShapeDtypeStruct((n_batches, t_new, ATT_WIDTH), F32)] + list(rider_out_shapes),
        input_output_aliases=aliases,
        scratch_shapes=[pltpu.VMEM((LANES, SA_WIDTH), F32), pltpu.VMEM((2, SA_WIDTH, LANES), F32),
                        pltpu.VMEM((RING, SA_WIDTH, n_buf), F32), pltpu.VMEM((RING, SA_WIDTH, n_buf), F32),
                        pltpu.SemaphoreType.DMA((2, RING))] + list(rider_scratch),
        compiler_params=_cparams(("arbitrary",)),
        name=name,
    )(*args)


def _expand_queries(q):
    bd, t_new, _ = q.shape
    hm = jnp.asarray(_head_mask(t_new), F32)
    qb = q.reshape(bd, t_new, SA_NBLK, SA_WIDTH).transpose(0, 2, 1, 3)
    return (jnp.tile(qb, (1, 1, SA_HEADS, 1)) * hm).astype(BF16)


def _rider_tiling(b, s, steps):
    assert (b * s) % steps == 0
    tm = b * s // steps
    assert tm % LANES == 0 and s % tm == 0
    return tm, s // tm


def _prompt_mod_spec(per_seq, col):
    return pl.BlockSpec((None, 1, D_MODEL), lambda i: (i // per_seq, 0, col))


def _inproj_prompt(x, mod, g, w_bf, qg, kg):
    b, s, _ = x.shape
    tm = TOKEN_TILE
    per_seq = s // tm
    keep = min(MAX_SPAN, s)
    kt_first = per_seq - keep // tm
    tok = lambda w: pl.BlockSpec((tm, w), lambda i: (i, 0))
    out_t =pl.BlockSpec((None, ATT_WIDTH, tm), lambda i: (i // per_seq, 0, jnp.maximum(i % per_seq - kt_first, 0)))
    sd = jax.ShapeDtypeStruct((b * s, ATT_WIDTH), F32)
    sdt = jax.ShapeDtypeStruct((b, ATT_WIDTH, keep), F32)
    q, k, v, u, kt_p, vt_p = pl.pallas_call(
        functools.partial(_inproj_body, kt_rule=(per_seq, kt_first)),
        grid=(b * per_seq,),
        in_specs=[tok(D_MODEL), _prompt_mod_spec(per_seq, MOD_SCALE1), _prompt_mod_spec(per_seq, MOD_SHIFT1),
                  _const_spec((1, D_MODEL)), _const_spec((D_MODEL, 4 * ATT_WIDTH)),
                  _const_spec((1, ATT_WIDTH)), _const_spec((1, ATT_WIDTH))],
        out_specs=[tok(ATT_WIDTH)] * 4 + [out_t] * 2,
        out_shape=[sd] * 4 + [sdt] * 2,
        compiler_params=_cparams(("arbitrary",)),
        name="inproj_prompt",
    )(x.reshape(b * s, D_MODEL), mod, mod, g, w_bf, qg, kg)
    r3 = lambda a: a.reshape(b, s, ATT_WIDTH)
    return r3(q), r3(k), r3(v), r3(u), kt_p, vt_p


def _attn_stream_batches(b, s):
    n_super = s // SUPER
    units = b * (ATT_WIDTH // LANES) * n_super
    steps = units * ATTN_PHASES
    assert steps % SA_NBLK == 0
    return steps // SA_NBLK


def _stream_attn(kt, vt, batch_lo, n_batches, qx, k_new, v_new, q, k, v):
    b, s, _ = q.shape
    assert s % SUPER == 0
    n_super = s // SUPER
    n_hp = ATT_WIDTH // LANES
    assert n_batches == _attn_stream_batches(b, s)

    n_units = b * n_hp * n_super

    def unit_block(lead):
        def index(i):
            un = jnp.minimum((i + lead) // ATTN_PHASES, n_units - 1)
            return un // (n_hp * n_super), un % n_super, (un // n_super) % n_hp
        return pl.BlockSpec((None, SUPER, LANES), index)

    outs = _stream(
        kt, vt, None, batch_lo, n_batches, qx, k_new, v_new,
        functools.partial(_attn_rider, n_super=n_super),
        (q, k, v, _attn_bias()),
        [unit_block(1), unit_block(ATTN_PHASES - 1), unit_block(ATTN_PHASES - 1), _const_spec((2, QB, 2 * QB))],
        [unit_block(0)], [jax.ShapeDtypeStruct((b, s, ATT_WIDTH), F32)], "stream_attn",
        rider_scratch=[pltpu.VMEM((2 * SUPER, LANES), F32)] * 2 + [pltpu.VMEM((len(DILATIONS), SUPER, LANES), F32)] * 3,
        rider_takes_side=True)
    okt, ovt, o_s, o_att = outs
    return (okt, ovt), o_s, o_att


def _stream_post(kt, vt, prev, batch_lo, n_batches, qx, k_new, v_new, x, o_att, y_ssm, mod, consts, weights):
    b, s, _ = x.shape
    tm, per_seq = _rider_tiling(b, s, n_batches * SA_NBLK)
    tok = lambda w: pl.BlockSpec((tm, w), lambda i: (i, 0))
    mods = [_prompt_mod_spec(per_seq, c) for c in (MOD_GATE1, MOD_SCALE2, MOD_SHIFT2, MOD_GATE2)]
    okt, ovt, o_s, y = _stream(
        kt, vt, prev, batch_lo, n_batches, qx, k_new, v_new, _post_body,
        (x.reshape(b * s, D_MODEL), o_att.reshape(b * s, ATT_WIDTH), y_ssm.reshape(b * s, SSM_WIDTH),
         mod, mod, mod, mod, *consts, *weights),
        [tok(D_MODEL), tok(ATT_WIDTH), tok(SSM_WIDTH)] + mods
        + [_const_spec(c.shape) for c in consts] + [_const_spec(w.shape) for w in weights],
        [tok(D_MODEL)], [jax.ShapeDtypeStruct((b * s, D_MODEL), F32)], "stream_post")
    return okt, ovt, o_s, y.reshape(b, s, D_MODEL)


def _s5_tables(a_re, a_im, log_dt, b_re, b_im):
    g, n = a_re.shape
    dt = jnp.exp(log_dt)[:, None]
    x, y = dt * a_re, dt * a_im
    ks = jnp.arange(CHUNK + 1, dtype=F32)
    mag = jnp.exp(x[:, :, None] * ks)
    pw_re, pw_im = mag * jnp.cos(y[:, :, None] * ks), mag * jnp.sin(y[:, :, None] * ks)
    e_re = jnp.expm1(x) * jnp.cos(y) - 2.0 * jnp.sin(0.5 * y) ** 2
    e_im = jnp.exp(x) * jnp.sin(y)
    den = a_re * a_re + a_im * a_im
    f_re = (e_re * a_re + e_im * a_im) / den
    f_im = (e_im * a_re - e_re * a_im) / den
    bb_re = f_re[..., None] * b_re - f_im[..., None] * b_im
    bb_im = f_re[..., None] * b_im + f_im[..., None] * b_re
    return dict(pw_re=pw_re, pw_im=pw_im, bb_re=bb_re, bb_im=bb_im)


def _s5_chunk_operators(tb, c_re, c_im, d_skip):
    pw_re, pw_im, bb_re, bb_im = tb["pw_re"], tb["pw_im"], tb["bb_re"], tb["bb_im"]
    g, n, ch = bb_re.shape
    L = CHUNK
    hp = lax.Precision.HIGHEST
    ct_re, ct_im = c_re.transpose(0, 2, 1), c_im.transpose(0, 2, 1)
    pr, pi = pw_re[:, :, :, None], pw_im[:, :, :, None]
    ca_re = ct_re[:, :, None, :] * pr - ct_im[:, :, None, :] * pi
    ca_im = ct_re[:, :, None, :] * pi + ct_im[:, :, None, :] * pr
    a_cat = jnp.concatenate([ca_re[:, :, :L].reshape(g, n, L * ch), -ca_im[:, :, :L].reshape(g, n, L * ch)], axis=1)
    b_cat = jnp.concatenate([bb_re, bb_im], axis=1)
    base = jnp.einsum("gnd,gnx->gdx", b_cat, a_cat, precision=hp)
    skip = jnp.eye(ch, dtype=F32)[None] * d_skip.reshape(g, ch, 1)
    base = base + jnp.pad(skip, ((0, 0), (0, 0), (0, (L - 1) * ch)))
    toe = jnp.stack([jnp.pad(base, ((0, 0), (0, 0), (s * ch, 0)))[:, :, :L * ch] for s in range(L)], axis=1)
    toe = toe.reshape(g, L * ch, L * ch)
    rv_re = pw_re[:, :, L - 1::-1].transpose(0, 2, 1)[:, :, None, :]
    rv_im = pw_im[:, :, L - 1::-1].transpose(0, 2, 1)[:, :, None, :]
    bt_re, bt_im = bb_re.transpose(0, 2, 1)[:, None], bb_im.transpose(0, 2, 1)[:, None]
    ws_re = (rv_re * bt_re - rv_im * bt_im).reshape(g, L * ch, n)
    ws_im = (rv_re * bt_im + rv_im * bt_re).reshape(g, L * ch, n)
    vr = ca_re[:, :, 1:].reshape(g, n, L * ch)
    vi = (-ca_im[:, :, 1:]).reshape(g, n, L * ch)
    gp = g // 2
    z_w = jnp.zeros((gp, L * ch, n), F32)
    wr, wi = ws_re.reshape(gp, 2, L * ch, n), ws_im.reshape(gp, 2, L * ch, n)
    w_pair = jnp.concatenate([
        jnp.concatenate([wr[:, 0], z_w, wi[:, 0], z_w], axis=2),
        jnp.concatenate([z_w, wr[:, 1], z_w, wi[:, 1]], axis=2)], axis=1)
    z_v = jnp.zeros((gp, n, L * ch), F32)
    vr, vi = vr.reshape(gp, 2, n, L * ch), vi.reshape(gp, 2, n, L * ch)
    v_pair = jnp.concatenate([
        jnp.concatenate([vr[:, 0], z_v], axis=2), jnp.concatenate([z_v, vr[:, 1]], axis=2),
        jnp.concatenate([vi[:, 0], z_v], axis=2), jnp.concatenate([z_v, vi[:, 1]], axis=2)], axis=1)
    a16_re = pw_re[:, :, L].reshape(1, g * n)
    a16_im = pw_im[:, :, L].reshape(1, g * n)
    return toe.astype(BF16), w_pair.astype(BF16), v_pair.astype(BF16), a16_re, a16_im


GROUPS_PER_BLOCK = LANES // SSM_CH
REGROUP_ROWS = 32


def _granule_transpose(arrs):
    n = GROUPS_PER_BLOCK
    gran = lax.broadcasted_iota(jnp.int32, (1, LANES), 1) // SSM_CH
    cur = list(arrs)
    k = n // 2
    while k >= 1:
        keep_low = (gran % (2 * k)) < k
        nxt = list(cur)
        for t in range(n):
            if t % (2 * k) < k:
                a, b = cur[t], cur[t + k]
                nxt[t] = jnp.where(keep_low, a, pltpu.roll(b, k * SSM_CH, axis=1))
                nxt[t + k] = jnp.where(keep_low, pltpu.roll(a, LANES - k * SSM_CH, axis=1), b)
        cur = nxt
        k //= 2
    return cur


def _ssm_a_kernel(u_ref, w_ref, xg_ref, sre_ref, sim_ref, xs_ref):
    nb, s_len, _ = u_ref.shape
    n_chunks = s_len // CHUNK
    half = CHUNK // 2

    for b in range(nb):
        def regroup(ct, c, b=b):
            c0 = pl.multiple_of(ct * REGROUP_ROWS, REGROUP_ROWS)
            rows = pl.ds(b * n_chunks + c0, REGROUP_ROWS)
            for j in range(2):
                z = [u_ref[b, pl.ds(c0 * CHUNK + half * j + tt, REGROUP_ROWS, stride=CHUNK), :]
                     for tt in range(half)]
                for p, xp in enumerate(_granule_transpose(z)):
                    xs_ref[2 * p + j, rows, :] = xp
            return c
        lax.fori_loop(0, n_chunks // REGROUP_ROWS, regroup, 0)

    for col in range(2 * GROUPS_PER_BLOCK):
        xg_ref[:, col * LANES:(col + 1) * LANES] = xs_ref[col].astype(BF16)
    for pp in range(GROUPS_PER_BLOCK // 2):
        xp = xg_ref[:, pp * 4 * LANES:(pp + 1) * 4 * LANES]
        st = jnp.dot(xp, w_ref[pp], preferred_element_type=F32)
        sre_ref[pp] = st[:, 0:LANES]
        sim_ref[pp] = st[:, LANES:2 * LANES]


def _ssm_scan_kernel(sre_ref, sim_ref, are_ref, aim_ref, hre_ref, him_ref, fre_ref, fim_ref, *, nb):
    ncb, rows, _ = sre_ref.shape
    n_chunks = rows // nb
    a_re = [are_ref[cb] for cb in range(ncb)]
    a_im = [aim_ref[cb] for cb in range(ncb)]

    def body(c, carry):
        r = pl.ds(c, nb, stride=n_chunks)
        out = []
        for cb in range(ncb):
            h_re, h_im = carry[2 * cb], carry[2 * cb + 1]
            hre_ref[cb, r, :] = h_re
            him_ref[cb, r, :] = h_im
            out.append(a_re[cb] * h_re - a_im[cb] * h_im + sre_ref[cb, r, :])
            out.append(a_re[cb] * h_im + a_im[cb] * h_re + sim_ref[cb, r, :])
        return tuple(out)

    z = jnp.zeros((nb, LANES), F32)
    fin = lax.fori_loop(0, n_chunks, body, (z,) * (2 * ncb), unroll=4)
    for cb in range(ncb):
        fre_ref[cb] = fin[2 * cb]
        fim_ref[cb] = fin[2 * cb + 1]


def _ssm_c_kernel(xg_ref, hre_ref, him_ref, t_ref, v_ref, y_ref, ys_ref):
    nb, s_len, _ = y_ref.shape
    n_chunks = s_len // CHUNK
    half = CHUNK // 2
    for pp in range(GROUPS_PER_BLOCK // 2):
        hcat = jnp.concatenate([hre_ref[pp], him_ref[pp]], axis=1).astype(BF16)
        inter = jnp.dot(hcat, v_ref[pp], preferred_element_type=F32)
        for e in range(2):
            g = 2 * pp + e
            intra = jnp.dot(xg_ref[:, g * 2 * LANES:(g + 1) * 2 * LANES], t_ref[g],
                            preferred_element_type=F32)
            yg = intra + inter[:, e * 2 * LANES:(e + 1) * 2 * LANES]
            ys_ref[2 * g] = yg[:, 0:LANES]
            ys_ref[2 * g + 1] = yg[:, LANES:2 * LANES]

    for b in range(nb):
        def regroup(ct, c, b=b):
            c0 = pl.multiple_of(ct * REGROUP_ROWS, REGROUP_ROWS)
            rows = pl.ds(b * n_chunks + c0, REGROUP_ROWS)
            for j in range(2):
                yp =[ys_ref[2 * p + j, rows, :] for p in range(GROUPS_PER_BLOCK)]
                for tt, zt in enumerate(_granule_transpose(yp)):
                    y_ref[b, pl.ds(c0 * CHUNK + half * j + tt, REGROUP_ROWS, stride=CHUNK), :] = zt
            return c
        lax.fori_loop(0, n_chunks // REGROUP_ROWS, regroup, 0)


def _ssm_prompt(u, toe, w_pair, v_pair, a16_re, a16_im):
    b, s, _ = u.shape
    rows = b * (s // CHUNK)
    nblk = SSM_WIDTH // LANES
    gw = GROUPS_PER_BLOCK * CHUNK * SSM_CH
    ncb = GROUPS_PER_BLOCK * SSM_STATE // LANES
    ublk = pl.BlockSpec((b, s, LANES), lambda j: (0, 0, j))
    sblk = pl.BlockSpec((ncb, rows, LANES), lambda j: (j, 0, 0))
    sd_h = jax.ShapeDtypeStruct((nblk * ncb, rows, LANES), F32)
    xg, s_re, s_im = pl.pallas_call(
        _ssm_a_kernel,
        grid=(nblk,),
        in_specs=[ublk, pl.BlockSpec((GROUPS_PER_BLOCK // 2, 4 * LANES, 2 * LANES), lambda j: (j, 0, 0))],
        out_specs=[pl.BlockSpec((rows, gw), lambda j: (0, j)), sblk, sblk],
        out_shape=[jax.ShapeDtypeStruct((rows, nblk * gw), BF16), sd_h, sd_h],
        scratch_shapes=[pltpu.VMEM((gw // LANES, rows, LANES), F32)],
        compiler_params=_cparams(("arbitrary",)),
        name="ssm_chunk_states",
    )(u, w_pair)

    ablk = pl.BlockSpec((ncb, 1, LANES), lambda j: (j, 0, 0))
    fblk = pl.BlockSpec((ncb, b, LANES), lambda j: (j, 0, 0))
    sd_f = jax.ShapeDtypeStruct((nblk * ncb, b, LANES), F32)
    a16_re = a16_re.reshape(nblk * ncb, 1, LANES)
    a16_im = a16_im.reshape(nblk * ncb, 1, LANES)
    h_re, h_im, f_re, f_im = pl.pallas_call(
        functools.partial(_ssm_scan_kernel, nb=b),
        grid=(nblk,),
        in_specs=[sblk, sblk, ablk, ablk],
        out_specs=[sblk, sblk, fblk, fblk],
        out_shape=[sd_h, sd_h, sd_f, sd_f],
        compiler_params=_cparams(("arbitrary",)),
        name="ssm_scan",
    )(s_re, s_im, a16_re, a16_im)

    y = pl.pallas_call(
        _ssm_c_kernel,
        grid=(nblk,),
        in_specs=[pl.BlockSpec((rows, gw), lambda j: (0, j)), sblk, sblk,
                  pl.BlockSpec((GROUPS_PER_BLOCK, 2 * LANES, 2 * LANES), lambda j: (j, 0, 0)),
                  pl.BlockSpec((GROUPS_PER_BLOCK // 2, 2 * LANES, 4 * LANES), lambda j: (j, 0, 0))],
        out_specs=ublk,
        out_shape=jax.ShapeDtypeStruct((b, s, SSM_WIDTH), F32),
        scratch_shapes=[pltpu.VMEM((gw // LANES, rows, LANES), F32)],
        compiler_params=_cparams(("arbitrary",)),
        name="ssm_outputs",
    )(xg, h_re, h_im, toe, v_pair)
    to_rows = lambda f: jnp.transpose(f, (1, 0, 2)).reshape(b, nblk * ncb * LANES)
    return y, to_rows(f_re), to_rows(f_im)


def _ssm_sample_kernel(u_ref, hre_ref, him_ref, are_ref, aim_ref, bre_ref, bim_ref, cre_ref, cim_ref, d_ref,
                       y_ref, ore_ref, oim_ref, *, t_new):
    h_re = hre_ref[...].T
    h_im = him_ref[...].T
    a_re, a_im = are_ref[...], aim_ref[...]
    bd = h_re.shape[0]
    for t in range(t_new):
        u = u_ref[t]
        ub = u.astype(BF16)
        n_re = a_re * h_re - a_im * h_im + jnp.dot(ub, bre_ref[...], preferred_element_type=F32)
        n_im = a_re * h_im + a_im * h_re + jnp.dot(ub, bim_ref[...], preferred_element_type=F32)
        h_re, h_im = n_re, n_im
        y = (jnp.dot(h_re.astype(BF16), cre_ref[...], preferred_element_type=F32)
             + jnp.dot(h_im.astype(BF16), cim_ref[...], preferred_element_type=F32) + d_ref[...] * u)
        y_ref[t] = y
    ore_ref[...] = h_re.T
    oim_ref[...] = h_im.T


def _block_diag(m):
    g, r, c = m.shape
    spread = jnp.asarray(np.tile(np.eye(c, dtype=np.float32), (1, g)), BF16)
    tiled = jnp.dot(m.reshape(g * r, c).astype(BF16), spread, preferred_element_type=F32)
    keep = np.arange(g * r)[:, None] // r == np.arange(g * c)[None, :] // c
    return jnp.where(jnp.asarray(keep), tiled, 0.0).astype(BF16)


def _ssm_sample(u, h0_re_t, h0_im_t, tb, c_re, c_im, d_skip, t_new):
    gn = SSM_GROUPS * SSM_STATE
    a_re = tb["pw_re"][:, :, 1].reshape(1, gn)
    a_im = tb["pw_im"][:, :, 1].reshape(1, gn)
    b_re = _block_diag(tb["bb_re"].transpose(0, 2, 1))
    b_im = _block_diag(tb["bb_im"].transpose(0, 2, 1))
    cb_re = _block_diag(c_re.transpose(0, 2, 1))
    cb_im = _block_diag(-c_im.transpose(0, 2, 1))
    bd = u.shape[1]
    full = lambda shape: pl.BlockSpec(shape, lambda i: (0,) * len(shape))
    return pl.pallas_call(
        functools.partial(_ssm_sample_kernel, t_new=t_new),
        grid=(1,),
        in_specs=[full((t_new, bd, SSM_WIDTH)), full((gn, bd)), full((gn, bd)), full((1, gn)), full((1, gn)),
                  full((SSM_WIDTH, gn)), full((SSM_WIDTH, gn)), full((gn, SSM_WIDTH)), full((gn, SSM_WIDTH)),
                  full((1, SSM_WIDTH))],
        out_specs=[full((t_new, bd, SSM_WIDTH)), full((gn, bd)), full((gn, bd))],
        out_shape=[jax.ShapeDtypeStruct((t_new, bd, SSM_WIDTH), F32),
                   jax.ShapeDtypeStruct((gn, bd), F32), jax.ShapeDtypeStruct((gn, bd), F32)],
        compiler_params=_cparams(("arbitrary",)),
        name="ssm_sample",
    )(u, h0_re_t, h0_im_t, a_re, a_im, b_re, b_im, cb_re, cb_im, d_skip.reshape(1, SSM_WIDTH))


def _rms(x, gain):
    return x * lax.rsqrt(jnp.mean(x * x, axis=-1, keepdims=True) + EPS) * gain


def _gelu_tanh(x):
    return 0.5 * x * (1.0 + jnp.tanh(math.sqrt(2.0 / math.pi) * (x + 0.044715 * (x * x * x))))


def _post_body(x_ref, oa_ref, ys_ref, g1_ref, sc2_ref, sh2_ref, g2_ref, n2_ref, ag_ref, sg_ref,
               wglu_ref, wout_ref, wg_ref, wu_ref, wd_ref, o_ref):
    ya = _gelu_tanh(ys_ref[...])
    ya = ya * _sigmoid(jnp.dot(ya.astype(BF16), wglu_ref[...], preferred_element_type=F32))
    merged = jnp.concatenate([_rms(oa_ref[...], ag_ref[...]), _rms(ya, sg_ref[...])], axis=1)
    x1 = x_ref[...] + g1_ref[...] * jnp.dot(merged.astype(BF16), wout_ref[...], preferred_element_type=F32)
    h2 = (_rms(x1, n2_ref[...]) * (1.0 + sc2_ref[...]) + sh2_ref[...]).astype(BF16)
    gate = jnp.dot(h2, wg_ref[...], preferred_element_type=F32)
    up = jnp.dot(h2, wu_ref[...], preferred_element_type=F32)
    act = (gate * _sigmoid(gate) * up).astype(BF16)
    o_ref[...] = x1 + g2_ref[...] * jnp.dot(act, wd_ref[...], preferred_element_type=F32)


def _post_sample(x, o_att, y_ssm, mod, bd, consts, weights):
    n = x.shape[0]
    tok = lambda w: pl.BlockSpec((bd, w), lambda t: (t, 0))
    mods = [_sample_mod_spec(bd, c) for c in (MOD_GATE1, MOD_SCALE2, MOD_SHIFT2, MOD_GATE2)]
    return pl.pallas_call(
        _post_body,
        grid=(n // bd,),
        in_specs=[tok(D_MODEL), tok(ATT_WIDTH), tok(SSM_WIDTH)] + mods
                 + [_const_spec(c.shape) for c in consts] + [_const_spec(w.shape) for w in weights],
        out_specs=tok(D_MODEL),
        out_shape=jax.ShapeDtypeStruct((n, D_MODEL), F32),
        compiler_params=_cparams(("arbitrary",)),
        name="post_sample",
    )(x, o_att, y_ssm, mod, mod, mod, mod, *consts, *weights)


def kernel(x_prompt, x_sample, cache_k, cache_v, state_ssm_re, state_ssm_im, c_prompt, c_sample, norm1_g, norm2_g, w_ada, b_ada, w_in, q_gain, k_gain, ssm_a_re, ssm_a_im, ssm_log_dt, ssm_b_re, ssm_b_im, ssm_c_re, ssm_c_im, ssm_d, w_glu, attn_out_g, ssm_out_g, w_out, w_gate, w_up, w_down):
    depth = norm1_g.shape[0]
    assert depth == 1, "one decoder layer"
    b, s, _ = x_prompt.shape
    bd, t_new, _ = x_sample.shape
    n_buf = cache_k.shape[2]
    L = 0

    assert bd % SUBLANES == 0
    pad = (-(b + bd)) % SUBLANES
    c_all = jnp.concatenate([c_sample, c_prompt, jnp.zeros((pad, D_MODEL), F32)], axis=0)
    mod = _ada(c_all, w_ada[L], b_ada[L].reshape(1, -1))
    mod_p = mod[bd:bd + b].reshape(b, 1, N_MOD * D_MODEL)

    w_in_bf = w_in[L].astype(BF16)
    n1 = norm1_g[L].reshape(1, D_MODEL)
    qg = jnp.tile(q_gain[L], ATT_HEADS).reshape(1, ATT_WIDTH)
    kg = jnp.tile(k_gain[L], ATT_HEADS).reshape(1, ATT_WIDTH)
    consts = (norm2_g[L].reshape(1, D_MODEL), attn_out_g[L].reshape(1, ATT_WIDTH), ssm_out_g[L].reshape(1, SSM_WIDTH))
    weights = tuple(w[L].astype(BF16) for w in (w_glu, w_out, w_gate, w_up, w_down))

    tb = _s5_tables(ssm_a_re[L], ssm_a_im[L], ssm_log_dt[L], ssm_b_re[L], ssm_b_im[L])
    toe, w_pair, v_pair, a16_re, a16_im = _s5_chunk_operators(tb, ssm_c_re[L], ssm_c_im[L], ssm_d[L])

    xs = jnp.transpose(x_sample, (1, 0, 2)).reshape(t_new * bd, D_MODEL)
    qs, ks, vs, us = _inproj_sample(xs, mod, bd, n1, w_in_bf, qg, kg)
    batch_major = lambda a: jnp.transpose(a.reshape(t_new, bd, ATT_WIDTH), (1, 0, 2))
    qx = _expand_queries(batch_major(qs))
    ks3, vs3 = batch_major(ks), batch_major(vs)
    kt = jnp.transpose(cache_k[L], (0, 2, 3, 1)).reshape(bd, ATT_WIDTH, n_buf)
    vt = jnp.transpose(cache_v[L], (0, 2, 3, 1)).reshape(bd, ATT_WIDTH, n_buf)
    q, k, v, u, kt_p, vt_p = _inproj_prompt(x_prompt, mod_p, n1, w_in_bf, qg, kg)
    n_first = _attn_stream_batches(b, s)
    assert 0 < n_first < bd
    shifted, o_s_first, o_att = _stream_attn(kt, vt, 0, n_first, qx, ks3, vs3, q, k, v)
    y_ssm, f_re, f_im = _ssm_prompt(u, toe, w_pair, v_pair, a16_re, a16_im)
    okt, ovt, o_s_second, y_prompt = _stream_post(
        kt, vt, shifted, n_first, bd - n_first, qx, ks3, vs3, x_prompt, o_att, y_ssm, mod_p, consts, weights)
    o_att_s = jnp.concatenate([o_s_first, o_s_second], axis=0)
    gn = SSM_GROUPS * SSM_STATE
    h0_re = jnp.transpose(state_ssm_re[L], (1, 2, 0)).reshape(gn, bd)
    h0_im = jnp.transpose(state_ssm_im[L], (1, 2, 0)).reshape(gn, bd)
    y_ssm_t, hs_re, hs_im = _ssm_sample(us.reshape(t_new, bd, SSM_WIDTH), h0_re, h0_im, tb,
                                        ssm_c_re[L], ssm_c_im[L], ssm_d[L], t_new)
    o_att_t = jnp.transpose(o_att_s, (1, 0, 2)).reshape(t_new * bd, ATT_WIDTH)
    y_sample = _post_sample(xs, o_att_t, y_ssm_t.reshape(t_new * bd, SSM_WIDTH), mod, bd, consts, weights)
    y_sample = jnp.transpose(y_sample.reshape(t_new, bd, D_MODEL), (1, 0, 2))

    def from_t(a, nb, keep):
        return jnp.transpose(a.reshape(nb, ATT_HEADS, HEAD_DIM, keep), (0, 3, 1, 2))[None]

    def state_from_t(a):
        return jnp.transpose(a.reshape(SSM_GROUPS, SSM_STATE, bd), (2, 0, 1))[None]

    keep = min(MAX_SPAN, s)
    return (y_prompt, y_sample,
            from_t(kt_p, b, keep), from_t(vt_p, b, keep),
            f_re.reshape(b, SSM_GROUPS, SSM_STATE)[None], f_im.reshape(b, SSM_GROUPS, SSM_STATE)[None],
            from_t(okt, bd, n_buf), from_t(ovt, bd, n_buf),
            state_from_t(hs_re), state_from_t(hs_im))
```

```python
import functools
import math

import jax
import jax.numpy as jnp
import numpy as np
from jax import lax
from jax.experimental import pallas as pl
from jax.experimental.pallas import tpu as pltpu

F32 = jnp.float32
BF16 = jnp.bfloat16

D_MODEL = 1024
HEAD_DIM = 64
ATT_WIDTH = 512
ATT_HEADS = 8
SSM_WIDTH = 512
SSM_CH = 16
SSM_GROUPS = 32
SSM_STATE = 64
DILATIONS = ((128, 1), (512, 4), (2048, 16))
N_BACK = 128
MAX_SPAN = 2048
FFN_HIDDEN = 2816
N_MOD = 6
MOD_SHIFT1, MOD_SCALE1, MOD_GATE1, MOD_SHIFT2, MOD_SCALE2, MOD_GATE2 = range(N_MOD)
EPS = 1e-6

LANES = 128
SUBLANES = 8
VMEM_LIMIT = 56 * 1024 * 1024

TOKEN_TILE = 512
SUPER = 2048
QB = 128
TILE_GROUP = 4
Q_SCALE = HEAD_DIM ** -0.5 * math.log2(math.e)
CHUNK = 16
NEG = -1e30


def _cparams(sem=None):
    return pltpu.CompilerParams(dimension_semantics=sem, vmem_limit_bytes=VMEM_LIMIT)


def _const_spec(shape):
    nd = len(shape)
    return pl.BlockSpec(shape, lambda *_: (0,) * nd, pipeline_mode=pl.Buffered(1))


def _sigmoid(x):
    return 1.0 / (1.0 + jnp.exp(-x))


def _split_bf16(a):
    hi = a.astype(BF16)
    lo = (a - hi.astype(F32)).astype(BF16)
    return hi, lo


def _ada_kernel(c_ref, w_ref, b_ref, o_ref):
    c = c_ref[...]
    a = c * _sigmoid(c)
    a_hi, a_lo = _split_bf16(a)
    w_hi, w_lo = _split_bf16(w_ref[...])
    acc = jnp.dot(a_hi, w_hi, preferred_element_type=F32)
    acc += jnp.dot(a_hi, w_lo, preferred_element_type=F32)
    acc += jnp.dot(a_lo, w_hi, preferred_element_type=F32)
    o_ref[...] = acc + b_ref[...]


def _ada(c_all, w_ada, b_ada):
    rows = c_all.shape[0]
    n = w_ada.shape[1]
    tn = 1024
    return pl.pallas_call(
        _ada_kernel,
        grid=(n // tn,),
        in_specs=[pl.BlockSpec((rows, D_MODEL), lambda j: (0, 0)),
                  pl.BlockSpec((D_MODEL, tn), lambda j: (0, j)),
                  pl.BlockSpec((1, tn), lambda j: (0, j))],
        out_specs=pl.BlockSpec((rows, tn), lambda j: (0, j)),
        out_shape=jax.ShapeDtypeStruct((rows, n), F32),
        compiler_params=_cparams(("arbitrary",)),
        name="ada",
    )(c_all, w_ada, b_ada)


def _head_rms(z, gain):
    lane = lax.broadcasted_iota(jnp.int32, (1, LANES), 1)
    lo = lane < HEAD_DIM
    outs = []
    for c in range(z.shape[1] // LANES):
        blk = z[:, c * LANES:(c + 1) * LANES]
        sq = blk * blk
        s_lo = jnp.sum(jnp.where(lo, sq, 0.0), axis=-1, keepdims=True)
        s_hi = jnp.sum(jnp.where(lo, 0.0, sq), axis=-1, keepdims=True)
        inv = jnp.where(lo, lax.rsqrt(s_lo * (1.0 / HEAD_DIM) + EPS), lax.rsqrt(s_hi * (1.0 / HEAD_DIM) + EPS))
        outs.append(blk * inv)
    return jnp.concatenate(outs, axis=1) * gain


def _inproj_body(x_ref, scale_ref, shift_ref, g_ref, w_ref, qg_ref, kg_ref, *out_refs, kt_rule=None):
    q_ref, k_ref, v_ref, u_ref = out_refs[:4]
    x = x_ref[...]
    ms = jnp.mean(x * x, axis=-1, keepdims=True)
    h = x * lax.rsqrt(ms + EPS) * g_ref[...]
    h = h * (1.0 + scale_ref[...]) + shift_ref[...]
    z = jnp.dot(h.astype(BF16), w_ref[...], preferred_element_type=F32)
    q = _head_rms(z[:, 0:ATT_WIDTH], qg_ref[...]) * Q_SCALE
    k = _head_rms(z[:, ATT_WIDTH:2 * ATT_WIDTH], kg_ref[...])
    v = z[:, 2 * ATT_WIDTH:3 * ATT_WIDTH]
    q_ref[...] = q
    k_ref[...] = k
    v_ref[...] = v
    u_ref[...] = z[:, 3 * ATT_WIDTH:]
    if kt_rule is not None:
        per_seq, kt_first = kt_rule
        kt_ref, vt_ref = out_refs[4:]

        @pl.when(pl.program_id(0) % per_seq >= kt_first)
        def _():
            kt_ref[...] = k.T
            vt_ref[...] = v.T


def _sample_mod_spec(bd, col):
    return pl.BlockSpec((bd, D_MODEL), lambda t: (0, col))


def _inproj_sample(x, mod, bd, g, w_bf, qg, kg):
    n = x.shape[0]
    tok = pl.BlockSpec((bd, D_MODEL), lambda t: (t, 0))
    out_tok = pl.BlockSpec((bd, ATT_WIDTH), lambda t: (t, 0))
    sd = jax.ShapeDtypeStruct((n, ATT_WIDTH), F32)
    return pl.pallas_call(
        _inproj_body,
        grid=(n // bd,),
        in_specs=[tok, _sample_mod_spec(bd, MOD_SCALE1), _sample_mod_spec(bd, MOD_SHIFT1), _const_spec((1, D_MODEL)),
                  _const_spec((D_MODEL, 4 * ATT_WIDTH)), _const_spec((1, ATT_WIDTH)), _const_spec((1, ATT_WIDTH))],
        out_specs=[out_tok] * 4,
        out_shape=[sd] * 4,
        compiler_params=_cparams(("arbitrary",)),
        name="inproj_sample",
    )(x, mod, mod, g, w_bf, qg, kg)


ATTN_PHASES = len(DILATIONS) + 1


def _attn_rider(q_ref, kc_ref, vc_ref, bias_ref, o_ref, kbuf, vbuf, acc_ref, m_ref, l_ref, *, n_super, side):
    step = pl.program_id(0)
    phase = step % ATTN_PHASES
    st = (step // ATTN_PHASES) % n_super
    lane = lax.broadcasted_iota(jnp.int32, (1, LANES), 1)
    lo = lane < HEAD_DIM
    nt_contract = (((1,), (1,)), ((), ()))
    n_groups = SUPER // QB // TILE_GROUP

    def pattern(p, d):
        per_res = SUPER // d // QB
        if p == 0:
            for buf, cur_ref in ((kbuf, kc_ref), (vbuf, vc_ref)):
                @pl.when(st == 0)
                def _(buf=buf):
                    buf[0:SUPER, :] = jnp.zeros((SUPER, LANES), F32)

                @pl.when(st > 0)
                def _(buf=buf):
                    buf[0:SUPER, :] = buf[SUPER:2 * SUPER, :]

                buf[SUPER:2 * SUPER, :] = cur_ref[...]

        for gi in range(n_groups):
            tiles = []
            for u in range(TILE_GROUP):
                ti = gi * TILE_GROUP + u
                r = ti // per_res
                jt = ti % per_res
                row0 = r + d * QB * jt
                ks = SUPER + row0 - d * QB
                bias = bias_ref[jnp.where(st == 0, 0, 1)] if jt == 0 else bias_ref[1]
                q = q_ref[pl.ds(row0, QB, stride=d), :]
                k = kbuf[pl.ds(ks, 2 * QB, stride=d), :].astype(BF16)
                v = vbuf[pl.ds(ks, 2 * QB, stride=d), :].astype(BF16)
                v = jnp.concatenate([v, jnp.ones_like(v)], axis=1)
                q2 = jnp.concatenate([jnp.where(lo, q, 0.0), jnp.where(lo, 0.0, q)], axis=0).astype(BF16)
                s = lax.dot_general(q2, k, nt_contract, preferred_element_type=F32)
                tiles.append((s, bias, v, row0))
            probs = []
            for s, bias, v, row0 in tiles:
                s = s + jnp.concatenate([bias, bias], axis=0)
                m = jnp.max(s, axis=-1, keepdims=True)
                pr = jnp.exp2(s - m)
                probs.append((pr.astype(BF16), m))
            for (s, bias, v, row0), (pr, m) in zip(tiles, probs):
                pv = jnp.dot(pr, v, preferred_element_type=F32)
                rows = pl.ds(row0, QB, stride=d)
                acc_ref[p, rows, :] = jnp.where(lo, pv[0:QB, 0:LANES], pv[QB:2 * QB, 0:LANES])
                m_ref[p, rows, :] = jnp.where(lo, m[0:QB], m[QB:2 * QB])
                l_ref[p, rows, :] = jnp.where(lo, pv[0:QB, LANES:2 * LANES], pv[QB:2 * QB, LANES:2 * LANES])
            side(gi, n_groups)

    for p, (_, d) in enumerate(DILATIONS):
        pl.when(phase == p)(functools.partial(pattern, p, d))

    @pl.when(phase == ATTN_PHASES - 1)
    def _():
        n_merge = SUPER // QB
        for ci in range(n_merge):
            rows = pl.ds(ci * QB, QB)
            m1, m2, m3 = m_ref[0, rows, :], m_ref[1, rows, :], m_ref[2, rows, :]
            mm = jnp.maximum(jnp.maximum(m1, m2), m3)
            w1, w2, w3 = jnp.exp2(m1 - mm), jnp.exp2(m2 - mm), jnp.exp2(m3 - mm)
            num = w1 * acc_ref[0, rows, :] + w2 * acc_ref[1, rows, :] + w3 * acc_ref[2, rows, :]
            den = w1 * l_ref[0, rows, :] + w2 * l_ref[1, rows, :] + w3 * l_ref[2, rows, :]
            o_ref[rows, :] = num / den
            if ci % (n_merge // n_groups) == 0:
                side(ci // (n_merge // n_groups), n_groups)


def _attn_bias():
    qi = np.arange(QB)[:, None]
    ki = np.arange(2 * QB)[None, :]
    dist = qi + QB - ki
    normal = (dist >= 0) & (dist <= N_BACK)
    first = normal & (ki >= QB)
    return jnp.asarray(np.where(np.stack([first, normal]), 0.0, NEG), F32)


SA_HEADS = 4
SA_WIDTH = SA_HEADS * HEAD_DIM
N_SA_IN = 7


RING = 3


def _stream_kernel(*refs, rider, rider_takes_side, batch_lo, n_rider_in, n_rider_out, n_prev, n_rider_scratch):
    kt_hbm, vt_hbm, qx_ref, kn_ref, vn_ref, w_ref, hm_ref = refs[:N_SA_IN]
    rider_in = refs[N_SA_IN:N_SA_IN + n_rider_in]
    n_in = N_SA_IN + n_rider_in + n_prev
    okt_ref, ovt_ref, o_ref = refs[n_in:n_in + 3]
    rider_out = refs[n_in + 3:n_in + 3 + n_rider_out]
    pad_ref, newt_ref, kring, vring, sems = refs[n_in + 3 + n_rider_out:n_in + 8 + n_rider_out]
    rider_scratch = refs[len(refs) - n_rider_scratch:] if n_rider_scratch else ()
    n_buf = kring.shape[2]
    t_new = kn_ref.shape[0]
    n_tiles = n_buf // LANES

    step = pl.program_id(0)
    n_steps = pl.num_programs(0)

    def block_copy(i, s):
        hbm, ring = ((kt_hbm, kring), (vt_hbm, vring))[i]
        src = hbm.at[batch_lo + s // SA_NBLK, pl.ds((s % SA_NBLK) * SA_WIDTH, SA_WIDTH), :]
        return pltpu.make_async_copy(src, ring.at[s % RING], sems.at[i, s % RING])

    @pl.when(step == 0)
    def _():
        for s in range(RING - 1):
            for i in range(2):
                block_copy(i, s).start()

    @pl.when(step + RING - 1 < n_steps)
    def _():
        for i in range(2):
            block_copy(i, step + RING - 1).start()

    for i in range(2):
        block_copy(i, step).wait()
    kt_ref = kring.at[step % RING]
    vt_ref = vring.at[step % RING]

    def sample_attention():
        for i, new_ref in enumerate((kn_ref, vn_ref)):
            pad_ref[...] = jnp.zeros_like(pad_ref)
            pad_ref[0:t_new, :] = new_ref[...]
            newt_ref[i] = pad_ref[...].T
        k_ext = jnp.concatenate([kt_ref[...], newt_ref[0]], axis=1)
        v_ext = jnp.concatenate([vt_ref[...], newt_ref[1]], axis=1)
        w = w_ref[...]
        s = jnp.dot(qx_ref[...], k_ext.astype(BF16), preferred_element_type=F32)
        s = jnp.where(w > 0.0, s, NEG)
        m = jnp.max(s, axis=-1, keepdims=True)
        e = w * jnp.exp2(s - m)
        l = jnp.sum(e, axis=-1, keepdims=True)
        pv = lax.dot_general(e.astype(BF16), v_ext.astype(BF16), (((1,), (1,)), ((), ())),
                             preferred_element_type=F32)
        pv = pv * hm_ref[...] / l
        o = pv[0:t_new]
        for h in range(1, SA_HEADS):
            o = o + pv[h * t_new:(h + 1) * t_new]
        o_ref[...] = o

    boundary = {}

    def shift(part, n_parts):
        per = n_tiles // n_parts
        first = part * per
        keep = lax.broadcasted_iota(jnp.int32, (1, LANES), 1) < LANES - t_new
        if part == 0:
            boundary.clear()
        for i, (src_ref, dst_ref) in enumerate(((kt_ref, okt_ref), (vt_ref, ovt_ref))):
            def rotated(j):
                if (i, j) in boundary:
                    return boundary.pop((i, j))
                t = src_ref[:, j * LANES:(j + 1) * LANES] if j < n_tiles else newt_ref[i]
                return pltpu.roll(t, LANES - t_new, axis=1)
            rolled = [rotated(j) for j in range(first, first + per + 1)]
            for j in range(per):
                dst_ref[:, (first + j) * LANES:(first + j + 1) * LANES] = jnp.where(keep, rolled[j], rolled[j + 1])
            if part + 1 < n_parts:
                boundary[(i, first + per)] = rolled[per]

    sample_attention()
    if rider_takes_side:
        rider(*rider_in, *rider_out, *rider_scratch, side=shift)
    else:
        shift(0, 1)
        rider(*rider_in, *rider_out, *rider_scratch)


def _sa_weights(t_new, n_buf):
    ext = n_buf + LANES
    pos = np.arange(ext)[None, :]
    i = np.arange(t_new)[:, None]
    delta = n_buf + i - pos
    is_real = (pos < n_buf + t_new)
    w = np.zeros((t_new, ext), np.float32)
    for window, d in DILATIONS:
        w += ((delta >= 0) & (delta % d == 0) & (delta <= window) & is_real)
    return jnp.asarray(np.tile(w, (SA_HEADS, 1)), F32)


def _head_mask(t_new):
    h = np.repeat(np.arange(SA_HEADS), t_new)[:, None]
    c = np.arange(SA_WIDTH)[None, :] // HEAD_DIM
    return (h == c).astype(np.float32)


SA_NBLK = ATT_WIDTH // SA_WIDTH


def _stream(kt, vt, prev, batch_lo, n_batches, qx, k_new, v_new, rider, rider_args, rider_in_specs,
            rider_out_specs, rider_out_shapes, name, rider_scratch=(), rider_takes_side=False):
    bd, _, n_buf = kt.shape
    t_new = k_new.shape[1]
    assert n_buf == MAX_SPAN, "window buffer must cover the longest pattern (no invalid positions)"
    rows = t_new * SA_HEADS
    buf = pl.BlockSpec((None, SA_WIDTH, n_buf), lambda i: (batch_lo + i // SA_NBLK, i % SA_NBLK, 0))
    new = pl.BlockSpec((None, t_new, SA_WIDTH), lambda i: (batch_lo + i // SA_NBLK, 0, i % SA_NBLK))
    o_blk = pl.BlockSpec((None, t_new, SA_WIDTH), lambda i: (i // SA_NBLK, 0, i % SA_NBLK))
    qx_blk = pl.BlockSpec((None, None, rows, SA_WIDTH), lambda i: (batch_lo + i // SA_NBLK, i % SA_NBLK, 0, 0))
    sd_buf = jax.ShapeDtypeStruct((bd, ATT_WIDTH, n_buf), F32)
    hbm = pl.BlockSpec(memory_space=pl.ANY)
    in_specs = [hbm, hbm, qx_blk, new, new, _const_spec((rows, n_buf + LANES)), _const_spec((rows, SA_WIDTH))]
    in_specs += list(rider_in_specs)
    args = [kt, vt, qx, k_new, v_new, _sa_weights(t_new, n_buf), jnp.asarray(_head_mask(t_new), F32)]
    args += list(rider_args)
    aliases = {}
    if prev is not None:
        aliases = {len(args): 0, len(args) + 1: 1}
        in_specs += [pl.BlockSpec(memory_space=pl.ANY)] * 2
        args += list(prev)
    return pl.pallas_call(
        functools.partial(_stream_kernel, rider=rider, rider_takes_side=rider_takes_side, batch_lo=batch_lo,
                          n_rider_in=len(rider_in_specs),
                          n_rider_out=len(rider_out_specs), n_prev=len(aliases),
                          n_rider_scratch=len(rider_scratch)),
        grid=(n_batches * SA_NBLK,),
        in_specs=in_specs,
        out_specs=[buf, buf, o_blk] + list(rider_out_specs),
        out_shape=[sd_buf, sd_buf, jax.ShapeDtypeStruct((n_batches, t_new, ATT_WIDTH), F32)] + list(rider_out_shapes),
        input_output_aliases=aliases,
        scratch_shapes=[pltpu.VMEM((LANES, SA_WIDTH), F32), pltpu.VMEM((2, SA_WIDTH, LANES), F32),
                        pltpu.VMEM((RING, SA_WIDTH, n_buf), F32), pltpu.VMEM((RING, SA_WIDTH, n_buf), F32),
                        pltpu.SemaphoreType.DMA((2, RING))] + list(rider_scratch),
        compiler_params=_cparams(("arbitrary",)),
        name=name,
    )(*args)


def _expand_queries(q):
    bd, t_new, _ = q.shape
    hm = jnp.asarray(_head_mask(t_new), F32)
    qb = q.reshape(bd, t_new, SA_NBLK, SA_WIDTH).transpose(0, 2, 1, 3)
    return (jnp.tile(qb, (1, 1, SA_HEADS, 1)) * hm).astype(BF16)


def _rider_tiling(b, s, steps):
    assert (b * s) % steps == 0
    tm = b * s // steps
    assert tm % LANES == 0 and s % tm == 0
    return tm, s // tm


def _prompt_mod_spec(per_seq, col):
    return pl.BlockSpec((None, 1, D_MODEL), lambda i: (i // per_seq, 0, col))


def _inproj_prompt(x, mod, g, w_bf, qg, kg):
    b, s, _ = x.shape
    tm = TOKEN_TILE
    per_seq = s // tm
    keep = min(MAX_SPAN, s)
    kt_first = per_seq - keep // tm
    tok = lambda w: pl.BlockSpec((tm, w), lambda i: (i, 0))
    out_t =pl.BlockSpec((None, ATT_WIDTH, tm), lambda i: (i // per_seq, 0, jnp.maximum(i % per_seq - kt_first, 0)))
    sd = jax.ShapeDtypeStruct((b * s, ATT_WIDTH), F32)
    sdt = jax.ShapeDtypeStruct((b, ATT_WIDTH, keep), F32)
    q, k, v, u, kt_p, vt_p = pl.pallas_call(
        functools.partial(_inproj_body, kt_rule=(per_seq, kt_first)),
        grid=(b * per_seq,),
        in_specs=[tok(D_MODEL), _prompt_mod_spec(per_seq, MOD_SCALE1), _prompt_mod_spec(per_seq, MOD_SHIFT1),
                  _const_spec((1, D_MODEL)), _const_spec((D_MODEL, 4 * ATT_WIDTH)),
                  _const_spec((1, ATT_WIDTH)), _const_spec((1, ATT_WIDTH))],
        out_specs=[tok(ATT_WIDTH)] * 4 + [out_t] * 2,
        out_shape=[sd] * 4 + [sdt] * 2,
        compiler_params=_cparams(("arbitrary",)),
        name="inproj_prompt",
    )(x.reshape(b * s, D_MODEL), mod, mod, g, w_bf, qg, kg)
    r3 = lambda a: a.reshape(b, s, ATT_WIDTH)
    return r3(q), r3(k), r3(v), r3(u), kt_p, vt_p


def _attn_stream_batches(b, s):
    n_super = s // SUPER
    units = b * (ATT_WIDTH // LANES) * n_super
    steps = units * ATTN_PHASES
    assert steps % SA_NBLK == 0
    return steps // SA_NBLK


def _stream_attn(kt, vt, batch_lo, n_batches, qx, k_new, v_new, q, k, v):
    b, s, _ = q.shape
    assert s % SUPER == 0
    n_super = s // SUPER
    n_hp = ATT_WIDTH // LANES
    assert n_batches == _attn_stream_batches(b, s)

    n_units = b * n_hp * n_super

    def unit_block(lead):
        def index(i):
            un = jnp.minimum((i + lead) // ATTN_PHASES, n_units - 1)
            return un // (n_hp * n_super), un % n_super, (un // n_super) % n_hp
        return pl.BlockSpec((None, SUPER, LANES), index)

    outs = _stream(
        kt, vt, None, batch_lo, n_batches, qx, k_new, v_new,
        functools.partial(_attn_rider, n_super=n_super),
        (q, k, v, _attn_bias()),
        [unit_block(1), unit_block(ATTN_PHASES - 1), unit_block(ATTN_PHASES - 1), _const_spec((2, QB, 2 * QB))],
        [unit_block(0)], [jax.ShapeDtypeStruct((b, s, ATT_WIDTH), F32)], "stream_attn",
        rider_scratch=[pltpu.VMEM((2 * SUPER, LANES), F32)] * 2 + [pltpu.VMEM((len(DILATIONS), SUPER, LANES), F32)] * 3,
        rider_takes_side=True)
    okt, ovt, o_s, o_att = outs
    return (okt, ovt), o_s, o_att


def _stream_post(kt, vt, prev, batch_lo, n_batches, qx, k_new, v_new, x, o_att, y_ssm, mod, consts, weights):
    b, s, _ = x.shape
    tm, per_seq = _rider_tiling(b, s, n_batches * SA_NBLK)
    tok = lambda w: pl.BlockSpec((tm, w), lambda i: (i, 0))
    mods = [_prompt_mod_spec(per_seq, c) for c in (MOD_GATE1, MOD_SCALE2, MOD_SHIFT2, MOD_GATE2)]
    okt, ovt, o_s, y = _stream(
        kt, vt, prev, batch_lo, n_batches, qx, k_new, v_new, _post_body,
        (x.reshape(b * s, D_MODEL), o_att.reshape(b * s, ATT_WIDTH), y_ssm.reshape(b * s, SSM_WIDTH),
         mod, mod, mod, mod, *consts, *weights),
        [tok(D_MODEL), tok(ATT_WIDTH), tok(SSM_WIDTH)] + mods
        + [_const_spec(c.shape) for c in consts] + [_const_spec(w.shape) for w in weights],
        [tok(D_MODEL)], [jax.ShapeDtypeStruct((b * s, D_MODEL), F32)], "stream_post")
    return okt, ovt, o_s, y.reshape(b, s, D_MODEL)


def _s5_tables(a_re, a_im, log_dt, b_re, b_im):
    g, n = a_re.shape
    dt = jnp.exp(log_dt)[:, None]
    x, y = dt * a_re, dt * a_im
    ks = jnp.arange(CHUNK + 1, dtype=F32)
    mag = jnp.exp(x[:, :, None] * ks)
    pw_re, pw_im = mag * jnp.cos(y[:, :, None] * ks), mag * jnp.sin(y[:, :, None] * ks)
    e_re = jnp.expm1(x) * jnp.cos(y) - 2.0 * jnp.sin(0.5 * y) ** 2
    e_im = jnp.exp(x) * jnp.sin(y)
    den = a_re * a_re + a_im * a_im
    f_re = (e_re * a_re + e_im * a_im) / den
    f_im = (e_im * a_re - e_re * a_im) / den
    bb_re = f_re[..., None] * b_re - f_im[..., None] * b_im
    bb_im = f_re[..., None] * b_im + f_im[..., None] * b_re
    return dict(pw_re=pw_re, pw_im=pw_im, bb_re=bb_re, bb_im=bb_im)


def _s5_chunk_operators(tb, c_re, c_im, d_skip):
    pw_re, pw_im, bb_re, bb_im = tb["pw_re"], tb["pw_im"], tb["bb_re"], tb["bb_im"]
    g, n, ch = bb_re.shape
    L = CHUNK
    hp = lax.Precision.HIGHEST
    ct_re, ct_im = c_re.transpose(0, 2, 1), c_im.transpose(0, 2, 1)
    pr, pi = pw_re[:, :, :, None], pw_im[:, :, :, None]
    ca_re = ct_re[:, :, None, :] * pr - ct_im[:, :, None, :] * pi
    ca_im = ct_re[:, :, None, :] * pi + ct_im[:, :, None, :] * pr
    a_cat = jnp.concatenate([ca_re[:, :, :L].reshape(g, n, L * ch), -ca_im[:, :, :L].reshape(g, n, L * ch)], axis=1)
    b_cat = jnp.concatenate([bb_re, bb_im], axis=1)
    base = jnp.einsum("gnd,gnx->gdx", b_cat, a_cat, precision=hp)
    skip = jnp.eye(ch, dtype=F32)[None] * d_skip.reshape(g, ch, 1)
    base = base + jnp.pad(skip, ((0, 0), (0, 0), (0, (L - 1) * ch)))
    toe = jnp.stack([jnp.pad(base, ((0, 0), (0, 0), (s * ch, 0)))[:, :, :L * ch] for s in range(L)], axis=1)
    toe = toe.reshape(g, L * ch, L * ch)
    rv_re = pw_re[:, :, L - 1::-1].transpose(0, 2, 1)[:, :, None, :]
    rv_im = pw_im[:, :, L - 1::-1].transpose(0, 2, 1)[:, :, None, :]
    bt_re, bt_im = bb_re.transpose(0, 2, 1)[:, None], bb_im.transpose(0, 2, 1)[:, None]
    ws_re = (rv_re * bt_re - rv_im * bt_im).reshape(g, L * ch, n)
    ws_im = (rv_re * bt_im + rv_im * bt_re).reshape(g, L * ch, n)
    vr = ca_re[:, :, 1:].reshape(g, n, L * ch)
    vi = (-ca_im[:, :, 1:]).reshape(g, n, L * ch)
    gp = g // 2
    z_w = jnp.zeros((gp, L * ch, n), F32)
    wr, wi = ws_re.reshape(gp, 2, L * ch, n), ws_im.reshape(gp, 2, L * ch, n)
    w_pair = jnp.concatenate([
        jnp.concatenate([wr[:, 0], z_w, wi[:, 0], z_w], axis=2),
        jnp.concatenate([z_w, wr[:, 1], z_w, wi[:, 1]], axis=2)], axis=1)
    z_v = jnp.zeros((gp, n, L * ch), F32)
    vr, vi = vr.reshape(gp, 2, n, L * ch), vi.reshape(gp, 2, n, L * ch)
    v_pair = jnp.concatenate([
        jnp.concatenate([vr[:, 0], z_v], axis=2), jnp.concatenate([z_v, vr[:, 1]], axis=2),
        jnp.concatenate([vi[:, 0], z_v], axis=2), jnp.concatenate([z_v, vi[:, 1]], axis=2)], axis=1)
    a16_re = pw_re[:, :, L].reshape(1, g * n)
    a16_im = pw_im[:, :, L].reshape(1, g * n)
    return toe.astype(BF16), w_pair.astype(BF16), v_pair.astype(BF16), a16_re, a16_im


GROUPS_PER_BLOCK = LANES // SSM_CH
REGROUP_ROWS = 32


def _granule_transpose(arrs):
    n = GROUPS_PER_BLOCK
    gran = lax.broadcasted_iota(jnp.int32, (1, LANES), 1) // SSM_CH
    skew = [a if t == 0 else pltpu.roll(a, t * SSM_CH, axis=1) for t, a in enumerate(arrs)]
    out = []
    for p in range(n):
        c = skew[(-p) % n]
        for g in range(1, n):
            c = jnp.where(gran == g, skew[(g - p) % n], c)
        out.append(c if p == 0 else pltpu.roll(c, LANES - p * SSM_CH, axis=1))
    return out


def _ssm_a_kernel(u_ref, w_ref, xg_ref, sre_ref, sim_ref, xs_ref):
    nb, s_len, _ = u_ref.shape
    n_chunks = s_len // CHUNK
    half = CHUNK // 2

    for b in range(nb):
        def regroup(ct, c, b=b):
            c0 = pl.multiple_of(ct * REGROUP_ROWS, REGROUP_ROWS)
            rows = pl.ds(b * n_chunks + c0, REGROUP_ROWS)
            for j in range(2):
                z = [u_ref[b, pl.ds(c0 * CHUNK + half * j + tt, REGROUP_ROWS, stride=CHUNK), :]
                     for tt in range(half)]
                for p, xp in enumerate(_granule_transpose(z)):
                    xs_ref[2 * p + j, rows, :] = xp
            return c
        lax.fori_loop(0, n_chunks // REGROUP_ROWS, regroup, 0)

    for col in range(2 * GROUPS_PER_BLOCK):
        xg_ref[:, col * LANES:(col + 1) * LANES] = xs_ref[col].astype(BF16)
    for pp in range(GROUPS_PER_BLOCK // 2):
        xp = xg_ref[:, pp * 4 * LANES:(pp + 1) * 4 * LANES]
        st = jnp.dot(xp, w_ref[pp], preferred_element_type=F32)
        sre_ref[pp] = st[:, 0:LANES]
        sim_ref[pp] = st[:, LANES:2 * LANES]


def _ssm_scan_kernel(sre_ref, sim_ref, are_ref, aim_ref, hre_ref, him_ref, fre_ref, fim_ref, *, nb):
    ncb, rows, _ = sre_ref.shape
    n_chunks = rows // nb
    a_re = [are_ref[cb] for cb in range(ncb)]
    a_im = [aim_ref[cb] for cb in range(ncb)]

    def body(c, carry):
        r = pl.ds(c, nb, stride=n_chunks)
        out = []
        for cb in range(ncb):
            h_re, h_im = carry[2 * cb], carry[2 * cb + 1]
            hre_ref[cb, r, :] = h_re
            him_ref[cb, r, :] = h_im
            out.append(a_re[cb] * h_re - a_im[cb] * h_im + sre_ref[cb, r, :])
            out.append(a_re[cb] * h_im + a_im[cb] * h_re + sim_ref[cb, r, :])
        return tuple(out)

    z = jnp.zeros((nb, LANES), F32)
    fin = lax.fori_loop(0, n_chunks, body, (z,) * (2 * ncb), unroll=4)
    for cb in range(ncb):
        fre_ref[cb] = fin[2 * cb]
        fim_ref[cb] = fin[2 * cb + 1]


def _ssm_c_kernel(xg_ref, hre_ref, him_ref, t_ref, v_ref, y_ref, ys_ref):
    nb, s_len, _ = y_ref.shape
    n_chunks = s_len // CHUNK
    half = CHUNK // 2
    for pp in range(GROUPS_PER_BLOCK // 2):
        hcat = jnp.concatenate([hre_ref[pp], him_ref[pp]], axis=1).astype(BF16)
        inter = jnp.dot(hcat, v_ref[pp], preferred_element_type=F32)
        for e in range(2):
            g = 2 * pp + e
            intra = jnp.dot(xg_ref[:, g * 2 * LANES:(g + 1) * 2 * LANES], t_ref[g],
                            preferred_element_type=F32)
            yg = intra + inter[:, e * 2 * LANES:(e + 1) * 2 * LANES]
            ys_ref[2 * g] = yg[:, 0:LANES]
            ys_ref[2 * g + 1] = yg[:, LANES:2 * LANES]

    for b in range(nb):
        def regroup(ct, c, b=b):
            c0 = pl.multiple_of(ct * REGROUP_ROWS, REGROUP_ROWS)
            rows = pl.ds(b * n_chunks + c0, REGROUP_ROWS)
            for j in range(2):
                yp =[ys_ref[2 * p + j, rows, :] for p in range(GROUPS_PER_BLOCK)]
                for tt, zt in enumerate(_granule_transpose(yp)):
                    y_ref[b, pl.ds(c0 * CHUNK + half * j + tt, REGROUP_ROWS, stride=CHUNK), :] = zt
            return c
        lax.fori_loop(0, n_chunks // REGROUP_ROWS, regroup, 0)


def _ssm_prompt(u, toe, w_pair, v_pair, a16_re, a16_im):
    b, s, _ = u.shape
    rows = b * (s // CHUNK)
    nblk = SSM_WIDTH // LANES
    gw = GROUPS_PER_BLOCK * CHUNK * SSM_CH
    ncb = GROUPS_PER_BLOCK * SSM_STATE // LANES
    ublk = pl.BlockSpec((b, s, LANES), lambda j: (0, 0, j))
    sblk = pl.BlockSpec((ncb, rows, LANES), lambda j: (j, 0, 0))
    sd_h = jax.ShapeDtypeStruct((nblk * ncb, rows, LANES), F32)
    xg, s_re, s_im = pl.pallas_call(
        _ssm_a_kernel,
        grid=(nblk,),
        in_specs=[ublk, pl.BlockSpec((GROUPS_PER_BLOCK // 2, 4 * LANES, 2 * LANES), lambda j: (j, 0, 0))],
        out_specs=[pl.BlockSpec((rows, gw), lambda j: (0, j)), sblk, sblk],
        out_shape=[jax.ShapeDtypeStruct((rows, nblk * gw), BF16), sd_h, sd_h],
        scratch_shapes=[pltpu.VMEM((gw // LANES, rows, LANES), F32)],
        compiler_params=_cparams(("arbitrary",)),
        name="ssm_chunk_states",
    )(u, w_pair)

    ablk = pl.BlockSpec((ncb, 1, LANES), lambda j: (j, 0, 0))
    fblk = pl.BlockSpec((ncb, b, LANES), lambda j: (j, 0, 0))
    sd_f = jax.ShapeDtypeStruct((nblk * ncb, b, LANES), F32)
    a16_re = a16_re.reshape(nblk * ncb, 1, LANES)
    a16_im = a16_im.reshape(nblk * ncb, 1, LANES)
    h_re, h_im, f_re, f_im = pl.pallas_call(
        functools.partial(_ssm_scan_kernel, nb=b),
        grid=(nblk,),
        in_specs=[sblk, sblk, ablk, ablk],
        out_specs=[sblk, sblk, fblk, fblk],
        out_shape=[sd_h, sd_h, sd_f, sd_f],
        compiler_params=_cparams(("arbitrary",)),
        name="ssm_scan",
    )(s_re, s_im, a16_re, a16_im)

    y = pl.pallas_call(
        _ssm_c_kernel,
        grid=(nblk,),
        in_specs=[pl.BlockSpec((rows, gw), lambda j: (0, j)), sblk, sblk,
                  pl.BlockSpec((GROUPS_PER_BLOCK, 2 * LANES, 2 * LANES), lambda j: (j, 0, 0)),
                  pl.BlockSpec((GROUPS_PER_BLOCK // 2, 2 * LANES, 4 * LANES), lambda j: (j, 0, 0))],
        out_specs=ublk,
        out_shape=jax.ShapeDtypeStruct((b, s, SSM_WIDTH), F32),
        scratch_shapes=[pltpu.VMEM((gw // LANES, rows, LANES), F32)],
        compiler_params=_cparams(("arbitrary",)),
        name="ssm_outputs",
    )(xg, h_re, h_im, toe, v_pair)
    to_rows = lambda f: jnp.transpose(f, (1, 0, 2)).reshape(b, nblk * ncb * LANES)
    return y, to_rows(f_re), to_rows(f_im)


def _ssm_sample_kernel(u_ref, hre_ref, him_ref, are_ref, aim_ref, bre_ref, bim_ref, cre_ref, cim_ref, d_ref,
                       y_ref, ore_ref, oim_ref, *, t_new):
    h_re = hre_ref[...].T
    h_im = him_ref[...].T
    a_re, a_im = are_ref[...], aim_ref[...]
    bd = h_re.shape[0]
    for t in range(t_new):
        u = u_ref[t]
        ub = u.astype(BF16)
        n_re = a_re * h_re - a_im * h_im + jnp.dot(ub, bre_ref[...], preferred_element_type=F32)
        n_im = a_re * h_im + a_im * h_re + jnp.dot(ub, bim_ref[...], preferred_element_type=F32)
        h_re, h_im = n_re, n_im
        y = (jnp.dot(h_re.astype(BF16), cre_ref[...], preferred_element_type=F32)
             + jnp.dot(h_im.astype(BF16), cim_ref[...], preferred_element_type=F32) + d_ref[...] * u)
        y_ref[t] = y
    ore_ref[...] = h_re.T
    oim_ref[...] = h_im.T


def _block_diag(m):
    g, r, c = m.shape
    spread = jnp.asarray(np.tile(np.eye(c, dtype=np.float32), (1, g)), BF16)
    tiled = jnp.dot(m.reshape(g * r, c).astype(BF16), spread, preferred_element_type=F32)
    keep = np.arange(g * r)[:, None] // r == np.arange(g * c)[None, :] // c
    return jnp.where(jnp.asarray(keep), tiled, 0.0).astype(BF16)


def _ssm_sample(u, h0_re_t, h0_im_t, tb, c_re, c_im, d_skip, t_new):
    gn = SSM_GROUPS * SSM_STATE
    a_re = tb["pw_re"][:, :, 1].reshape(1, gn)
    a_im = tb["pw_im"][:, :, 1].reshape(1, gn)
    b_re = _block_diag(tb["bb_re"].transpose(0, 2, 1))
    b_im = _block_diag(tb["bb_im"].transpose(0, 2, 1))
    cb_re = _block_diag(c_re.transpose(0, 2, 1))
    cb_im = _block_diag(-c_im.transpose(0, 2, 1))
    bd = u.shape[1]
    full = lambda shape: pl.BlockSpec(shape, lambda i: (0,) * len(shape))
    return pl.pallas_call(
        functools.partial(_ssm_sample_kernel, t_new=t_new),
        grid=(1,),
        in_specs=[full((t_new, bd, SSM_WIDTH)), full((gn, bd)), full((gn, bd)), full((1, gn)), full((1, gn)),
                  full((SSM_WIDTH, gn)), full((SSM_WIDTH, gn)), full((gn, SSM_WIDTH)), full((gn, SSM_WIDTH)),
                  full((1, SSM_WIDTH))],
        out_specs=[full((t_new, bd, SSM_WIDTH)), full((gn, bd)), full((gn, bd))],
        out_shape=[jax.ShapeDtypeStruct((t_new, bd, SSM_WIDTH), F32),
                   jax.ShapeDtypeStruct((gn, bd), F32), jax.ShapeDtypeStruct((gn, bd), F32)],
        compiler_params=_cparams(("arbitrary",)),
        name="ssm_sample",
    )(u, h0_re_t, h0_im_t, a_re, a_im, b_re, b_im, cb_re, cb_im, d_skip.reshape(1, SSM_WIDTH))


def _rms(x, gain):
    return x * lax.rsqrt(jnp.mean(x * x, axis=-1, keepdims=True) + EPS) * gain


def _gelu_tanh(x):
    return 0.5 * x * (1.0 + jnp.tanh(math.sqrt(2.0 / math.pi) * (x + 0.044715 * (x * x * x))))


def _post_body(x_ref, oa_ref, ys_ref, g1_ref, sc2_ref, sh2_ref, g2_ref, n2_ref, ag_ref, sg_ref,
               wglu_ref, wout_ref, wg_ref, wu_ref, wd_ref, o_ref):
    ya = _gelu_tanh(ys_ref[...])
    ya = ya * _sigmoid(jnp.dot(ya.astype(BF16), wglu_ref[...], preferred_element_type=F32))
    merged = jnp.concatenate([_rms(oa_ref[...], ag_ref[...]), _rms(ya, sg_ref[...])], axis=1)
    x1 = x_ref[...] + g1_ref[...] * jnp.dot(merged.astype(BF16), wout_ref[...], preferred_element_type=F32)
    h2 = (_rms(x1, n2_ref[...]) * (1.0 + sc2_ref[...]) + sh2_ref[...]).astype(BF16)
    gate = jnp.dot(h2, wg_ref[...], preferred_element_type=F32)
    up = jnp.dot(h2, wu_ref[...], preferred_element_type=F32)
    act = (gate * _sigmoid(gate) * up).astype(BF16)
    o_ref[...] = x1 + g2_ref[...] * jnp.dot(act, wd_ref[...], preferred_element_type=F32)


def _post_sample(x, o_att, y_ssm, mod, bd, consts, weights):
    n = x.shape[0]
    tok = lambda w: pl.BlockSpec((bd, w), lambda t: (t, 0))
    mods = [_sample_mod_spec(bd, c) for c in (MOD_GATE1, MOD_SCALE2, MOD_SHIFT2, MOD_GATE2)]
    return pl.pallas_call(
        _post_body,
        grid=(n // bd,),
        in_specs=[tok(D_MODEL), tok(ATT_WIDTH), tok(SSM_WIDTH)] + mods
                 + [_const_spec(c.shape) for c in consts] + [_const_spec(w.shape) for w in weights],
        out_specs=tok(D_MODEL),
        out_shape=jax.ShapeDtypeStruct((n, D_MODEL), F32),
        compiler_params=_cparams(("arbitrary",)),
        name="post_sample",
    )(x, o_att, y_ssm, mod, mod, mod, mod, *consts, *weights)


def kernel(x_prompt, x_sample, cache_k, cache_v, state_ssm_re, state_ssm_im, c_prompt, c_sample, norm1_g, norm2_g, w_ada, b_ada, w_in, q_gain, k_gain, ssm_a_re, ssm_a_im, ssm_log_dt, ssm_b_re, ssm_b_im, ssm_c_re, ssm_c_im, ssm_d, w_glu, attn_out_g, ssm_out_g, w_out, w_gate, w_up, w_down):
    depth = norm1_g.shape[0]
    assert depth == 1, "one decoder layer"
    b, s, _ = x_prompt.shape
    bd, t_new, _ = x_sample.shape
    n_buf = cache_k.shape[2]
    L = 0

    assert bd % SUBLANES == 0
    pad = (-(b + bd)) % SUBLANES
    c_all = jnp.concatenate([c_sample, c_prompt, jnp.zeros((pad, D_MODEL), F32)], axis=0)
    mod = _ada(c_all, w_ada[L], b_ada[L].reshape(1, -1))
    mod_p = mod[bd:bd + b].reshape(b, 1, N_MOD * D_MODEL)

    w_in_bf = w_in[L].astype(BF16)
    n1 = norm1_g[L].reshape(1, D_MODEL)
    qg = jnp.tile(q_gain[L], ATT_HEADS).reshape(1, ATT_WIDTH)
    kg = jnp.tile(k_gain[L], ATT_HEADS).reshape(1, ATT_WIDTH)
    consts = (norm2_g[L].reshape(1, D_MODEL), attn_out_g[L].reshape(1, ATT_WIDTH), ssm_out_g[L].reshape(1, SSM_WIDTH))
    weights = tuple(w[L].astype(BF16) for w in (w_glu, w_out, w_gate, w_up, w_down))

    tb = _s5_tables(ssm_a_re[L], ssm_a_im[L], ssm_log_dt[L], ssm_b_re[L], ssm_b_im[L])
    toe, w_pair, v_pair, a16_re, a16_im = _s5_chunk_operators(tb, ssm_c_re[L], ssm_c_im[L], ssm_d[L])

    xs = jnp.transpose(x_sample, (1, 0, 2)).reshape(t_new * bd, D_MODEL)
    qs, ks, vs, us = _inproj_sample(xs, mod, bd, n1, w_in_bf, qg, kg)
    batch_major = lambda a: jnp.transpose(a.reshape(t_new, bd, ATT_WIDTH), (1, 0, 2))
    qx = _expand_queries(batch_major(qs))
    ks3, vs3 = batch_major(ks), batch_major(vs)
    kt = jnp.transpose(cache_k[L], (0, 2, 3, 1)).reshape(bd, ATT_WIDTH, n_buf)
    vt = jnp.transpose(cache_v[L], (0, 2, 3, 1)).reshape(bd, ATT_WIDTH, n_buf)
    q, k, v, u, kt_p, vt_p = _inproj_prompt(x_prompt, mod_p, n1, w_in_bf, qg, kg)
    n_first = _attn_stream_batches(b, s)
    assert 0 < n_first < bd
    shifted, o_s_first, o_att = _stream_attn(kt, vt, 0, n_first, qx, ks3, vs3, q, k, v)
    y_ssm, f_re, f_im = _ssm_prompt(u, toe, w_pair, v_pair, a16_re, a16_im)
    okt, ovt, o_s_second, y_prompt = _stream_post(
        kt, vt, shifted, n_first, bd - n_first, qx, ks3, vs3, x_prompt, o_att, y_ssm, mod_p, consts, weights)
    o_att_s = jnp.concatenate([o_s_first, o_s_second], axis=0)
    gn = SSM_GROUPS * SSM_STATE
    h0_re = jnp.transpose(state_ssm_re[L], (1, 2, 0)).reshape(gn, bd)
    h0_im = jnp.transpose(state_ssm_im[L], (1, 2, 0)).reshape(gn, bd)
    y_ssm_t, hs_re, hs_im = _ssm_sample(us.reshape(t_new, bd, SSM_WIDTH), h0_re, h0_im, tb,
                                        ssm_c_re[L], ssm_c_im[L], ssm_d[L], t_new)
    o_att_t = jnp.transpose(o_att_s, (1, 0, 2)).reshape(t_new * bd, ATT_WIDTH)
    y_sample = _post_sample(xs, o_att_t, y_ssm_t.reshape(t_new * bd, SSM_WIDTH), mod, bd, consts, weights)
    y_sample = jnp.transpose(y_sample.reshape(t_new, bd, D_MODEL), (1, 0, 2))

    def from_t(a, nb, keep):
        return jnp.transpose(a.reshape(nb, ATT_HEADS, HEAD_DIM, keep), (0, 3, 1, 2))[None]

    def state_from_t(a):
        return jnp.transpose(a.reshape(SSM_GROUPS, SSM_STATE, bd), (2, 0, 1))[None]

    keep = min(MAX_SPAN, s)
    return (y_prompt, y_sample,
            from_t(kt_p, b, keep), from_t(vt_p, b, keep),
            f_re.reshape(b, SSM_GROUPS, SSM_STATE)[None], f_im.reshape(b, SSM_GROUPS, SSM_STATE)[None],
            from_t(okt, bd, n_buf), from_t(ovt, bd, n_buf),
            state_from_t(hs_re), state_from_t(hs_im))
```

```python
import functools
import math

import jax
import jax.numpy as jnp
import numpy as np
from jax import lax
from jax.experimental import pallas as pl
from jax.experimental.pallas import tpu as pltpu

F32 = jnp.float32
BF16 = jnp.bfloat16

D_MODEL = 1024
HEAD_DIM = 64
ATT_WIDTH = 512
ATT_HEADS = 8
SSM_WIDTH = 512
SSM_CH = 16
SSM_GROUPS = 32
SSM_STATE = 64
DILATIONS = ((128, 1), (512, 4), (2048, 16))
N_BACK = 128
MAX_SPAN = 2048
FFN_HIDDEN = 2816
N_MOD = 6
MOD_SHIFT1, MOD_SCALE1, MOD_GATE1, MOD_SHIFT2, MOD_SCALE2, MOD_GATE2 = range(N_MOD)
EPS = 1e-6

LANES = 128
SUBLANES = 8
VMEM_LIMIT = 56 * 1024 * 1024

TOKEN_TILE = 512
SUPER = 2048
QB = 128
TILE_GROUP = 4
Q_SCALE = HEAD_DIM ** -0.5 * math.log2(math.e)
CHUNK = 16
NEG = -1e30


def _cparams(sem=None):
    return pltpu.CompilerParams(dimension_semantics=sem, vmem_limit_bytes=VMEM_LIMIT)


def _const_spec(shape):
    nd = len(shape)
    return pl.BlockSpec(shape, lambda *_: (0,) * nd, pipeline_mode=pl.Buffered(1))


def _sigmoid(x):
    return 1.0 / (1.0 + jnp.exp(-x))


def _split_bf16(a):
    hi = a.astype(BF16)
    lo = (a - hi.astype(F32)).astype(BF16)
    return hi, lo


def _ada_kernel(c_ref, w_ref, b_ref, o_ref):
    c = c_ref[...]
    a = c * _sigmoid(c)
    a_hi, a_lo = _split_bf16(a)
    w_hi, w_lo = _split_bf16(w_ref[...])
    acc = jnp.dot(a_hi, w_hi, preferred_element_type=F32)
    acc += jnp.dot(a_hi, w_lo, preferred_element_type=F32)
    acc += jnp.dot(a_lo, w_hi, preferred_element_type=F32)
    o_ref[...] = acc + b_ref[...]


def _ada(c_all, w_ada, b_ada):
    rows = c_all.shape[0]
    n = w_ada.shape[1]
    tn = 1024
    return pl.pallas_call(
        _ada_kernel,
        grid=(n // tn,),
        in_specs=[pl.BlockSpec((rows, D_MODEL), lambda j: (0, 0)),
                  pl.BlockSpec((D_MODEL, tn), lambda j: (0, j)),
                  pl.BlockSpec((1, tn), lambda j: (0, j))],
        out_specs=pl.BlockSpec((rows, tn), lambda j: (0, j)),
        out_shape=jax.ShapeDtypeStruct((rows, n), F32),
        compiler_params=_cparams(("arbitrary",)),
        name="ada",
    )(c_all, w_ada, b_ada)


def _head_rms(z, gain):
    lane = lax.broadcasted_iota(jnp.int32, (1, LANES), 1)
    lo = lane < HEAD_DIM
    outs = []
    for c in range(z.shape[1] // LANES):
        blk = z[:, c * LANES:(c + 1) * LANES]
        sq = blk * blk
        s_lo = jnp.sum(jnp.where(lo, sq, 0.0), axis=-1, keepdims=True)
        s_hi = jnp.sum(jnp.where(lo, 0.0, sq), axis=-1, keepdims=True)
        inv = jnp.where(lo, lax.rsqrt(s_lo * (1.0 / HEAD_DIM) + EPS), lax.rsqrt(s_hi * (1.0 / HEAD_DIM) + EPS))
        outs.append(blk * inv)
    return jnp.concatenate(outs, axis=1) * gain


def _inproj_body(x_ref, scale_ref, shift_ref, g_ref, w_ref, qg_ref, kg_ref, *out_refs, kt_rule=None):
    q_ref, k_ref, v_ref, u_ref = out_refs[:4]
    x = x_ref[...]
    ms = jnp.mean(x * x, axis=-1, keepdims=True)
    h = x * lax.rsqrt(ms + EPS) * g_ref[...]
    h = h * (1.0 + scale_ref[...]) + shift_ref[...]
    z = jnp.dot(h.astype(BF16), w_ref[...], preferred_element_type=F32)
    q = _head_rms(z[:, 0:ATT_WIDTH], qg_ref[...]) * Q_SCALE
    k = _head_rms(z[:, ATT_WIDTH:2 * ATT_WIDTH], kg_ref[...])
    v = z[:, 2 * ATT_WIDTH:3 * ATT_WIDTH]
    q_ref[...] = q
    k_ref[...] = k
    v_ref[...] = v
    u_ref[...] = z[:, 3 * ATT_WIDTH:]
    if kt_rule is not None:
        per_seq, kt_first = kt_rule
        kt_ref, vt_ref = out_refs[4:]

        @pl.when(pl.program_id(0) % per_seq >= kt_first)
        def _():
            kt_ref[...] = k.T
            vt_ref[...] = v.T


def _sample_mod_spec(bd, col):
    return pl.BlockSpec((bd, D_MODEL), lambda t: (0, col))


def _inproj_sample(x, mod, bd, g, w_bf, qg, kg):
    n = x.shape[0]
    tok = pl.BlockSpec((bd, D_MODEL), lambda t: (t, 0))
    out_tok = pl.BlockSpec((bd, ATT_WIDTH), lambda t: (t, 0))
    sd = jax.ShapeDtypeStruct((n, ATT_WIDTH), F32)
    return pl.pallas_call(
        _inproj_body,
        grid=(n // bd,),
        in_specs=[tok, _sample_mod_spec(bd, MOD_SCALE1), _sample_mod_spec(bd, MOD_SHIFT1), _const_spec((1, D_MODEL)),
                  _const_spec((D_MODEL, 4 * ATT_WIDTH)), _const_spec((1, ATT_WIDTH)), _const_spec((1, ATT_WIDTH))],
        out_specs=[out_tok] * 4,
        out_shape=[sd] * 4,
        compiler_params=_cparams(("arbitrary",)),
        name="inproj_sample",
    )(x, mod, mod, g, w_bf, qg, kg)


ATTN_PHASES = len(DILATIONS) + 1


def _attn_rider(q_ref, kc_ref, vc_ref, bias_ref, o_ref, kbuf, vbuf, acc_ref, m_ref, l_ref, *, n_super, side):
    step = pl.program_id(0)
    phase = step % ATTN_PHASES
    st = (step // ATTN_PHASES) % n_super
    lane = lax.broadcasted_iota(jnp.int32, (1, LANES), 1)
    lo = lane < HEAD_DIM
    nt_contract = (((1,), (1,)), ((), ()))
    n_groups = SUPER // QB // TILE_GROUP

    def pattern(p, d):
        per_res = SUPER // d // QB
        if p == 0:
            for buf, cur_ref in ((kbuf, kc_ref), (vbuf, vc_ref)):
                @pl.when(st == 0)
                def _(buf=buf):
                    buf[0:SUPER, :] = jnp.zeros((SUPER, LANES), F32)

                @pl.when(st > 0)
                def _(buf=buf):
                    buf[0:SUPER, :] = buf[SUPER:2 * SUPER, :]

                buf[SUPER:2 * SUPER, :] = cur_ref[...]

        for gi in range(n_groups):
            tiles = []
            for u in range(TILE_GROUP):
                ti = gi * TILE_GROUP + u
                r = ti // per_res
                jt = ti % per_res
                row0 = r + d * QB * jt
                ks = SUPER + row0 - d * QB
                bias = bias_ref[jnp.where(st == 0, 0, 1)] if jt == 0 else bias_ref[1]
                q = q_ref[pl.ds(row0, QB, stride=d), :]
                k = kbuf[pl.ds(ks, 2 * QB, stride=d), :].astype(BF16)
                v = vbuf[pl.ds(ks, 2 * QB, stride=d), :].astype(BF16)
                v = jnp.concatenate([v, jnp.ones_like(v)], axis=1)
                q2 = jnp.concatenate([jnp.where(lo, q, 0.0), jnp.where(lo, 0.0, q)], axis=0).astype(BF16)
                s = lax.dot_general(q2, k, nt_contract, preferred_element_type=F32)
                tiles.append((s, bias, v, row0))
            probs = []
            for s, bias, v, row0 in tiles:
                s = s + jnp.concatenate([bias, bias], axis=0)
                m = jnp.max(s, axis=-1, keepdims=True)
                pr = jnp.exp2(s - m)
                probs.append((pr.astype(BF16), m))
            for (s, bias, v, row0), (pr, m) in zip(tiles, probs):
                pv = jnp.dot(pr, v, preferred_element_type=F32)
                rows = pl.ds(row0, QB, stride=d)
                acc_ref[p, rows, :] = jnp.where(lo, pv[0:QB, 0:LANES], pv[QB:2 * QB, 0:LANES])
                m_ref[p, rows, :] = jnp.where(lo, m[0:QB], m[QB:2 * QB])
                l_ref[p, rows, :] = jnp.where(lo, pv[0:QB, LANES:2 * LANES], pv[QB:2 * QB, LANES:2 * LANES])
            if gi < SIDE_PARTS:
                side(gi)

    for p, (_, d) in enumerate(DILATIONS):
        pl.when(phase == p)(functools.partial(pattern, p, d))

    @pl.when(phase == ATTN_PHASES - 1)
    def _():
        n_merge = SUPER // QB
        for ci in range(n_merge):
            rows = pl.ds(ci * QB, QB)
            m1, m2, m3 = m_ref[0, rows, :], m_ref[1, rows, :], m_ref[2, rows, :]
            mm = jnp.maximum(jnp.maximum(m1, m2), m3)
            w1, w2, w3 = jnp.exp2(m1 - mm), jnp.exp2(m2 - mm), jnp.exp2(m3 - mm)
            num = w1 * acc_ref[0, rows, :] + w2 * acc_ref[1, rows, :] + w3 * acc_ref[2, rows, :]
            den = w1 * l_ref[0, rows, :] + w2 * l_ref[1, rows, :] + w3 * l_ref[2, rows, :]
            o_ref[rows, :] = num / den
            if ci % (n_merge // SIDE_PARTS) == 0 and ci // (n_merge // SIDE_PARTS) < SIDE_PARTS:
                side(ci // (n_merge // SIDE_PARTS))


def _attn_bias():
    qi = np.arange(QB)[:, None]
    ki = np.arange(2 * QB)[None, :]
    dist = qi + QB - ki
    normal = (dist >= 0) & (dist <= N_BACK)
    first = normal & (ki >= QB)
    return jnp.asarray(np.where(np.stack([first, normal]), 0.0, NEG), F32)


SA_HEADS = 4
SA_WIDTH = SA_HEADS * HEAD_DIM
N_SA_IN = 7


RING = 3
SIDE_PARTS = 3


def _stream_kernel(*refs, rider, rider_takes_side, batch_lo, n_rider_in, n_rider_out, n_prev, n_rider_scratch):
    kt_hbm, vt_hbm, qx_ref, kn_ref, vn_ref, w_ref, hm_ref = refs[:N_SA_IN]
    rider_in = refs[N_SA_IN:N_SA_IN + n_rider_in]
    n_in = N_SA_IN + n_rider_in + n_prev
    okt_ref, ovt_ref, o_ref = refs[n_in:n_in + 3]
    rider_out = refs[n_in + 3:n_in + 3 + n_rider_out]
    pad_ref, newt_ref, kring, vring, sems = refs[n_in + 3 + n_rider_out:n_in + 8 + n_rider_out]
    rider_scratch = refs[len(refs) - n_rider_scratch:] if n_rider_scratch else ()
    n_buf = kring.shape[2]
    t_new = kn_ref.shape[0]
    n_tiles = n_buf // LANES

    step = pl.program_id(0)
    n_steps = pl.num_programs(0)

    def block_copy(i, s):
        hbm, ring = ((kt_hbm, kring), (vt_hbm, vring))[i]
        src = hbm.at[batch_lo + s // SA_NBLK, pl.ds((s % SA_NBLK) * SA_WIDTH, SA_WIDTH), :]
        return pltpu.make_async_copy(src, ring.at[s % RING], sems.at[i, s % RING])

    @pl.when(step == 0)
    def _():
        for s in range(RING - 1):
            for i in range(2):
                block_copy(i, s).start()

    @pl.when(step + RING - 1 < n_steps)
    def _():
        for i in range(2):
            block_copy(i, step + RING - 1).start()

    for i in range(2):
        block_copy(i, step).wait()
    kt_ref = kring.at[step % RING]
    vt_ref = vring.at[step % RING]

    def sample_attention():
        for i, new_ref in enumerate((kn_ref, vn_ref)):
            pad_ref[...] = jnp.zeros_like(pad_ref)
            pad_ref[0:t_new, :] = new_ref[...]
            newt_ref[i] = pad_ref[...].T
        k_ext = jnp.concatenate([kt_ref[...], newt_ref[0]], axis=1)
        v_ext = jnp.concatenate([vt_ref[...], newt_ref[1]], axis=1)
        w = w_ref[...]
        s = jnp.dot(qx_ref[...], k_ext.astype(BF16), preferred_element_type=F32)
        s = jnp.where(w > 0.0, s, NEG)
        m = jnp.max(s, axis=-1, keepdims=True)
        e = w * jnp.exp2(s - m)
        l = jnp.sum(e, axis=-1, keepdims=True)
        pv = lax.dot_general(e.astype(BF16), v_ext.astype(BF16), (((1,), (1,)), ((), ())),
                             preferred_element_type=F32)
        pv = pv * hm_ref[...] / l
        o = pv[0:t_new]
        for h in range(1, SA_HEADS):
            o = o + pv[h * t_new:(h + 1) * t_new]
        o_ref[...] = o

    boundary = {}

    def shift(part, n_parts, fresh):
        per = n_tiles // n_parts
        first = part * per
        keep = lax.broadcasted_iota(jnp.int32, (1, LANES), 1) < LANES - t_new
        if fresh:
            boundary.clear()
        for i, (src_ref, dst_ref) in enumerate(((kt_ref, okt_ref), (vt_ref, ovt_ref))):
            def rotated(j):
                if (i, j) in boundary:
                    return boundary.pop((i, j))
                t = src_ref[:, j * LANES:(j + 1) * LANES] if j < n_tiles else newt_ref[i]
                return pltpu.roll(t, LANES - t_new, axis=1)
            rolled = [rotated(j) for j in range(first, first + per + 1)]
            for j in range(per):
                dst_ref[:, (first + j) * LANES:(first + j + 1) * LANES] = jnp.where(keep, rolled[j], rolled[j + 1])
            if part + 1 < n_parts:
                boundary[(i, first + per)] = rolled[per]

    sample_attention()
    if rider_takes_side:
        shift(0, SIDE_PARTS + 1, fresh=True)
        rider(*rider_in, *rider_out, *rider_scratch,
              side=lambda j: shift(j + 1, SIDE_PARTS + 1, fresh=(j == 0)))
    else:
        shift(0, 1, fresh=True)
        rider(*rider_in, *rider_out, *rider_scratch)


def _sa_weights(t_new, n_buf):
    ext = n_buf + LANES
    pos = np.arange(ext)[None, :]
    i = np.arange(t_new)[:, None]
    delta = n_buf + i - pos
    is_real = (pos < n_buf + t_new)
    w = np.zeros((t_new, ext), np.float32)
    for window, d in DILATIONS:
        w += ((delta >= 0) & (delta % d == 0) & (delta <= window) & is_real)
    return jnp.asarray(np.tile(w, (SA_HEADS, 1)), F32)


def _head_mask(t_new):
    h = np.repeat(np.arange(SA_HEADS), t_new)[:, None]
    c = np.arange(SA_WIDTH)[None, :] // HEAD_DIM
    return (h == c).astype(np.float32)


SA_NBLK = ATT_WIDTH // SA_WIDTH


def _stream(kt, vt, prev, batch_lo, n_batches, qx, k_new, v_new, rider, rider_args, rider_in_specs,
            rider_out_specs, rider_out_shapes, name, rider_scratch=(), rider_takes_side=False):
    bd, _, n_buf = kt.shape
    t_new = k_new.shape[1]
    assert n_buf == MAX_SPAN, "window buffer must cover the longest pattern (no invalid positions)"
    rows = t_new * SA_HEADS
    buf = pl.BlockSpec((None, SA_WIDTH, n_buf), lambda i: (batch_lo + i // SA_NBLK, i % SA_NBLK, 0))
    new = pl.BlockSpec((None, t_new, SA_WIDTH), lambda i: (batch_lo + i // SA_NBLK, 0, i % SA_NBLK))
    o_blk = pl.BlockSpec((None, t_new, SA_WIDTH), lambda i: (i // SA_NBLK, 0, i % SA_NBLK))
    qx_blk = pl.BlockSpec((None, None, rows, SA_WIDTH), lambda i: (batch_lo + i // SA_NBLK, i % SA_NBLK, 0, 0))
    sd_buf = jax.ShapeDtypeStruct((bd, ATT_WIDTH, n_buf), F32)
    hbm = pl.BlockSpec(memory_space=pl.ANY)
    in_specs = [hbm, hbm, qx_blk, new, new, _const_spec((rows, n_buf + LANES)), _const_spec((rows, SA_WIDTH))]
    in_specs += list(rider_in_specs)
    args = [kt, vt, qx, k_new, v_new, _sa_weights(t_new, n_buf), jnp.asarray(_head_mask(t_new), F32)]
    args += list(rider_args)
    aliases = {}
    if prev is not None:
        aliases = {len(args): 0, len(args) + 1: 1}
        in_specs += [pl.BlockSpec(memory_space=pl.ANY)] * 2
        args += list(prev)
    return pl.pallas_call(
        functools.partial(_stream_kernel, rider=rider, rider_takes_side=rider_takes_side, batch_lo=batch_lo,
                          n_rider_in=len(rider_in_specs),
                          n_rider_out=len(rider_out_specs), n_prev=len(aliases),
                          n_rider_scratch=len(rider_scratch)),
        grid=(n_batches * SA_NBLK,),
        in_specs=in_specs,
        out_specs=[buf, buf, o_blk] + list(rider_out_specs),
        out_shape=[sd_buf, sd_buf, jax.ShapeDtypeStruct((n_batches, t_new, ATT_WIDTH), F32)] + list(rider_out_shapes),
        input_output_aliases=aliases,
        scratch_shapes=[pltpu.VMEM((LANES, SA_WIDTH), F32), pltpu.VMEM((2, SA_WIDTH, LANES), F32),
                        pltpu.VMEM((RING, SA_WIDTH, n_buf), F32), pltpu.VMEM((RING, SA_WIDTH, n_buf), F32),
                        pltpu.SemaphoreType.DMA((2, RING))] + list(rider_scratch),
        compiler_params=_cparams(("arbitrary",)),
        name=name,
    )(*args)


def _expand_queries(q):
    bd, t_new, _ = q.shape
    hm = jnp.asarray(_head_mask(t_new), F32)
    qb = q.reshape(bd, t_new, SA_NBLK, SA_WIDTH).transpose(0, 2, 1, 3)
    return (jnp.tile(qb, (1, 1, SA_HEADS, 1)) * hm).astype(BF16)


def _rider_tiling(b, s, steps):
    assert (b * s) % steps == 0
    tm = b * s // steps
    assert tm % LANES == 0 and s % tm == 0
    return tm, s // tm


def _prompt_mod_spec(per_seq, col):
    return pl.BlockSpec((None, 1, D_MODEL), lambda i: (i // per_seq, 0, col))


def _inproj_prompt(x, mod, g, w_bf, qg, kg):
    b, s, _ = x.shape
    tm = TOKEN_TILE
    per_seq = s // tm
    keep = min(MAX_SPAN, s)
    kt_first = per_seq - keep // tm
    tok = lambda w: pl.BlockSpec((tm, w), lambda i: (i, 0))
    out_t =pl.BlockSpec((None, ATT_WIDTH, tm), lambda i: (i // per_seq, 0, jnp.maximum(i % per_seq - kt_first, 0)))
    sd = jax.ShapeDtypeStruct((b * s, ATT_WIDTH), F32)
    sdt = jax.ShapeDtypeStruct((b, ATT_WIDTH, keep), F32)
    q, k, v, u, kt_p, vt_p = pl.pallas_call(
        functools.partial(_inproj_body, kt_rule=(per_seq, kt_first)),
        grid=(b * per_seq,),
        in_specs=[tok(D_MODEL), _prompt_mod_spec(per_seq, MOD_SCALE1), _prompt_mod_spec(per_seq, MOD_SHIFT1),
                  _const_spec((1, D_MODEL)), _const_spec((D_MODEL, 4 * ATT_WIDTH)),
                  _const_spec((1, ATT_WIDTH)), _const_spec((1, ATT_WIDTH))],
        out_specs=[tok(ATT_WIDTH)] * 4 + [out_t] * 2,
        out_shape=[sd] * 4 + [sdt] * 2,
        compiler_params=_cparams(("arbitrary",)),
        name="inproj_prompt",
    )(x.reshape(b * s, D_MODEL), mod, mod, g, w_bf, qg, kg)
    r3 = lambda a: a.reshape(b, s, ATT_WIDTH)
    return r3(q), r3(k), r3(v), r3(u), kt_p, vt_p


def _attn_stream_batches(b, s):
    n_super = s // SUPER
    units = b * (ATT_WIDTH // LANES) * n_super
    steps = units * ATTN_PHASES
    assert steps % SA_NBLK == 0
    return steps // SA_NBLK


def _stream_attn(kt, vt, batch_lo, n_batches, qx, k_new, v_new, q, k, v):
    b, s, _ = q.shape
    assert s % SUPER == 0
    n_super = s // SUPER
    n_hp = ATT_WIDTH // LANES
    assert n_batches == _attn_stream_batches(b, s)

    n_units = b * n_hp * n_super

    def unit_block(lead):
        def index(i):
            un = jnp.minimum((i + lead) // ATTN_PHASES, n_units - 1)
            return un // (n_hp * n_super), un % n_super, (un // n_super) % n_hp
        return pl.BlockSpec((None, SUPER, LANES), index)

    outs = _stream(
        kt, vt, None, batch_lo, n_batches, qx, k_new, v_new,
        functools.partial(_attn_rider, n_super=n_super),
        (q, k, v, _attn_bias()),
        [unit_block(1), unit_block(ATTN_PHASES - 1), unit_block(ATTN_PHASES - 1), _const_spec((2, QB, 2 * QB))],
        [unit_block(0)], [jax.ShapeDtypeStruct((b, s, ATT_WIDTH), F32)], "stream_attn",
        rider_scratch=[pltpu.VMEM((2 * SUPER, LANES), F32)] * 2 + [pltpu.VMEM((len(DILATIONS), SUPER, LANES), F32)] * 3,
        rider_takes_side=True)
    okt, ovt, o_s, o_att = outs
    return (okt, ovt), o_s, o_att


def _stream_post(kt, vt, prev, batch_lo, n_batches, qx, k_new, v_new, x, o_att, y_ssm, mod, consts, weights):
    b, s, _ = x.shape
    tm, per_seq = _rider_tiling(b, s, n_batches * SA_NBLK)
    tok = lambda w: pl.BlockSpec((tm, w), lambda i: (i, 0))
    mods = [_prompt_mod_spec(per_seq, c) for c in (MOD_GATE1, MOD_SCALE2, MOD_SHIFT2, MOD_GATE2)]
    okt, ovt, o_s, y = _stream(
        kt, vt, prev, batch_lo, n_batches, qx, k_new, v_new, _post_body,
        (x.reshape(b * s, D_MODEL), o_att.reshape(b * s, ATT_WIDTH), y_ssm.reshape(b * s, SSM_WIDTH),
         mod, mod, mod, mod, *consts, *weights),
        [tok(D_MODEL), tok(ATT_WIDTH), tok(SSM_WIDTH)] + mods
        + [_const_spec(c.shape) for c in consts] + [_const_spec(w.shape) for w in weights],
        [tok(D_MODEL)], [jax.ShapeDtypeStruct((b * s, D_MODEL), F32)], "stream_post")
    return okt, ovt, o_s, y.reshape(b, s, D_MODEL)


def _s5_tables(a_re, a_im, log_dt, b_re, b_im):
    g, n = a_re.shape
    dt = jnp.exp(log_dt)[:, None]
    x, y = dt * a_re, dt * a_im
    ks = jnp.arange(CHUNK + 1, dtype=F32)
    mag = jnp.exp(x[:, :, None] * ks)
    pw_re, pw_im = mag * jnp.cos(y[:, :, None] * ks), mag * jnp.sin(y[:, :, None] * ks)
    e_re = jnp.expm1(x) * jnp.cos(y) - 2.0 * jnp.sin(0.5 * y) ** 2
    e_im = jnp.exp(x) * jnp.sin(y)
    den = a_re * a_re + a_im * a_im
    f_re = (e_re * a_re + e_im * a_im) / den
    f_im = (e_im * a_re - e_re * a_im) / den
    bb_re = f_re[..., None] * b_re - f_im[..., None] * b_im
    bb_im = f_re[..., None] * b_im + f_im[..., None] * b_re
    return dict(pw_re=pw_re, pw_im=pw_im, bb_re=bb_re, bb_im=bb_im)


def _s5_chunk_operators(tb, c_re, c_im, d_skip):
    pw_re, pw_im, bb_re, bb_im = tb["pw_re"], tb["pw_im"], tb["bb_re"], tb["bb_im"]
    g, n, ch = bb_re.shape
    L = CHUNK
    hp = lax.Precision.HIGHEST
    ct_re, ct_im = c_re.transpose(0, 2, 1), c_im.transpose(0, 2, 1)
    pr, pi = pw_re[:, :, :, None], pw_im[:, :, :, None]
    ca_re = ct_re[:, :, None, :] * pr - ct_im[:, :, None, :] * pi
    ca_im = ct_re[:, :, None, :] * pi + ct_im[:, :, None, :] * pr
    a_cat = jnp.concatenate([ca_re[:, :, :L].reshape(g, n, L * ch), -ca_im[:, :, :L].reshape(g, n, L * ch)], axis=1)
    b_cat = jnp.concatenate([bb_re, bb_im], axis=1)
    base = jnp.einsum("gnd,gnx->gdx", b_cat, a_cat, precision=hp)
    skip = jnp.eye(ch, dtype=F32)[None] * d_skip.reshape(g, ch, 1)
    base = base + jnp.pad(skip, ((0, 0), (0, 0), (0, (L - 1) * ch)))
    toe = jnp.stack([jnp.pad(base, ((0, 0), (0, 0), (s * ch, 0)))[:, :, :L * ch] for s in range(L)], axis=1)
    toe = toe.reshape(g, L * ch, L * ch)
    rv_re = pw_re[:, :, L - 1::-1].transpose(0, 2, 1)[:, :, None, :]
    rv_im = pw_im[:, :, L - 1::-1].transpose(0, 2, 1)[:, :, None, :]
    bt_re, bt_im = bb_re.transpose(0, 2, 1)[:, None], bb_im.transpose(0, 2, 1)[:, None]
    ws_re = (rv_re * bt_re - rv_im * bt_im).reshape(g, L * ch, n)
    ws_im = (rv_re * bt_im + rv_im * bt_re).reshape(g, L * ch, n)
    vr = ca_re[:, :, 1:].reshape(g, n, L * ch)
    vi = (-ca_im[:, :, 1:]).reshape(g, n, L * ch)
    gp = g // 2
    z_w = jnp.zeros((gp, L * ch, n), F32)
    wr, wi = ws_re.reshape(gp, 2, L * ch, n), ws_im.reshape(gp, 2, L * ch, n)
    w_pair = jnp.concatenate([
        jnp.concatenate([wr[:, 0], z_w, wi[:, 0], z_w], axis=2),
        jnp.concatenate([z_w, wr[:, 1], z_w, wi[:, 1]], axis=2)], axis=1)
    z_v = jnp.zeros((gp, n, L * ch), F32)
    vr, vi = vr.reshape(gp, 2, n, L * ch), vi.reshape(gp, 2, n, L * ch)
    v_pair = jnp.concatenate([
        jnp.concatenate([vr[:, 0], z_v], axis=2), jnp.concatenate([z_v, vr[:, 1]], axis=2),
        jnp.concatenate([vi[:, 0], z_v], axis=2), jnp.concatenate([z_v, vi[:, 1]], axis=2)], axis=1)
    a16_re = pw_re[:, :, L].reshape(1, g * n)
    a16_im = pw_im[:, :, L].reshape(1, g * n)
    return toe.astype(BF16), w_pair.astype(BF16), v_pair.astype(BF16), a16_re, a16_im


GROUPS_PER_BLOCK = LANES // SSM_CH
REGROUP_ROWS = 32


def _granule_transpose(arrs):
    n = GROUPS_PER_BLOCK
    gran = lax.broadcasted_iota(jnp.int32, (1, LANES), 1) // SSM_CH
    skew = [a if t == 0 else pltpu.roll(a, t * SSM_CH, axis=1) for t, a in enumerate(arrs)]
    out = []
    for p in range(n):
        c = skew[(-p) % n]
        for g in range(1, n):
            c = jnp.where(gran == g, skew[(g - p) % n], c)
        out.append(c if p == 0 else pltpu.roll(c, LANES - p * SSM_CH, axis=1))
    return out


def _ssm_a_kernel(u_ref, w_ref, xg_ref, sre_ref, sim_ref, xs_ref):
    nb, s_len, _ = u_ref.shape
    n_chunks = s_len // CHUNK
    half = CHUNK // 2

    for b in range(nb):
        def regroup(ct, c, b=b):
            c0 = pl.multiple_of(ct * REGROUP_ROWS, REGROUP_ROWS)
            rows = pl.ds(b * n_chunks + c0, REGROUP_ROWS)
            for j in range(2):
                z = [u_ref[b, pl.ds(c0 * CHUNK + half * j + tt, REGROUP_ROWS, stride=CHUNK), :]
                     for tt in range(half)]
                for p, xp in enumerate(_granule_transpose(z)):
                    xs_ref[2 * p + j, rows, :] = xp
            return c
        lax.fori_loop(0, n_chunks // REGROUP_ROWS, regroup, 0)

    for col in range(2 * GROUPS_PER_BLOCK):
        xg_ref[:, col * LANES:(col + 1) * LANES] = xs_ref[col].astype(BF16)
    for pp in range(GROUPS_PER_BLOCK // 2):
        xp = xg_ref[:, pp * 4 * LANES:(pp + 1) * 4 * LANES]
        st = jnp.dot(xp, w_ref[pp], preferred_element_type=F32)
        sre_ref[pp] = st[:, 0:LANES]
        sim_ref[pp] = st[:, LANES:2 * LANES]


def _ssm_scan_kernel(sre_ref, sim_ref, are_ref, aim_ref, hre_ref, him_ref, fre_ref, fim_ref, *, nb):
    ncb, rows, _ = sre_ref.shape
    n_chunks = rows // nb
    a_re = [are_ref[cb] for cb in range(ncb)]
    a_im = [aim_ref[cb] for cb in range(ncb)]

    def body(c, carry):
        r = pl.ds(c, nb, stride=n_chunks)
        out = []
        for cb in range(ncb):
            h_re, h_im = carry[2 * cb], carry[2 * cb + 1]
            hre_ref[cb, r, :] = h_re
            him_ref[cb, r, :] = h_im
            out.append(a_re[cb] * h_re - a_im[cb] * h_im + sre_ref[cb, r, :])
            out.append(a_re[cb] * h_im + a_im[cb] * h_re + sim_ref[cb, r, :])
        return tuple(out)

    z = jnp.zeros((nb, LANES), F32)
    fin = lax.fori_loop(0, n_chunks, body, (z,) * (2 * ncb), unroll=4)
    for cb in range(ncb):
        fre_ref[cb] = fin[2 * cb]
        fim_ref[cb] = fin[2 * cb + 1]


def _ssm_c_kernel(xg_ref, hre_ref, him_ref, t_ref, v_ref, y_ref, ys_ref):
    nb, s_len, _ = y_ref.shape
    n_chunks = s_len // CHUNK
    half = CHUNK // 2
    for pp in range(GROUPS_PER_BLOCK // 2):
        hcat = jnp.concatenate([hre_ref[pp], him_ref[pp]], axis=1).astype(BF16)
        inter = jnp.dot(hcat, v_ref[pp], preferred_element_type=F32)
        for e in range(2):
            g = 2 * pp + e
            intra = jnp.dot(xg_ref[:, g * 2 * LANES:(g + 1) * 2 * LANES], t_ref[g],
                            preferred_element_type=F32)
            yg = intra + inter[:, e * 2 * LANES:(e + 1) * 2 * LANES]
            ys_ref[2 * g] = yg[:, 0:LANES]
            ys_ref[2 * g + 1] = yg[:, LANES:2 * LANES]

    for b in range(nb):
        def regroup(ct, c, b=b):
            c0 = pl.multiple_of(ct * REGROUP_ROWS, REGROUP_ROWS)
            rows = pl.ds(b * n_chunks + c0, REGROUP_ROWS)
            for j in range(2):
                yp =[ys_ref[2 * p + j, rows, :] for p in range(GROUPS_PER_BLOCK)]
                for tt, zt in enumerate(_granule_transpose(yp)):
                    y_ref[b, pl.ds(c0 * CHUNK + half * j + tt, REGROUP_ROWS, stride=CHUNK), :] = zt
            return c
        lax.fori_loop(0, n_chunks // REGROUP_ROWS, regroup, 0)


def _ssm_prompt(u, toe, w_pair, v_pair, a16_re, a16_im):
    b, s, _ = u.shape
    rows = b * (s // CHUNK)
    nblk = SSM_WIDTH // LANES
    gw = GROUPS_PER_BLOCK * CHUNK * SSM_CH
    ncb = GROUPS_PER_BLOCK * SSM_STATE // LANES
    ublk = pl.BlockSpec((b, s, LANES), lambda j: (0, 0, j))
    sblk = pl.BlockSpec((ncb, rows, LANES), lambda j: (j, 0, 0))
    sd_h = jax.ShapeDtypeStruct((nblk * ncb, rows, LANES), F32)
    xg, s_re, s_im = pl.pallas_call(
        _ssm_a_kernel,
        grid=(nblk,),
        in_specs=[ublk, pl.BlockSpec((GROUPS_PER_BLOCK // 2, 4 * LANES, 2 * LANES), lambda j: (j, 0, 0))],
        out_specs=[pl.BlockSpec((rows, gw), lambda j: (0, j)), sblk, sblk],
        out_shape=[jax.ShapeDtypeStruct((rows, nblk * gw), BF16), sd_h, sd_h],
        scratch_shapes=[pltpu.VMEM((gw // LANES, rows, LANES), F32)],
        compiler_params=_cparams(("arbitrary",)),
        name="ssm_chunk_states",
    )(u, w_pair)

    ablk = pl.BlockSpec((ncb, 1, LANES), lambda j: (j, 0, 0))
    fblk = pl.BlockSpec((ncb, b, LANES), lambda j: (j, 0, 0))
    sd_f = jax.ShapeDtypeStruct((nblk * ncb, b, LANES), F32)
    a16_re = a16_re.reshape(nblk * ncb, 1, LANES)
    a16_im = a16_im.reshape(nblk * ncb, 1, LANES)
    h_re, h_im, f_re, f_im = pl.pallas_call(
        functools.partial(_ssm_scan_kernel, nb=b),
        grid=(nblk,),
        in_specs=[sblk, sblk, ablk, ablk],
        out_specs=[sblk, sblk, fblk, fblk],
        out_shape=[sd_h, sd_h, sd_f, sd_f],
        compiler_params=_cparams(("arbitrary",)),
        name="ssm_scan",
    )(s_re, s_im, a16_re, a16_im)

    y = pl.pallas_call(
        _ssm_c_kernel,
        grid=(nblk,),
        in_specs=[pl.BlockSpec((rows, gw), lambda j: (0, j)), sblk, sblk,
                  pl.BlockSpec((GROUPS_PER_BLOCK, 2 * LANES, 2 * LANES), lambda j: (j, 0, 0)),
                  pl.BlockSpec((GROUPS_PER_BLOCK // 2, 2 * LANES, 4 * LANES), lambda j: (j, 0, 0))],
        out_specs=ublk,
        out_shape=jax.ShapeDtypeStruct((b, s, SSM_WIDTH), F32),
        scratch_shapes=[pltpu.VMEM((gw // LANES, rows, LANES), F32)],
        compiler_params=_cparams(("arbitrary",)),
        name="ssm_outputs",
    )(xg, h_re, h_im, toe, v_pair)
    to_rows = lambda f: jnp.transpose(f, (1, 0, 2)).reshape(b, nblk * ncb * LANES)
    return y, to_rows(f_re), to_rows(f_im)


def _ssm_sample_kernel(u_ref, hre_ref, him_ref, are_ref, aim_ref, bre_ref, bim_ref, cre_ref, cim_ref, d_ref,
                       y_ref, ore_ref, oim_ref, *, t_new):
    h_re = hre_ref[...].T
    h_im = him_ref[...].T
    a_re, a_im = are_ref[...], aim_ref[...]
    bd = h_re.shape[0]
    for t in range(t_new):
        u = u_ref[t]
        ub = u.astype(BF16)
        n_re = a_re * h_re - a_im * h_im + jnp.dot(ub, bre_ref[...], preferred_element_type=F32)
        n_im = a_re * h_im + a_im * h_re + jnp.dot(ub, bim_ref[...], preferred_element_type=F32)
        h_re, h_im = n_re, n_im
        y = (jnp.dot(h_re.astype(BF16), cre_ref[...], preferred_element_type=F32)
             + jnp.dot(h_im.astype(BF16), cim_ref[...], preferred_element_type=F32) + d_ref[...] * u)
        y_ref[t] = y
    ore_ref[...] = h_re.T
    oim_ref[...] = h_im.T


def _block_diag(m):
    g, r, c = m.shape
    spread = jnp.asarray(np.tile(np.eye(c, dtype=np.float32), (1, g)), BF16)
    tiled = jnp.dot(m.reshape(g * r, c).astype(BF16), spread, preferred_element_type=F32)
    keep = np.arange(g * r)[:, None] // r == np.arange(g * c)[None, :] // c
    return jnp.where(jnp.asarray(keep), tiled, 0.0).astype(BF16)


def _ssm_sample(u, h0_re_t, h0_im_t, tb, c_re, c_im, d_skip, t_new):
    gn = SSM_GROUPS * SSM_STATE
    a_re = tb["pw_re"][:, :, 1].reshape(1, gn)
    a_im = tb["pw_im"][:, :, 1].reshape(1, gn)
    b_re = _block_diag(tb["bb_re"].transpose(0, 2, 1))
    b_im = _block_diag(tb["bb_im"].transpose(0, 2, 1))
    cb_re = _block_diag(c_re.transpose(0, 2, 1))
    cb_im = _block_diag(-c_im.transpose(0, 2, 1))
    bd = u.shape[1]
    full = lambda shape: pl.BlockSpec(shape, lambda i: (0,) * len(shape))
    return pl.pallas_call(
        functools.partial(_ssm_sample_kernel, t_new=t_new),
        grid=(1,),
        in_specs=[full((t_new, bd, SSM_WIDTH)), full((gn, bd)), full((gn, bd)), full((1, gn)), full((1, gn)),
                  full((SSM_WIDTH, gn)), full((SSM_WIDTH, gn)), full((gn, SSM_WIDTH)), full((gn, SSM_WIDTH)),
                  full((1, SSM_WIDTH))],
        out_specs=[full((t_new, bd, SSM_WIDTH)), full((gn, bd)), full((gn, bd))],
        out_shape=[jax.ShapeDtypeStruct((t_new, bd, SSM_WIDTH), F32),
                   jax.ShapeDtypeStruct((gn, bd), F32), jax.ShapeDtypeStruct((gn, bd), F32)],
        compiler_params=_cparams(("arbitrary",)),
        name="ssm_sample",
    )(u, h0_re_t, h0_im_t, a_re, a_im, b_re, b_im, cb_re, cb_im, d_skip.reshape(1, SSM_WIDTH))


def _rms(x, gain):
    return x * lax.rsqrt(jnp.mean(x * x, axis=-1, keepdims=True) + EPS) * gain


def _gelu_tanh(x):
    return 0.5 * x * (1.0 + jnp.tanh(math.sqrt(2.0 / math.pi) * (x + 0.044715 * (x * x * x))))


def _post_body(x_ref, oa_ref, ys_ref, g1_ref, sc2_ref, sh2_ref, g2_ref, n2_ref, ag_ref, sg_ref,
               wglu_ref, wout_ref, wg_ref, wu_ref, wd_ref, o_ref):
    ya = _gelu_tanh(ys_ref[...])
    ya = ya * _sigmoid(jnp.dot(ya.astype(BF16), wglu_ref[...], preferred_element_type=F32))
    merged = jnp.concatenate([_rms(oa_ref[...], ag_ref[...]), _rms(ya, sg_ref[...])], axis=1)
    x1 = x_ref[...] + g1_ref[...] * jnp.dot(merged.astype(BF16), wout_ref[...], preferred_element_type=F32)
    h2 = (_rms(x1, n2_ref[...]) * (1.0 + sc2_ref[...]) + sh2_ref[...]).astype(BF16)
    gate = jnp.dot(h2, wg_ref[...], preferred_element_type=F32)
    up = jnp.dot(h2, wu_ref[...], preferred_element_type=F32)
    act = (gate * _sigmoid(gate) * up).astype(BF16)
    o_ref[...] = x1 + g2_ref[...] * jnp.dot(act, wd_ref[...], preferred_element_type=F32)


def _post_sample(x, o_att, y_ssm, mod, bd, consts, weights):
    n = x.shape[0]
    tok = lambda w: pl.BlockSpec((bd, w), lambda t: (t, 0))
    mods = [_sample_mod_spec(bd, c) for c in (MOD_GATE1, MOD_SCALE2, MOD_SHIFT2, MOD_GATE2)]
    return pl.pallas_call(
        _post_body,
        grid=(n // bd,),
        in_specs=[tok(D_MODEL), tok(ATT_WIDTH), tok(SSM_WIDTH)] + mods
                 + [_const_spec(c.shape) for c in consts] + [_const_spec(w.shape) for w in weights],
        out_specs=tok(D_MODEL),
        out_shape=jax.ShapeDtypeStruct((n, D_MODEL), F32),
        compiler_params=_cparams(("arbitrary",)),
        name="post_sample",
    )(x, o_att, y_ssm, mod, mod, mod, mod, *consts, *weights)


def kernel(x_prompt, x_sample, cache_k, cache_v, state_ssm_re, state_ssm_im, c_prompt, c_sample, norm1_g, norm2_g, w_ada, b_ada, w_in, q_gain, k_gain, ssm_a_re, ssm_a_im, ssm_log_dt, ssm_b_re, ssm_b_im, ssm_c_re, ssm_c_im, ssm_d, w_glu, attn_out_g, ssm_out_g, w_out, w_gate, w_up, w_down):
    depth = norm1_g.shape[0]
    assert depth == 1, "one decoder layer"
    b, s, _ = x_prompt.shape
    bd, t_new, _ = x_sample.shape
    n_buf = cache_k.shape[2]
    L = 0

    assert bd % SUBLANES == 0
    pad = (-(b + bd)) % SUBLANES
    c_all = jnp.concatenate([c_sample, c_prompt, jnp.zeros((pad, D_MODEL), F32)], axis=0)
    mod = _ada(c_all, w_ada[L], b_ada[L].reshape(1, -1))
    mod_p = mod[bd:bd + b].reshape(b, 1, N_MOD * D_MODEL)

    w_in_bf = w_in[L].astype(BF16)
    n1 = norm1_g[L].reshape(1, D_MODEL)
    qg = jnp.tile(q_gain[L], ATT_HEADS).reshape(1, ATT_WIDTH)
    kg = jnp.tile(k_gain[L], ATT_HEADS).reshape(1, ATT_WIDTH)
    consts = (norm2_g[L].reshape(1, D_MODEL), attn_out_g[L].reshape(1, ATT_WIDTH), ssm_out_g[L].reshape(1, SSM_WIDTH))
    weights = tuple(w[L].astype(BF16) for w in (w_glu, w_out, w_gate, w_up, w_down))

    tb = _s5_tables(ssm_a_re[L], ssm_a_im[L], ssm_log_dt[L], ssm_b_re[L], ssm_b_im[L])
    toe, w_pair, v_pair, a16_re, a16_im = _s5_chunk_operators(tb, ssm_c_re[L], ssm_c_im[L], ssm_d[L])

    xs = jnp.transpose(x_sample, (1, 0, 2)).reshape(t_new * bd, D_MODEL)
    qs, ks, vs, us = _inproj_sample(xs, mod, bd, n1, w_in_bf, qg, kg)
    batch_major = lambda a: jnp.transpose(a.reshape(t_new, bd, ATT_WIDTH), (1, 0, 2))
    qx = _expand_queries(batch_major(qs))
    ks3, vs3 = batch_major(ks), batch_major(vs)
    kt = jnp.transpose(cache_k[L], (0, 2, 3, 1)).reshape(bd, ATT_WIDTH, n_buf)
    vt = jnp.transpose(cache_v[L], (0, 2, 3, 1)).reshape(bd, ATT_WIDTH, n_buf)
    q, k, v, u, kt_p, vt_p = _inproj_prompt(x_prompt, mod_p, n1, w_in_bf, qg, kg)
    n_first = _attn_stream_batches(b, s)
    assert 0 < n_first < bd
    shifted, o_s_first, o_att = _stream_attn(kt, vt, 0, n_first, qx, ks3, vs3, q, k, v)
    y_ssm, f_re, f_im = _ssm_prompt(u, toe, w_pair, v_pair, a16_re, a16_im)
    okt, ovt, o_s_second, y_prompt = _stream_post(
        kt, vt, shifted, n_first, bd - n_first, qx, ks3, vs3, x_prompt, o_att, y_ssm, mod_p, consts, weights)
    o_att_s = jnp.concatenate([o_s_first, o_s_second], axis=0)
    gn = SSM_GROUPS * SSM_STATE
    h0_re = jnp.transpose(state_ssm_re[L], (1, 2, 0)).reshape(gn, bd)
    h0_im = jnp.transpose(state_ssm_im[L], (1, 2, 0)).reshape(gn, bd)
    y_ssm_t, hs_re, hs_im = _ssm_sample(us.reshape(t_new, bd, SSM_WIDTH), h0_re, h0_im, tb,
                                        ssm_c_re[L], ssm_c_im[L], ssm_d[L], t_new)
    o_att_t = jnp.transpose(o_att_s, (1, 0, 2)).reshape(t_new * bd, ATT_WIDTH)
    y_sample = _post_sample(xs, o_att_t, y_ssm_t.reshape(t_new * bd, SSM_WIDTH), mod, bd, consts, weights)
    y_sample = jnp.transpose(y_sample.reshape(t_new, bd, D_MODEL), (1, 0, 2))

    def from_t(a, nb, keep):
        return jnp.transpose(a.reshape(nb, ATT_HEADS, HEAD_DIM, keep), (0, 3, 1, 2))[None]

    def state_from_t(a):
        return jnp.transpose(a.reshape(SSM_GROUPS, SSM_STATE, bd), (2, 0, 1))[None]

    keep = min(MAX_SPAN, s)
    return (y_prompt, y_sample,
            from_t(kt_p, b, keep), from_t(vt_p, b, keep),
            f_re.reshape(b, SSM_GROUPS, SSM_STATE)[None], f_im.reshape(b, SSM_GROUPS, SSM_STATE)[None],
            from_t(okt, bd, n_buf), from_t(ovt, bd, n_buf),
            state_from_t(hs_re), state_from_t(hs_im))
```

```python
import functools
import math

import jax
import jax.numpy as jnp
import numpy as np
from jax import lax
from jax.experimental import pallas as pl
from jax.experimental.pallas import tpu as pltpu

F32 = jnp.float32
BF16 = jnp.bfloat16

D_MODEL = 1024
HEAD_DIM = 64
ATT_WIDTH = 512
ATT_HEADS = 8
SSM_WIDTH = 512
SSM_CH = 16
SSM_GROUPS = 32
SSM_STATE = 64
DILATIONS = ((128, 1), (512, 4), (2048, 16))
N_BACK = 128
MAX_SPAN = 2048
FFN_HIDDEN = 2816
N_MOD = 6
MOD_SHIFT1, MOD_SCALE1, MOD_GATE1, MOD_SHIFT2, MOD_SCALE2, MOD_GATE2 = range(N_MOD)
EPS = 1e-6

LANES = 128
SUBLANES = 8
VMEM_LIMIT = 56 * 1024 * 1024

TOKEN_TILE = 512
SUPER = 2048
QB = 128
TILE_GROUP = 4
LATE_GROUPS = 1
Q_SCALE = HEAD_DIM ** -0.5 * math.log2(math.e)
CHUNK = 16
NEG = -1e30


def _cparams(sem=None):
    return pltpu.CompilerParams(dimension_semantics=sem, vmem_limit_bytes=VMEM_LIMIT)


def _const_spec(shape):
    nd = len(shape)
    return pl.BlockSpec(shape, lambda *_: (0,) * nd, pipeline_mode=pl.Buffered(1))


def _sigmoid(x):
    return 1.0 / (1.0 + jnp.exp(-x))


def _split_bf16(a):
    hi = a.astype(BF16)
    lo = (a - hi.astype(F32)).astype(BF16)
    return hi, lo


def _ada_kernel(c_ref, w_ref, b_ref, o_ref):
    c = c_ref[...]
    a = c * _sigmoid(c)
    a_hi, a_lo = _split_bf16(a)
    w_hi, w_lo = _split_bf16(w_ref[...])
    acc = jnp.dot(a_hi, w_hi, preferred_element_type=F32)
    acc += jnp.dot(a_hi, w_lo, preferred_element_type=F32)
    acc += jnp.dot(a_lo, w_hi, preferred_element_type=F32)
    o_ref[...] = acc + b_ref[...]


def _ada(c_all, w_ada, b_ada):
    rows = c_all.shape[0]
    n = w_ada.shape[1]
    tn = 1024
    return pl.pallas_call(
        _ada_kernel,
        grid=(n // tn,),
        in_specs=[pl.BlockSpec((rows, D_MODEL), lambda j: (0, 0)),
                  pl.BlockSpec((D_MODEL, tn), lambda j: (0, j)),
                  pl.BlockSpec((1, tn), lambda j: (0, j))],
        out_specs=pl.BlockSpec((rows, tn), lambda j: (0, j)),
        out_shape=jax.ShapeDtypeStruct((rows, n), F32),
        compiler_params=_cparams(("arbitrary",)),
        name="ada",
    )(c_all, w_ada, b_ada)


def _head_rms(z, gain):
    lane = lax.broadcasted_iota(jnp.int32, (1, LANES), 1)
    lo = lane < HEAD_DIM
    outs = []
    for c in range(z.shape[1] // LANES):
        blk = z[:, c * LANES:(c + 1) * LANES]
        sq = blk * blk
        s_lo = jnp.sum(jnp.where(lo, sq, 0.0), axis=-1, keepdims=True)
        s_hi = jnp.sum(jnp.where(lo, 0.0, sq), axis=-1, keepdims=True)
        inv = jnp.where(lo, lax.rsqrt(s_lo * (1.0 / HEAD_DIM) + EPS), lax.rsqrt(s_hi * (1.0 / HEAD_DIM) + EPS))
        outs.append(blk * inv)
    return jnp.concatenate(outs, axis=1) * gain


def _inproj_body(x_ref, scale_ref, shift_ref, g_ref, w_ref, qg_ref, kg_ref, *out_refs, kt_rule=None):
    q_ref, k_ref, v_ref, u_ref = out_refs[:4]
    x = x_ref[...]
    ms = jnp.mean(x * x, axis=-1, keepdims=True)
    h = x * lax.rsqrt(ms + EPS) * g_ref[...]
    h = h * (1.0 + scale_ref[...]) + shift_ref[...]
    z = jnp.dot(h.astype(BF16), w_ref[...], preferred_element_type=F32)
    q = _head_rms(z[:, 0:ATT_WIDTH], qg_ref[...]) * Q_SCALE
    k = _head_rms(z[:, ATT_WIDTH:2 * ATT_WIDTH], kg_ref[...])
    v = z[:, 2 * ATT_WIDTH:3 * ATT_WIDTH]
    q_ref[...] = q
    k_ref[...] = k
    v_ref[...] = v
    u_ref[...] = z[:, 3 * ATT_WIDTH:]
    if kt_rule is not None:
        per_seq, kt_first = kt_rule
        kt_ref, vt_ref = out_refs[4:]

        @pl.when(pl.program_id(0) % per_seq >= kt_first)
        def _():
            kt_ref[...] = k.T
            vt_ref[...] = v.T


def _sample_mod_spec(bd, col):
    return pl.BlockSpec((bd, D_MODEL), lambda t: (0, col))


def _inproj_sample(x, mod, bd, g, w_bf, qg, kg):
    n = x.shape[0]
    tok = pl.BlockSpec((bd, D_MODEL), lambda t: (t, 0))
    out_tok = pl.BlockSpec((bd, ATT_WIDTH), lambda t: (t, 0))
    sd = jax.ShapeDtypeStruct((n, ATT_WIDTH), F32)
    return pl.pallas_call(
        _inproj_body,
        grid=(n // bd,),
        in_specs=[tok, _sample_mod_spec(bd, MOD_SCALE1), _sample_mod_spec(bd, MOD_SHIFT1), _const_spec((1, D_MODEL)),
                  _const_spec((D_MODEL, 4 * ATT_WIDTH)), _const_spec((1, ATT_WIDTH)), _const_spec((1, ATT_WIDTH))],
        out_specs=[out_tok] * 4,
        out_shape=[sd] * 4,
        compiler_params=_cparams(("arbitrary",)),
        name="inproj_sample",
    )(x, mod, mod, g, w_bf, qg, kg)


ATTN_PHASES = len(DILATIONS) + 1


def _attn_rider(q_ref, kc_ref, vc_ref, bias_ref, o_ref, kbuf, vbuf, acc_ref, m_ref, l_ref, *, n_super, side):
    step = pl.program_id(0)
    phase = step % ATTN_PHASES
    st = (step // ATTN_PHASES) % n_super
    lane = lax.broadcasted_iota(jnp.int32, (1, LANES), 1)
    lo = lane < HEAD_DIM
    nt_contract = (((1,), (1,)), ((), ()))
    n_groups = SUPER // QB // TILE_GROUP

    def pattern(p, d, groups, with_side):
        per_res = SUPER // d // QB
        if p == 0:
            for buf, cur_ref in ((kbuf, kc_ref), (vbuf, vc_ref)):
                @pl.when(st == 0)
                def _(buf=buf):
                    buf[0:SUPER, :] = jnp.zeros((SUPER, LANES), F32)

                @pl.when(st > 0)
                def _(buf=buf):
                    buf[0:SUPER, :] = buf[SUPER:2 * SUPER, :]

                buf[SUPER:2 * SUPER, :] = cur_ref[...]

        for gi in groups:
            tiles = []
            for u in range(TILE_GROUP):
                ti = gi * TILE_GROUP + u
                r = ti // per_res
                jt = ti % per_res
                row0 = r + d * QB * jt
                ks = SUPER + row0 - d * QB
                bias = bias_ref[jnp.where(st == 0, 0, 1)] if jt == 0 else bias_ref[1]
                q = q_ref[pl.ds(row0, QB, stride=d), :]
                k = kbuf[pl.ds(ks, 2 * QB, stride=d), :].astype(BF16)
                v = vbuf[pl.ds(ks, 2 * QB, stride=d), :].astype(BF16)
                v = jnp.concatenate([v, jnp.ones_like(v)], axis=1)
                q2 = jnp.concatenate([jnp.where(lo, q, 0.0), jnp.where(lo, 0.0, q)], axis=0).astype(BF16)
                s = lax.dot_general(q2, k, nt_contract, preferred_element_type=F32)
                tiles.append((s, bias, v, row0))
            probs = []
            for s, bias, v, row0 in tiles:
                s = s + jnp.concatenate([bias, bias], axis=0)
                m = jnp.max(s, axis=-1, keepdims=True)
                pr = jnp.exp2(s - m)
                probs.append((pr.astype(BF16), m))
            for (s, bias, v, row0), (pr, m) in zip(tiles, probs):
                pv = jnp.dot(pr, v, preferred_element_type=F32)
                rows = pl.ds(row0, QB, stride=d)
                acc_ref[p, rows, :] = jnp.where(lo, pv[0:QB, 0:LANES], pv[QB:2 * QB, 0:LANES])
                m_ref[p, rows, :] = jnp.where(lo, m[0:QB], m[QB:2 * QB])
                l_ref[p, rows, :] = jnp.where(lo, pv[0:QB, LANES:2 * LANES], pv[QB:2 * QB, LANES:2 * LANES])
            if with_side and gi < SIDE_PARTS:
                side(gi)

    last = len(DILATIONS) - 1
    for p, (_, d) in enumerate(DILATIONS):
        groups = range(n_groups - LATE_GROUPS) if p == last else range(n_groups)
        pl.when(phase == p)(functools.partial(pattern, p, d, groups, True))

    @pl.when(phase == ATTN_PHASES - 1)
    def _():
        pattern(last, DILATIONS[last][1], range(n_groups - LATE_GROUPS, n_groups), False)
        n_merge = SUPER // QB
        for ci in range(n_merge):
            rows = pl.ds(ci * QB, QB)
            m1, m2, m3 = m_ref[0, rows, :], m_ref[1, rows, :], m_ref[2, rows, :]
            mm = jnp.maximum(jnp.maximum(m1, m2), m3)
            w1, w2, w3 = jnp.exp2(m1 - mm), jnp.exp2(m2 - mm), jnp.exp2(m3 - mm)
            num = w1 * acc_ref[0, rows, :] + w2 * acc_ref[1, rows, :] + w3 * acc_ref[2, rows, :]
            den = w1 * l_ref[0, rows, :] + w2 * l_ref[1, rows, :] + w3 * l_ref[2, rows, :]
            o_ref[rows, :] = num / den
            if ci % (n_merge // SIDE_PARTS) == 0 and ci // (n_merge // SIDE_PARTS) < SIDE_PARTS:
                side(ci // (n_merge // SIDE_PARTS))


def _attn_bias():
    qi = np.arange(QB)[:, None]
    ki = np.arange(2 * QB)[None, :]
    dist = qi + QB - ki
    normal = (dist >= 0) & (dist <= N_BACK)
    first = normal & (ki >= QB)
    return jnp.asarray(np.where(np.stack([first, normal]), 0.0, NEG), F32)


SA_HEADS = 4
SA_WIDTH = SA_HEADS * HEAD_DIM
N_SA_IN = 7


RING = 3
SIDE_PARTS = 3


def _stream_kernel(*refs, rider, rider_takes_side, batch_lo, n_rider_in, n_rider_out, n_prev, n_rider_scratch):
    kt_hbm, vt_hbm, qx_ref, kn_ref, vn_ref, w_ref, hm_ref = refs[:N_SA_IN]
    rider_in = refs[N_SA_IN:N_SA_IN + n_rider_in]
    n_in = N_SA_IN + n_rider_in + n_prev
    okt_ref, ovt_ref, o_ref = refs[n_in:n_in + 3]
    rider_out = refs[n_in + 3:n_in + 3 + n_rider_out]
    pad_ref, newt_ref, kring, vring, sems = refs[n_in + 3 + n_rider_out:n_in + 8 + n_rider_out]
    rider_scratch = refs[len(refs) - n_rider_scratch:] if n_rider_scratch else ()
    n_buf = kring.shape[2]
    t_new = kn_ref.shape[0]
    n_tiles = n_buf // LANES

    step = pl.program_id(0)
    n_steps = pl.num_programs(0)

    def block_copy(i, s):
        hbm, ring = ((kt_hbm, kring), (vt_hbm, vring))[i]
        src = hbm.at[batch_lo + s // SA_NBLK, pl.ds((s % SA_NBLK) * SA_WIDTH, SA_WIDTH), :]
        return pltpu.make_async_copy(src, ring.at[s % RING], sems.at[i, s % RING])

    @pl.when(step == 0)
    def _():
        for s in range(RING - 1):
            for i in range(2):
                block_copy(i, s).start()

    @pl.when(step + RING - 1 < n_steps)
    def _():
        for i in range(2):
            block_copy(i, step + RING - 1).start()

    for i in range(2):
        block_copy(i, step).wait()
    kt_ref = kring.at[step % RING]
    vt_ref = vring.at[step % RING]

    def sample_attention():
        for i, new_ref in enumerate((kn_ref, vn_ref)):
            pad_ref[...] = jnp.zeros_like(pad_ref)
            pad_ref[0:t_new, :] = new_ref[...]
            newt_ref[i] = pad_ref[...].T
        k_ext = jnp.concatenate([kt_ref[...], newt_ref[0]], axis=1)
        v_ext = jnp.concatenate([vt_ref[...], newt_ref[1]], axis=1)
        w = w_ref[...]
        s = jnp.dot(qx_ref[...], k_ext.astype(BF16), preferred_element_type=F32)
        s = jnp.where(w > 0.0, s, NEG)
        m = jnp.max(s, axis=-1, keepdims=True)
        e = w * jnp.exp2(s - m)
        l = jnp.sum(e, axis=-1, keepdims=True)
        pv = lax.dot_general(e.astype(BF16), v_ext.astype(BF16), (((1,), (1,)), ((), ())),
                             preferred_element_type=F32)
        pv = pv * hm_ref[...] / l
        o = pv[0:t_new]
        for h in range(1, SA_HEADS):
            o = o + pv[h * t_new:(h + 1) * t_new]
        o_ref[...] = o

    boundary = {}

    def shift(part, n_parts, fresh):
        per = n_tiles // n_parts
        first = part * per
        keep = lax.broadcasted_iota(jnp.int32, (1, LANES), 1) < LANES - t_new
        if fresh:
            boundary.clear()
        for i, (src_ref, dst_ref) in enumerate(((kt_ref, okt_ref), (vt_ref, ovt_ref))):
            def rotated(j):
                if (i, j) in boundary:
                    return boundary.pop((i, j))
                t = src_ref[:, j * LANES:(j + 1) * LANES] if j < n_tiles else newt_ref[i]
                return pltpu.roll(t, LANES - t_new, axis=1)
            rolled = [rotated(j) for j in range(first, first + per + 1)]
            for j in range(per):
                dst_ref[:, (first + j) * LANES:(first + j + 1) * LANES] = jnp.where(keep, rolled[j], rolled[j + 1])
            if part + 1 < n_parts:
                boundary[(i, first + per)] = rolled[per]

    sample_attention()
    if rider_takes_side:
        shift(0, SIDE_PARTS + 1, fresh=True)
        rider(*rider_in, *rider_out, *rider_scratch,
              side=lambda j: shift(j + 1, SIDE_PARTS + 1, fresh=(j == 0)))
    else:
        shift(0, 1, fresh=True)
        rider(*rider_in, *rider_out, *rider_scratch)


def _sa_weights(t_new, n_buf):
    ext = n_buf + LANES
    pos = np.arange(ext)[None, :]
    i = np.arange(t_new)[:, None]
    delta = n_buf + i - pos
    is_real = (pos < n_buf + t_new)
    w = np.zeros((t_new, ext), np.float32)
    for window, d in DILATIONS:
        w += ((delta >= 0) & (delta % d == 0) & (delta <= window) & is_real)
    return jnp.asarray(np.tile(w, (SA_HEADS, 1)), F32)


def _head_mask(t_new):
    h = np.repeat(np.arange(SA_HEADS), t_new)[:, None]
    c = np.arange(SA_WIDTH)[None, :] // HEAD_DIM
    return (h == c).astype(np.float32)


SA_NBLK = ATT_WIDTH // SA_WIDTH


def _stream(kt, vt, prev, batch_lo, n_batches, qx, k_new, v_new, rider, rider_args, rider_in_specs,
            rider_out_specs, rider_out_shapes, name, rider_scratch=(), rider_takes_side=False):
    bd, _, n_buf = kt.shape
    t_new = k_new.shape[1]
    assert n_buf == MAX_SPAN, "window buffer must cover the longest pattern (no invalid positions)"
    rows = t_new * SA_HEADS
    buf = pl.BlockSpec((None, SA_WIDTH, n_buf), lambda i: (batch_lo + i // SA_NBLK, i % SA_NBLK, 0))
    new = pl.BlockSpec((None, t_new, SA_WIDTH), lambda i: (batch_lo + i // SA_NBLK, 0, i % SA_NBLK))
    o_blk = pl.BlockSpec((None, t_new, SA_WIDTH), lambda i: (i // SA_NBLK, 0, i % SA_NBLK))
    qx_blk = pl.BlockSpec((None, None, rows, SA_WIDTH), lambda i: (batch_lo + i // SA_NBLK, i % SA_NBLK, 0, 0))
    sd_buf = jax.ShapeDtypeStruct((bd, ATT_WIDTH, n_buf), F32)
    hbm = pl.BlockSpec(memory_space=pl.ANY)
    in_specs = [hbm, hbm, qx_blk, new, new, _const_spec((rows, n_buf + LANES)), _const_spec((rows, SA_WIDTH))]
    in_specs += list(rider_in_specs)
    args = [kt, vt, qx, k_new, v_new, _sa_weights(t_new, n_buf), jnp.asarray(_head_mask(t_new), F32)]
    args += list(rider_args)
    aliases = {}
    if prev is not None:
        aliases = {len(args): 0, len(args) + 1: 1}
        in_specs += [pl.BlockSpec(memory_space=pl.ANY)] * 2
        args += list(prev)
    return pl.pallas_call(
        functools.partial(_stream_kernel, rider=rider, rider_takes_side=rider_takes_side, batch_lo=batch_lo,
                          n_rider_in=len(rider_in_specs),
                          n_rider_out=len(rider_out_specs), n_prev=len(aliases),
                          n_rider_scratch=len(rider_scratch)),
        grid=(n_batches * SA_NBLK,),
        in_specs=in_specs,
        out_specs=[buf, buf, o_blk] + list(rider_out_specs),
        out_shape=[sd_buf, sd_buf, jax.ShapeDtypeStruct((n_batches, t_new, ATT_WIDTH), F32)] + list(rider_out_shapes),
        input_output_aliases=aliases,
        scratch_shapes=[pltpu.VMEM((LANES, SA_WIDTH), F32), pltpu.VMEM((2, SA_WIDTH, LANES), F32),
                        pltpu.VMEM((RING, SA_WIDTH, n_buf), F32), pltpu.VMEM((RING, SA_WIDTH, n_buf), F32),
                        pltpu.SemaphoreType.DMA((2, RING))] + list(rider_scratch),
        compiler_params=_cparams(("arbitrary",)),
        name=name,
    )(*args)


def _expand_queries(q):
    bd, t_new, _ = q.shape
    hm = jnp.asarray(_head_mask(t_new), F32)
    qb = q.reshape(bd, t_new, SA_NBLK, SA_WIDTH).transpose(0, 2, 1, 3)
    return (jnp.tile(qb, (1, 1, SA_HEADS, 1)) * hm).astype(BF16)


def _rider_tiling(b, s, steps):
    assert (b * s) % steps == 0
    tm = b * s // steps
    assert tm % LANES == 0 and s % tm == 0
    return tm, s // tm


def _prompt_mod_spec(per_seq, col):
    return pl.BlockSpec((None, 1, D_MODEL), lambda i: (i // per_seq, 0, col))


def _inproj_prompt(x, mod, g, w_bf, qg, kg):
    b, s, _ = x.shape
    tm = TOKEN_TILE
    per_seq = s // tm
    keep = min(MAX_SPAN, s)
    kt_first = per_seq - keep // tm
    tok = lambda w: pl.BlockSpec((tm, w), lambda i: (i, 0))
    out_t =pl.BlockSpec((None, ATT_WIDTH, tm), lambda i: (i // per_seq, 0, jnp.maximum(i % per_seq - kt_first, 0)))
    sd = jax.ShapeDtypeStruct((b * s, ATT_WIDTH), F32)
    sdt = jax.ShapeDtypeStruct((b, ATT_WIDTH, keep), F32)
    q, k, v, u, kt_p, vt_p = pl.pallas_call(
        functools.partial(_inproj_body, kt_rule=(per_seq, kt_first)),
        grid=(b * per_seq,),
        in_specs=[tok(D_MODEL), _prompt_mod_spec(per_seq, MOD_SCALE1), _prompt_mod_spec(per_seq, MOD_SHIFT1),
                  _const_spec((1, D_MODEL)), _const_spec((D_MODEL, 4 * ATT_WIDTH)),
                  _const_spec((1, ATT_WIDTH)), _const_spec((1, ATT_WIDTH))],
        out_specs=[tok(ATT_WIDTH)] * 4 + [out_t] * 2,
        out_shape=[sd] * 4 + [sdt] * 2,
        compiler_params=_cparams(("arbitrary",)),
        name="inproj_prompt",
    )(x.reshape(b * s, D_MODEL), mod, mod, g, w_bf, qg, kg)
    r3 = lambda a: a.reshape(b, s, ATT_WIDTH)
    return r3(q), r3(k), r3(v), r3(u), kt_p, vt_p


def _attn_stream_batches(b, s):
    n_super = s // SUPER
    units = b * (ATT_WIDTH // LANES) * n_super
    steps = units * ATTN_PHASES
    assert steps % SA_NBLK == 0
    return steps // SA_NBLK


def _stream_attn(kt, vt, batch_lo, n_batches, qx, k_new, v_new, q, k, v):
    b, s, _ = q.shape
    assert s % SUPER == 0
    n_super = s // SUPER
    n_hp = ATT_WIDTH // LANES
    assert n_batches == _attn_stream_batches(b, s)

    n_units = b * n_hp * n_super

    def unit_block(lead):
        def index(i):
            un = jnp.minimum((i + lead) // ATTN_PHASES, n_units - 1)
            return un // (n_hp * n_super), un % n_super, (un // n_super) % n_hp
        return pl.BlockSpec((None, SUPER, LANES), index)

    outs = _stream(
        kt, vt, None, batch_lo, n_batches, qx, k_new, v_new,
        functools.partial(_attn_rider, n_super=n_super),
        (q, k, v, _attn_bias()),
        [unit_block(0), unit_block(ATTN_PHASES - 1), unit_block(ATTN_PHASES - 1), _const_spec((2, QB, 2 * QB))],
        [unit_block(0)], [jax.ShapeDtypeStruct((b, s, ATT_WIDTH), F32)], "stream_attn",
        rider_scratch=[pltpu.VMEM((2 * SUPER, LANES), F32)] * 2 + [pltpu.VMEM((len(DILATIONS), SUPER, LANES), F32)] * 3,
        rider_takes_side=True)
    okt, ovt, o_s, o_att = outs
    return (okt, ovt), o_s, o_att


def _stream_post(kt, vt, prev, batch_lo, n_batches, qx, k_new, v_new, x, o_att, y_ssm, mod, consts, weights):
    b, s, _ = x.shape
    tm, per_seq = _rider_tiling(b, s, n_batches * SA_NBLK)
    tok = lambda w: pl.BlockSpec((tm, w), lambda i: (i, 0))
    mods = [_prompt_mod_spec(per_seq, c) for c in (MOD_GATE1, MOD_SCALE2, MOD_SHIFT2, MOD_GATE2)]
    okt, ovt, o_s, y = _stream(
        kt, vt, prev, batch_lo, n_batches, qx, k_new, v_new, _post_body,
        (x.reshape(b * s, D_MODEL), o_att.reshape(b * s, ATT_WIDTH), y_ssm.reshape(b * s, SSM_WIDTH),
         mod, mod, mod, mod, *consts, *weights),
        [tok(D_MODEL), tok(ATT_WIDTH), tok(SSM_WIDTH)] + mods
        + [_const_spec(c.shape) for c in consts] + [_const_spec(w.shape) for w in weights],
        [tok(D_MODEL)], [jax.ShapeDtypeStruct((b * s, D_MODEL), F32)], "stream_post")
    return okt, ovt, o_s, y.reshape(b, s, D_MODEL)


def _s5_tables(a_re, a_im, log_dt, b_re, b_im):
    g, n = a_re.shape
    dt = jnp.exp(log_dt)[:, None]
    x, y = dt * a_re, dt * a_im
    ks = jnp.arange(CHUNK + 1, dtype=F32)
    mag = jnp.exp(x[:, :, None] * ks)
    pw_re, pw_im = mag * jnp.cos(y[:, :, None] * ks), mag * jnp.sin(y[:, :, None] * ks)
    e_re = jnp.expm1(x) * jnp.cos(y) - 2.0 * jnp.sin(0.5 * y) ** 2
    e_im = jnp.exp(x) * jnp.sin(y)
    den = a_re * a_re + a_im * a_im
    f_re = (e_re * a_re + e_im * a_im) / den
    f_im = (e_im * a_re - e_re * a_im) / den
    bb_re = f_re[..., None] * b_re - f_im[..., None] * b_im
    bb_im = f_re[..., None] * b_im + f_im[..., None] * b_re
    return dict(pw_re=pw_re, pw_im=pw_im, bb_re=bb_re, bb_im=bb_im)


def _s5_chunk_operators(tb, c_re, c_im, d_skip):
    pw_re, pw_im, bb_re, bb_im = tb["pw_re"], tb["pw_im"], tb["bb_re"], tb["bb_im"]
    g, n, ch = bb_re.shape
    L = CHUNK
    hp = lax.Precision.HIGHEST
    ct_re, ct_im = c_re.transpose(0, 2, 1), c_im.transpose(0, 2, 1)
    pr, pi = pw_re[:, :, :, None], pw_im[:, :, :, None]
    ca_re = ct_re[:, :, None, :] * pr - ct_im[:, :, None, :] * pi
    ca_im = ct_re[:, :, None, :] * pi + ct_im[:, :, None, :] * pr
    a_cat = jnp.concatenate([ca_re[:, :, :L].reshape(g, n, L * ch), -ca_im[:, :, :L].reshape(g, n, L * ch)], axis=1)
    b_cat = jnp.concatenate([bb_re, bb_im], axis=1)
    base = jnp.einsum("gnd,gnx->gdx", b_cat, a_cat, precision=hp)
    skip = jnp.eye(ch, dtype=F32)[None] * d_skip.reshape(g, ch, 1)
    base = base + jnp.pad(skip, ((0, 0), (0, 0), (0, (L - 1) * ch)))
    toe = jnp.stack([jnp.pad(base, ((0, 0), (0, 0), (s * ch, 0)))[:, :, :L * ch] for s in range(L)], axis=1)
    toe = toe.reshape(g, L * ch, L * ch)
    rv_re = pw_re[:, :, L - 1::-1].transpose(0, 2, 1)[:, :, None, :]
    rv_im = pw_im[:, :, L - 1::-1].transpose(0, 2, 1)[:, :, None, :]
    bt_re, bt_im = bb_re.transpose(0, 2, 1)[:, None], bb_im.transpose(0, 2, 1)[:, None]
    ws_re = (rv_re * bt_re - rv_im * bt_im).reshape(g, L * ch, n)
    ws_im = (rv_re * bt_im + rv_im * bt_re).reshape(g, L * ch, n)
    vr = ca_re[:, :, 1:].reshape(g, n, L * ch)
    vi = (-ca_im[:, :, 1:]).reshape(g, n, L * ch)
    gp = g // 2
    z_w = jnp.zeros((gp, L * ch, n), F32)
    wr, wi = ws_re.reshape(gp, 2, L * ch, n), ws_im.reshape(gp, 2, L * ch, n)
    w_pair = jnp.concatenate([
        jnp.concatenate([wr[:, 0], z_w, wi[:, 0], z_w], axis=2),
        jnp.concatenate([z_w, wr[:, 1], z_w, wi[:, 1]], axis=2)], axis=1)
    z_v = jnp.zeros((gp, n, L * ch), F32)
    vr, vi = vr.reshape(gp, 2, n, L * ch), vi.reshape(gp, 2, n, L * ch)
    v_pair = jnp.concatenate([
        jnp.concatenate([vr[:, 0], z_v], axis=2), jnp.concatenate([z_v, vr[:, 1]], axis=2),
        jnp.concatenate([vi[:, 0], z_v], axis=2), jnp.concatenate([z_v, vi[:, 1]], axis=2)], axis=1)
    a16_re = pw_re[:, :, L].reshape(1, g * n)
    a16_im = pw_im[:, :, L].reshape(1, g * n)
    return toe.astype(BF16), w_pair.astype(BF16), v_pair.astype(BF16), a16_re, a16_im


GROUPS_PER_BLOCK = LANES // SSM_CH
REGROUP_ROWS = 32


def _granule_transpose(arrs):
    n = GROUPS_PER_BLOCK
    gran = lax.broadcasted_iota(jnp.int32, (1, LANES), 1) // SSM_CH
    skew = [a if t == 0 else pltpu.roll(a, t * SSM_CH, axis=1) for t, a in enumerate(arrs)]
    out = []
    for p in range(n):
        c = skew[(-p) % n]
        for g in range(1, n):
            c = jnp.where(gran == g, skew[(g - p) % n], c)
        out.append(c if p == 0 else pltpu.roll(c, LANES - p * SSM_CH, axis=1))
    return out


def _ssm_a_kernel(u_ref, w_ref, xg_ref, sre_ref, sim_ref, xs_ref):
    nb, s_len, _ = u_ref.shape
    n_chunks = s_len // CHUNK
    half = CHUNK // 2

    for b in range(nb):
        def regroup(ct, c, b=b):
            c0 = pl.multiple_of(ct * REGROUP_ROWS, REGROUP_ROWS)
            rows = pl.ds(b * n_chunks + c0, REGROUP_ROWS)
            for j in range(2):
                z = [u_ref[b, pl.ds(c0 * CHUNK + half * j + tt, REGROUP_ROWS, stride=CHUNK), :]
                     for tt in range(half)]
                for p, xp in enumerate(_granule_transpose(z)):
                    xs_ref[2 * p + j, rows, :] = xp
            return c
        lax.fori_loop(0, n_chunks // REGROUP_ROWS, regroup, 0)

    for col in range(2 * GROUPS_PER_BLOCK):
        xg_ref[:, col * LANES:(col + 1) * LANES] = xs_ref[col].astype(BF16)
    for pp in range(GROUPS_PER_BLOCK // 2):
        xp = xg_ref[:, pp * 4 * LANES:(pp + 1) * 4 * LANES]
        st = jnp.dot(xp, w_ref[pp], preferred_element_type=F32)
        sre_ref[pp] = st[:, 0:LANES]
        sim_ref[pp] = st[:, LANES:2 * LANES]


def _ssm_scan_kernel(sre_ref, sim_ref, are_ref, aim_ref, hre_ref, him_ref, fre_ref, fim_ref, *, nb):
    ncb, rows, _ = sre_ref.shape
    n_chunks = rows // nb
    a_re = [are_ref[cb] for cb in range(ncb)]
    a_im = [aim_ref[cb] for cb in range(ncb)]

    def body(c, carry):
        r = pl.ds(c, nb, stride=n_chunks)
        out = []
        for cb in range(ncb):
            h_re, h_im = carry[2 * cb], carry[2 * cb + 1]
            hre_ref[cb, r, :] = h_re
            him_ref[cb, r, :] = h_im
            out.append(a_re[cb] * h_re - a_im[cb] * h_im + sre_ref[cb, r, :])
            out.append(a_re[cb] * h_im + a_im[cb] * h_re + sim_ref[cb, r, :])
        return tuple(out)

    z = jnp.zeros((nb, LANES), F32)
    fin = lax.fori_loop(0, n_chunks, body, (z,) * (2 * ncb), unroll=4)
    for cb in range(ncb):
        fre_ref[cb] = fin[2 * cb]
        fim_ref[cb] = fin[2 * cb + 1]


def _ssm_c_kernel(xg_ref, hre_ref, him_ref, t_ref, v_ref, y_ref, ys_ref):
    nb, s_len, _ = y_ref.shape
    n_chunks = s_len // CHUNK
    half = CHUNK // 2
    for pp in range(GROUPS_PER_BLOCK // 2):
        hcat = jnp.concatenate([hre_ref[pp], him_ref[pp]], axis=1).astype(BF16)
        inter = jnp.dot(hcat, v_ref[pp], preferred_element_type=F32)
        for e in range(2):
            g = 2 * pp + e
            intra = jnp.dot(xg_ref[:, g * 2 * LANES:(g + 1) * 2 * LANES], t_ref[g],
                            preferred_element_type=F32)
            yg = intra + inter[:, e * 2 * LANES:(e + 1) * 2 * LANES]
            ys_ref[2 * g] = yg[:, 0:LANES]
            ys_ref[2 * g + 1] = yg[:, LANES:2 * LANES]

    for b in range(nb):
        def regroup(ct, c, b=b):
            c0 = pl.multiple_of(ct * REGROUP_ROWS, REGROUP_ROWS)
            rows = pl.ds(b * n_chunks + c0, REGROUP_ROWS)
            for j in range(2):
                yp =[ys_ref[2 * p + j, rows, :] for p in range(GROUPS_PER_BLOCK)]
                for tt, zt in enumerate(_granule_transpose(yp)):
                    y_ref[b, pl.ds(c0 * CHUNK + half * j + tt, REGROUP_ROWS, stride=CHUNK), :] = zt
            return c
        lax.fori_loop(0, n_chunks // REGROUP_ROWS, regroup, 0)


def _ssm_prompt(u, toe, w_pair, v_pair, a16_re, a16_im):
    b, s, _ = u.shape
    rows = b * (s // CHUNK)
    nblk = SSM_WIDTH // LANES
    gw = GROUPS_PER_BLOCK * CHUNK * SSM_CH
    ncb = GROUPS_PER_BLOCK * SSM_STATE // LANES
    ublk = pl.BlockSpec((b, s, LANES), lambda j: (0, 0, j))
    sblk = pl.BlockSpec((ncb, rows, LANES), lambda j: (j, 0, 0))
    sd_h = jax.ShapeDtypeStruct((nblk * ncb, rows, LANES), F32)
    xg, s_re, s_im = pl.pallas_call(
        _ssm_a_kernel,
        grid=(nblk,),
        in_specs=[ublk, pl.BlockSpec((GROUPS_PER_BLOCK // 2, 4 * LANES, 2 * LANES), lambda j: (j, 0, 0))],
        out_specs=[pl.BlockSpec((rows, gw), lambda j: (0, j)), sblk, sblk],
        out_shape=[jax.ShapeDtypeStruct((rows, nblk * gw), BF16), sd_h, sd_h],
        scratch_shapes=[pltpu.VMEM((gw // LANES, rows, LANES), F32)],
        compiler_params=_cparams(("arbitrary",)),
        name="ssm_chunk_states",
    )(u, w_pair)

    scan_cb = 2 * ncb
    assert (nblk * ncb) % scan_cb == 0
    scan_blk = pl.BlockSpec((scan_cb, rows, LANES), lambda j: (j, 0, 0))
    ablk = pl.BlockSpec((scan_cb, 1, LANES), lambda j: (j, 0, 0))
    fblk = pl.BlockSpec((scan_cb, b, LANES), lambda j: (j, 0, 0))
    sd_f = jax.ShapeDtypeStruct((nblk * ncb, b, LANES), F32)
    a16_re = a16_re.reshape(nblk * ncb, 1, LANES)
    a16_im = a16_im.reshape(nblk * ncb, 1, LANES)
    h_re, h_im, f_re, f_im = pl.pallas_call(
        functools.partial(_ssm_scan_kernel, nb=b),
        grid=(nblk * ncb // scan_cb,),
        in_specs=[scan_blk, scan_blk, ablk, ablk],
        out_specs=[scan_blk, scan_blk, fblk, fblk],
        out_shape=[sd_h, sd_h, sd_f, sd_f],
        compiler_params=_cparams(("arbitrary",)),
        name="ssm_scan",
    )(s_re, s_im, a16_re, a16_im)

    y = pl.pallas_call(
        _ssm_c_kernel,
        grid=(nblk,),
        in_specs=[pl.BlockSpec((rows, gw), lambda j: (0, j)), sblk, sblk,
                  pl.BlockSpec((GROUPS_PER_BLOCK, 2 * LANES, 2 * LANES), lambda j: (j, 0, 0)),
                  pl.BlockSpec((GROUPS_PER_BLOCK // 2, 2 * LANES, 4 * LANES), lambda j: (j, 0, 0))],
        out_specs=ublk,
        out_shape=jax.ShapeDtypeStruct((b, s, SSM_WIDTH), F32),
        scratch_shapes=[pltpu.VMEM((gw // LANES, rows, LANES), F32)],
        compiler_params=_cparams(("arbitrary",)),
        name="ssm_outputs",
    )(xg, h_re, h_im, toe, v_pair)
    to_rows = lambda f: jnp.transpose(f, (1, 0, 2)).reshape(b, nblk * ncb * LANES)
    return y, to_rows(f_re), to_rows(f_im)


def _ssm_sample_kernel(u_ref, hre_ref, him_ref, are_ref, aim_ref, bre_ref, bim_ref, cre_ref, cim_ref, d_ref,
                       y_ref, ore_ref, oim_ref, *, t_new):
    h_re = hre_ref[...].T
    h_im = him_ref[...].T
    a_re, a_im = are_ref[...], aim_ref[...]
    bd = h_re.shape[0]
    for t in range(t_new):
        u = u_ref[t]
        ub = u.astype(BF16)
        n_re = a_re * h_re - a_im * h_im + jnp.dot(ub, bre_ref[...], preferred_element_type=F32)
        n_im = a_re * h_im + a_im * h_re + jnp.dot(ub, bim_ref[...], preferred_element_type=F32)
        h_re, h_im = n_re, n_im
        y = (jnp.dot(h_re.astype(BF16), cre_ref[...], preferred_element_type=F32)
             + jnp.dot(h_im.astype(BF16), cim_ref[...], preferred_element_type=F32) + d_ref[...] * u)
        y_ref[t] = y
    ore_ref[...] = h_re.T
    oim_ref[...] = h_im.T


def _block_diag(m):
    g, r, c = m.shape
    spread = jnp.asarray(np.tile(np.eye(c, dtype=np.float32), (1, g)), BF16)
    tiled = jnp.dot(m.reshape(g * r, c).astype(BF16), spread, preferred_element_type=F32)
    keep = np.arange(g * r)[:, None] // r == np.arange(g * c)[None, :] // c
    return jnp.where(jnp.asarray(keep), tiled, 0.0).astype(BF16)


def _ssm_sample(u, h0_re_t, h0_im_t, tb, c_re, c_im, d_skip, t_new):
    gn = SSM_GROUPS * SSM_STATE
    a_re = tb["pw_re"][:, :, 1].reshape(1, gn)
    a_im = tb["pw_im"][:, :, 1].reshape(1, gn)
    b_re = _block_diag(tb["bb_re"].transpose(0, 2, 1))
    b_im = _block_diag(tb["bb_im"].transpose(0, 2, 1))
    cb_re = _block_diag(c_re.transpose(0, 2, 1))
    cb_im = _block_diag(-c_im.transpose(0, 2, 1))
    bd = u.shape[1]
    full = lambda shape: pl.BlockSpec(shape, lambda i: (0,) * len(shape))
    return pl.pallas_call(
        functools.partial(_ssm_sample_kernel, t_new=t_new),
        grid=(1,),
        in_specs=[full((t_new, bd, SSM_WIDTH)), full((gn, bd)), full((gn, bd)), full((1, gn)), full((1, gn)),
                  full((SSM_WIDTH, gn)), full((SSM_WIDTH, gn)), full((gn, SSM_WIDTH)), full((gn, SSM_WIDTH)),
                  full((1, SSM_WIDTH))],
        out_specs=[full((t_new, bd, SSM_WIDTH)), full((gn, bd)), full((gn, bd))],
        out_shape=[jax.ShapeDtypeStruct((t_new, bd, SSM_WIDTH), F32),
                   jax.ShapeDtypeStruct((gn, bd), F32), jax.ShapeDtypeStruct((gn, bd), F32)],
        compiler_params=_cparams(("arbitrary",)),
        name="ssm_sample",
    )(u, h0_re_t, h0_im_t, a_re, a_im, b_re, b_im, cb_re, cb_im, d_skip.reshape(1, SSM_WIDTH))


def _rms(x, gain):
    return x * lax.rsqrt(jnp.mean(x * x, axis=-1, keepdims=True) + EPS) * gain


def _gelu_tanh(x):
    return 0.5 * x * (1.0 + jnp.tanh(math.sqrt(2.0 / math.pi) * (x + 0.044715 * (x * x * x))))


def _post_body(x_ref, oa_ref, ys_ref, g1_ref, sc2_ref, sh2_ref, g2_ref, n2_ref, ag_ref, sg_ref,
               wglu_ref, wout_ref, wg_ref, wu_ref, wd_ref, o_ref):
    ya = _gelu_tanh(ys_ref[...])
    ya = ya * _sigmoid(jnp.dot(ya.astype(BF16), wglu_ref[...], preferred_element_type=F32))
    merged = jnp.concatenate([_rms(oa_ref[...], ag_ref[...]), _rms(ya, sg_ref[...])], axis=1)
    x1 = x_ref[...] + g1_ref[...] * jnp.dot(merged.astype(BF16), wout_ref[...], preferred_element_type=F32)
    h2 = (_rms(x1, n2_ref[...]) * (1.0 + sc2_ref[...]) + sh2_ref[...]).astype(BF16)
    gate = jnp.dot(h2, wg_ref[...], preferred_element_type=F32)
    up = jnp.dot(h2, wu_ref[...], preferred_element_type=F32)
    act = (gate * _sigmoid(gate) * up).astype(BF16)
    o_ref[...] = x1 + g2_ref[...] * jnp.dot(act, wd_ref[...], preferred_element_type=F32)


def _post_sample(x, o_att, y_ssm, mod, bd, consts, weights):
    n = x.shape[0]
    tok = lambda w: pl.BlockSpec((bd, w), lambda t: (t, 0))
    mods = [_sample_mod_spec(bd, c) for c in (MOD_GATE1, MOD_SCALE2, MOD_SHIFT2, MOD_GATE2)]
    return pl.pallas_call(
        _post_body,
        grid=(n // bd,),
        in_specs=[tok(D_MODEL), tok(ATT_WIDTH), tok(SSM_WIDTH)] + mods
                 + [_const_spec(c.shape) for c in consts] + [_const_spec(w.shape) for w in weights],
        out_specs=tok(D_MODEL),
        out_shape=jax.ShapeDtypeStruct((n, D_MODEL), F32),
        compiler_params=_cparams(("arbitrary",)),
        name="post_sample",
    )(x, o_att, y_ssm, mod, mod, mod, mod, *consts, *weights)


def kernel(x_prompt, x_sample, cache_k, cache_v, state_ssm_re, state_ssm_im, c_prompt, c_sample, norm1_g, norm2_g, w_ada, b_ada, w_in, q_gain, k_gain, ssm_a_re, ssm_a_im, ssm_log_dt, ssm_b_re, ssm_b_im, ssm_c_re, ssm_c_im, ssm_d, w_glu, attn_out_g, ssm_out_g, w_out, w_gate, w_up, w_down):
    depth = norm1_g.shape[0]
    assert depth == 1, "one decoder layer"
    b, s, _ = x_prompt.shape
    bd, t_new, _ = x_sample.shape
    n_buf = cache_k.shape[2]
    L = 0

    assert bd % SUBLANES == 0
    pad = (-(b + bd)) % SUBLANES
    c_all = jnp.concatenate([c_sample, c_prompt, jnp.zeros((pad, D_MODEL), F32)], axis=0)
    mod = _ada(c_all, w_ada[L], b_ada[L].reshape(1, -1))
    mod_p = mod[bd:bd + b].reshape(b, 1, N_MOD * D_MODEL)

    w_in_bf = w_in[L].astype(BF16)
    n1 = norm1_g[L].reshape(1, D_MODEL)
    qg = jnp.tile(q_gain[L], ATT_HEADS).reshape(1, ATT_WIDTH)
    kg = jnp.tile(k_gain[L], ATT_HEADS).reshape(1, ATT_WIDTH)
    consts = (norm2_g[L].reshape(1, D_MODEL), attn_out_g[L].reshape(1, ATT_WIDTH), ssm_out_g[L].reshape(1, SSM_WIDTH))
    weights = tuple(w[L].astype(BF16) for w in (w_glu, w_out, w_gate, w_up, w_down))

    tb = _s5_tables(ssm_a_re[L], ssm_a_im[L], ssm_log_dt[L], ssm_b_re[L], ssm_b_im[L])
    toe, w_pair, v_pair, a16_re, a16_im = _s5_chunk_operators(tb, ssm_c_re[L], ssm_c_im[L], ssm_d[L])

    xs = jnp.transpose(x_sample, (1, 0, 2)).reshape(t_new * bd, D_MODEL)
    qs, ks, vs, us = _inproj_sample(xs, mod, bd, n1, w_in_bf, qg, kg)
    batch_major = lambda a: jnp.transpose(a.reshape(t_new, bd, ATT_WIDTH), (1, 0, 2))
    qx = _expand_queries(batch_major(qs))
    ks3, vs3 = batch_major(ks), batch_major(vs)
    kt = jnp.transpose(cache_k[L], (0, 2, 3, 1)).reshape(bd, ATT_WIDTH, n_buf)
    vt = jnp.transpose(cache_v[L], (0, 2, 3, 1)).reshape(bd, ATT_WIDTH, n_buf)
    q, k, v, u, kt_p, vt_p = _inproj_prompt(x_prompt, mod_p, n1, w_in_bf, qg, kg)
    n_first = _attn_stream_batches(b, s)
    assert 0 < n_first < bd
    shifted, o_s_first, o_att = _stream_attn(kt, vt, 0, n_first, qx, ks3, vs3, q, k, v)
    y_ssm, f_re, f_im = _ssm_prompt(u, toe, w_pair, v_pair, a16_re, a16_im)
    okt, ovt, o_s_second, y_prompt = _stream_post(
        kt, vt, shifted, n_first, bd - n_first, qx, ks3, vs3, x_prompt, o_att, y_ssm, mod_p, consts, weights)
    o_att_s = jnp.concatenate([o_s_first, o_s_second], axis=0)
    gn = SSM_GROUPS * SSM_STATE
    h0_re = jnp.transpose(state_ssm_re[L], (1, 2, 0)).reshape(gn, bd)
    h0_im = jnp.transpose(state_ssm_im[L], (1, 2, 0)).reshape(gn, bd)
    y_ssm_t, hs_re, hs_im = _ssm_sample(us.reshape(t_new, bd, SSM_WIDTH), h0_re, h0_im, tb,
                                        ssm_c_re[L], ssm_c_im[L], ssm_d[L], t_new)
    o_att_t = jnp.transpose(o_att_s, (1, 0, 2)).reshape(t_new * bd, ATT_WIDTH)
    y_sample = _post_sample(xs, o_att_t, y_ssm_t.reshape(t_new * bd, SSM_WIDTH), mod, bd, consts, weights)
    y_sample = jnp.transpose(y_sample.reshape(t_new, bd, D_MODEL), (1, 0, 2))

    def from_t(a, nb, keep):
        return jnp.transpose(a.reshape(nb, ATT_HEADS, HEAD_DIM, keep), (0, 3, 1, 2))[None]

    def state_from_t(a):
        return jnp.transpose(a.reshape(SSM_GROUPS, SSM_STATE, bd), (2, 0, 1))[None]

    keep = min(MAX_SPAN, s)
    return (y_prompt, y_sample,
            from_t(kt_p, b, keep), from_t(vt_p, b, keep),
            f_re.reshape(b, SSM_GROUPS, SSM_STATE)[None], f_im.reshape(b, SSM_GROUPS, SSM_STATE)[None],
            from_t(okt, bd, n_buf), from_t(ovt, bd, n_buf),
            state_from_t(hs_re), state_from_t(hs_im))
```

```python
import functools
import math

import jax
import jax.numpy as jnp
import numpy as np
from jax import lax
from jax.experimental import pallas as pl
from jax.experimental.pallas import tpu as pltpu

F32 = jnp.float32
BF16 = jnp.bfloat16

D_MODEL = 1024
HEAD_DIM = 64
ATT_WIDTH = 512
ATT_HEADS = 8
SSM_WIDTH = 512
SSM_CH = 16
SSM_GROUPS = 32
SSM_STATE = 64
DILATIONS = ((128, 1), (512, 4), (2048, 16))
N_BACK = 128
MAX_SPAN = 2048
FFN_HIDDEN = 2816
N_MOD = 6
MOD_SHIFT1, MOD_SCALE1, MOD_GATE1, MOD_SHIFT2, MOD_SCALE2, MOD_GATE2 = range(N_MOD)
EPS = 1e-6

LANES = 128
SUBLANES = 8
VMEM_LIMIT = 56 * 1024 * 1024

TOKEN_TILE = 512
SUPER = 2048
QB = 128
TILE_GROUP = 4
Q_SCALE = HEAD_DIM ** -0.5 * math.log2(math.e)
CHUNK = 16
NEG = -1e30


def _cparams(sem=None):
    return pltpu.CompilerParams(dimension_semantics=sem, vmem_limit_bytes=VMEM_LIMIT)


def _const_spec(shape):
    nd = len(shape)
    return pl.BlockSpec(shape, lambda *_: (0,) * nd, pipeline_mode=pl.Buffered(1))


def _sigmoid(x):
    return 1.0 / (1.0 + jnp.exp(-x))


def _split_bf16(a):
    hi = a.astype(BF16)
    lo = (a - hi.astype(F32)).astype(BF16)
    return hi, lo


def _ada_kernel(c_ref, w_ref, b_ref, o_ref):
    c = c_ref[...]
    a = c * _sigmoid(c)
    a_hi, a_lo = _split_bf16(a)
    w_hi, w_lo = _split_bf16(w_ref[...])
    acc = jnp.dot(a_hi, w_hi, preferred_element_type=F32)
    acc += jnp.dot(a_hi, w_lo, preferred_element_type=F32)
    acc += jnp.dot(a_lo, w_hi, preferred_element_type=F32)
    o_ref[...] = acc + b_ref[...]


def _ada(c_all, w_ada, b_ada):
    rows = c_all.shape[0]
    n = w_ada.shape[1]
    tn = 1024
    return pl.pallas_call(
        _ada_kernel,
        grid=(n // tn,),
        in_specs=[pl.BlockSpec((rows, D_MODEL), lambda j: (0, 0)),
                  pl.BlockSpec((D_MODEL, tn), lambda j: (0, j)),
                  pl.BlockSpec((1, tn), lambda j: (0, j))],
        out_specs=pl.BlockSpec((rows, tn), lambda j: (0, j)),
        out_shape=jax.ShapeDtypeStruct((rows, n), F32),
        compiler_params=_cparams(("arbitrary",)),
        name="ada",
    )(c_all, w_ada, b_ada)


def _head_rms(z, gain):
    lane = lax.broadcasted_iota(jnp.int32, (1, LANES), 1)
    lo = lane < HEAD_DIM
    outs = []
    for c in range(z.shape[1] // LANES):
        blk = z[:, c * LANES:(c + 1) * LANES]
        sq = blk * blk
        s_lo = jnp.sum(jnp.where(lo, sq, 0.0), axis=-1, keepdims=True)
        s_hi = jnp.sum(jnp.where(lo, 0.0, sq), axis=-1, keepdims=True)
        inv = jnp.where(lo, lax.rsqrt(s_lo * (1.0 / HEAD_DIM) + EPS), lax.rsqrt(s_hi * (1.0 / HEAD_DIM) + EPS))
        outs.append(blk * inv)
    return jnp.concatenate(outs, axis=1) * gain


def _inproj_body(x_ref, scale_ref, shift_ref, g_ref, w_ref, qg_ref, kg_ref, *out_refs, kt_rule=None):
    q_ref, k_ref, v_ref, u_ref = out_refs[:4]
    x = x_ref[...]
    ms = jnp.mean(x * x, axis=-1, keepdims=True)
    h = x * lax.rsqrt(ms + EPS) * g_ref[...]
    h = h * (1.0 + scale_ref[...]) + shift_ref[...]
    z = jnp.dot(h.astype(BF16), w_ref[...], preferred_element_type=F32)
    q = _head_rms(z[:, 0:ATT_WIDTH], qg_ref[...]) * Q_SCALE
    k = _head_rms(z[:, ATT_WIDTH:2 * ATT_WIDTH], kg_ref[...])
    v = z[:, 2 * ATT_WIDTH:3 * ATT_WIDTH]
    q_ref[...] = q
    k_ref[...] = k
    v_ref[...] = v
    u_ref[...] = z[:, 3 * ATT_WIDTH:]
    if kt_rule is not None:
        per_seq, kt_first = kt_rule
        kt_ref, vt_ref = out_refs[4:]

        @pl.when(pl.program_id(0) % per_seq >= kt_first)
        def _():
            kt_ref[...] = k.T
            vt_ref[...] = v.T


def _sample_mod_spec(bd, col):
    return pl.BlockSpec((bd, D_MODEL), lambda t: (0, col))


def _inproj_sample(x, mod, bd, g, w_bf, qg, kg):
    n = x.shape[0]
    tok = pl.BlockSpec((bd, D_MODEL), lambda t: (t, 0))
    out_tok = pl.BlockSpec((bd, ATT_WIDTH), lambda t: (t, 0))
    sd = jax.ShapeDtypeStruct((n, ATT_WIDTH), F32)
    return pl.pallas_call(
        _inproj_body,
        grid=(n // bd,),
        in_specs=[tok, _sample_mod_spec(bd, MOD_SCALE1), _sample_mod_spec(bd, MOD_SHIFT1), _const_spec((1, D_MODEL)),
                  _const_spec((D_MODEL, 4 * ATT_WIDTH)), _const_spec((1, ATT_WIDTH)), _const_spec((1, ATT_WIDTH))],
        out_specs=[out_tok] * 4,
        out_shape=[sd] * 4,
        compiler_params=_cparams(("arbitrary",)),
        name="inproj_sample",
    )(x, mod, mod, g, w_bf, qg, kg)


ATTN_PHASES = len(DILATIONS) + 1


def _attn_rider(q_ref, kc_ref, vc_ref, bias_ref, o_ref, kbuf, vbuf, acc_ref, m_ref, l_ref, *, n_super, side):
    step = pl.program_id(0)
    phase = step % ATTN_PHASES
    st = (step // ATTN_PHASES) % n_super
    lane = lax.broadcasted_iota(jnp.int32, (1, LANES), 1)
    lo = lane < HEAD_DIM
    nt_contract = (((1,), (1,)), ((), ()))
    n_groups = SUPER // QB // TILE_GROUP

    def pattern(p, d, groups, with_side):
        per_res = SUPER // d // QB
        if p == 0:
            for buf, cur_ref in ((kbuf, kc_ref), (vbuf, vc_ref)):
                @pl.when(st == 0)
                def _(buf=buf):
                    buf[0:SUPER, :] = jnp.zeros((SUPER, LANES), F32)

                @pl.when(st > 0)
                def _(buf=buf):
                    buf[0:SUPER, :] = buf[SUPER:2 * SUPER, :]

                buf[SUPER:2 * SUPER, :] = cur_ref[...]

        for gi in groups:
            tiles = []
            for u in range(TILE_GROUP):
                ti = gi * TILE_GROUP + u
                r = ti // per_res
                jt = ti % per_res
                row0 = r + d * QB * jt
                ks = SUPER + row0 - d * QB
                bias = bias_ref[jnp.where(st == 0, 0, 1)] if jt == 0 else bias_ref[1]
                q = q_ref[pl.ds(row0, QB, stride=d), :]
                k = kbuf[pl.ds(ks, 2 * QB, stride=d), :].astype(BF16)
                v = vbuf[pl.ds(ks, 2 * QB, stride=d), :].astype(BF16)
                v = jnp.concatenate([v, jnp.ones_like(v)], axis=1)
                q2 = jnp.concatenate([jnp.where(lo, q, 0.0), jnp.where(lo, 0.0, q)], axis=0).astype(BF16)
                s = lax.dot_general(q2, k, nt_contract, preferred_element_type=F32)
                tiles.append((s, bias, v, row0))
            probs = []
            for s, bias, v, row0 in tiles:
                s = s + jnp.concatenate([bias, bias], axis=0)
                m = jnp.max(s, axis=-1, keepdims=True)
                pr = jnp.exp2(s - m)
                probs.append((pr.astype(BF16), m))
            for (s, bias, v, row0), (pr, m) in zip(tiles, probs):
                pv = jnp.dot(pr, v, preferred_element_type=F32)
                rows = pl.ds(row0, QB, stride=d)
                acc_ref[p, rows, :] = jnp.where(lo, pv[0:QB, 0:LANES], pv[QB:2 * QB, 0:LANES])
                m_ref[p, rows, :] = jnp.where(lo, m[0:QB], m[QB:2 * QB])
                l_ref[p, rows, :] = jnp.where(lo, pv[0:QB, LANES:2 * LANES], pv[QB:2 * QB, LANES:2 * LANES])
            if with_side and gi < SIDE_PARTS:
                side(gi)

    last = len(DILATIONS) - 1

    def phase_work(p):
        d = DILATIONS[p][1]
        pattern(p, d, range(n_groups - 1) if p == last else range(n_groups), True)
        if p == last - 1:
            pattern(last, DILATIONS[last][1], range(n_groups - 1, n_groups), False)

    for p in range(len(DILATIONS)):
        pl.when(phase == p)(functools.partial(phase_work, p))

    @pl.when(phase == ATTN_PHASES - 1)
    def _():
        n_merge = SUPER // QB
        for ci in range(n_merge):
            rows = pl.ds(ci * QB, QB)
            m1, m2, m3 = m_ref[0, rows, :], m_ref[1, rows, :], m_ref[2, rows, :]
            mm = jnp.maximum(jnp.maximum(m1, m2), m3)
            w1, w2, w3 = jnp.exp2(m1 - mm), jnp.exp2(m2 - mm), jnp.exp2(m3 - mm)
            num = w1 * acc_ref[0, rows, :] + w2 * acc_ref[1, rows, :] + w3 * acc_ref[2, rows, :]
            den = w1 * l_ref[0, rows, :] + w2 * l_ref[1, rows, :] + w3 * l_ref[2, rows, :]
            o_ref[rows, :] = num / den
            if ci % (n_merge // SIDE_PARTS) == 0 and ci // (n_merge // SIDE_PARTS) < SIDE_PARTS:
                side(ci // (n_merge // SIDE_PARTS))


def _attn_bias():
    qi = np.arange(QB)[:, None]
    ki = np.arange(2 * QB)[None, :]
    dist = qi + QB - ki
    normal = (dist >= 0) & (dist <= N_BACK)
    first = normal & (ki >= QB)
    return jnp.asarray(np.where(np.stack([first, normal]), 0.0, NEG), F32)


SA_HEADS = 4
SA_WIDTH = SA_HEADS * HEAD_DIM
N_SA_IN = 7


RING = 3
SIDE_PARTS = 3


def _stream_kernel(*refs, rider, rider_takes_side, batch_lo, n_rider_in, n_rider_out, n_prev, n_rider_scratch):
    kt_hbm, vt_hbm, qx_ref, kn_ref, vn_ref, w_ref, hm_ref = refs[:N_SA_IN]
    rider_in = refs[N_SA_IN:N_SA_IN + n_rider_in]
    n_in = N_SA_IN + n_rider_in + n_prev
    okt_ref, ovt_ref, o_ref = refs[n_in:n_in + 3]
    rider_out = refs[n_in + 3:n_in + 3 + n_rider_out]
    pad_ref, newt_ref, kring, vring, sems = refs[n_in + 3 + n_rider_out:n_in + 8 + n_rider_out]
    rider_scratch = refs[len(refs) - n_rider_scratch:] if n_rider_scratch else ()
    n_buf = kring.shape[2]
    t_new = kn_ref.shape[0]
    n_tiles = n_buf // LANES

    step = pl.program_id(0)
    n_steps = pl.num_programs(0)

    def block_copy(i, s):
        hbm, ring = ((kt_hbm, kring), (vt_hbm, vring))[i]
        src = hbm.at[batch_lo + s // SA_NBLK, pl.ds((s % SA_NBLK) * SA_WIDTH, SA_WIDTH), :]
        return pltpu.make_async_copy(src, ring.at[s % RING], sems.at[i, s % RING])

    @pl.when(step == 0)
    def _():
        for s in range(RING - 1):
            for i in range(2):
                block_copy(i, s).start()

    @pl.when(step + RING - 1 < n_steps)
    def _():
        for i in range(2):
            block_copy(i, step + RING - 1).start()

    for i in range(2):
        block_copy(i, step).wait()
    kt_ref = kring.at[step % RING]
    vt_ref = vring.at[step % RING]

    def sample_attention():
        for i, new_ref in enumerate((kn_ref, vn_ref)):
            pad_ref[...] = jnp.zeros_like(pad_ref)
            pad_ref[0:t_new, :] = new_ref[...]
            newt_ref[i] = pad_ref[...].T
        k_ext = jnp.concatenate([kt_ref[...], newt_ref[0]], axis=1)
        v_ext = jnp.concatenate([vt_ref[...], newt_ref[1]], axis=1)
        w = w_ref[...]
        s = jnp.dot(qx_ref[...], k_ext.astype(BF16), preferred_element_type=F32)
        s = jnp.where(w > 0.0, s, NEG)
        m = jnp.max(s, axis=-1, keepdims=True)
        e = w * jnp.exp2(s - m)
        l = jnp.sum(e, axis=-1, keepdims=True)
        pv = lax.dot_general(e.astype(BF16), v_ext.astype(BF16), (((1,), (1,)), ((), ())),
                             preferred_element_type=F32)
        pv = pv * hm_ref[...] / l
        o = pv[0:t_new]
        for h in range(1, SA_HEADS):
            o = o + pv[h * t_new:(h + 1) * t_new]
        o_ref[...] = o

    boundary = {}

    def shift(part, n_parts, fresh):
        per = n_tiles // n_parts
        first = part * per
        keep = lax.broadcasted_iota(jnp.int32, (1, LANES), 1) < LANES - t_new
        if fresh:
            boundary.clear()
        for i, (src_ref, dst_ref) in enumerate(((kt_ref, okt_ref), (vt_ref, ovt_ref))):
            def rotated(j):
                if (i, j) in boundary:
                    return boundary.pop((i, j))
                t = src_ref[:, j * LANES:(j + 1) * LANES] if j < n_tiles else newt_ref[i]
                return pltpu.roll(t, LANES - t_new, axis=1)
            rolled = [rotated(j) for j in range(first, first + per + 1)]
            for j in range(per):
                dst_ref[:, (first + j) * LANES:(first + j + 1) * LANES] = jnp.where(keep, rolled[j], rolled[j + 1])
            if part + 1 < n_parts:
                boundary[(i, first + per)] = rolled[per]

    sample_attention()
    if rider_takes_side:
        shift(0, SIDE_PARTS + 1, fresh=True)
        rider(*rider_in, *rider_out, *rider_scratch,
              side=lambda j: shift(j + 1, SIDE_PARTS + 1, fresh=(j == 0)))
    else:
        shift(0, 1, fresh=True)
        rider(*rider_in, *rider_out, *rider_scratch)


def _sa_weights(t_new, n_buf):
    ext = n_buf + LANES
    pos = np.arange(ext)[None, :]
    i = np.arange(t_new)[:, None]
    delta = n_buf + i - pos
    is_real = (pos < n_buf + t_new)
    w = np.zeros((t_new, ext), np.float32)
    for window, d in DILATIONS:
        w += ((delta >= 0) & (delta % d == 0) & (delta <= window) & is_real)
    return jnp.asarray(np.tile(w, (SA_HEADS, 1)), F32)


def _head_mask(t_new):
    h = np.repeat(np.arange(SA_HEADS), t_new)[:, None]
    c = np.arange(SA_WIDTH)[None, :] // HEAD_DIM
    return (h == c).astype(np.float32)


SA_NBLK = ATT_WIDTH // SA_WIDTH


def _stream(kt, vt, prev, batch_lo, n_batches, qx, k_new, v_new, rider, rider_args, rider_in_specs,
            rider_out_specs, rider_out_shapes, name, rider_scratch=(), rider_takes_side=False):
    bd, _, n_buf = kt.shape
    t_new = k_new.shape[1]
    assert n_buf == MAX_SPAN, "window buffer must cover the longest pattern (no invalid positions)"
    rows = t_new * SA_HEADS
    buf = pl.BlockSpec((None, SA_WIDTH, n_buf), lambda i: (batch_lo + i // SA_NBLK, i % SA_NBLK, 0))
    new = pl.BlockSpec((None, t_new, SA_WIDTH), lambda i: (batch_lo + i // SA_NBLK, 0, i % SA_NBLK))
    o_blk = pl.BlockSpec((None, t_new, SA_WIDTH), lambda i: (i // SA_NBLK, 0, i % SA_NBLK))
    qx_blk = pl.BlockSpec((None, None, rows, SA_WIDTH), lambda i: (batch_lo + i // SA_NBLK, i % SA_NBLK, 0, 0))
    sd_buf = jax.ShapeDtypeStruct((bd, ATT_WIDTH, n_buf), F32)
    hbm = pl.BlockSpec(memory_space=pl.ANY)
    in_specs = [hbm, hbm, qx_blk, new, new, _const_spec((rows, n_buf + LANES)), _const_spec((rows, SA_WIDTH))]
    in_specs += list(rider_in_specs)
    args = [kt, vt, qx, k_new, v_new, _sa_weights(t_new, n_buf), jnp.asarray(_head_mask(t_new), F32)]
    args += list(rider_args)
    aliases = {}
    if prev is not None:
        aliases = {len(args): 0, len(args) + 1: 1}
        in_specs += [pl.BlockSpec(memory_space=pl.ANY)] * 2
        args += list(prev)
    return pl.pallas_call(
        functools.partial(_stream_kernel, rider=rider, rider_takes_side=rider_takes_side, batch_lo=batch_lo,
                          n_rider_in=len(rider_in_specs),
                          n_rider_out=len(rider_out_specs), n_prev=len(aliases),
                          n_rider_scratch=len(rider_scratch)),
        grid=(n_batches * SA_NBLK,),
        in_specs=in_specs,
        out_specs=[buf, buf, o_blk] + list(rider_out_specs),
        out_shape=[sd_buf, sd_buf, jax.ShapeDtypeStruct((n_batches, t_new, ATT_WIDTH), F32)] + list(rider_out_shapes),
        input_output_aliases=aliases,
        scratch_shapes=[pltpu.VMEM((LANES, SA_WIDTH), F32), pltpu.VMEM((2, SA_WIDTH, LANES), F32),
                        pltpu.VMEM((RING, SA_WIDTH, n_buf), F32), pltpu.VMEM((RING, SA_WIDTH, n_buf), F32),
                        pltpu.SemaphoreType.DMA((2, RING))] + list(rider_scratch),
        compiler_params=_cparams(("arbitrary",)),
        name=name,
    )(*args)


def _expand_queries(q):
    bd, t_new, _ = q.shape
    hm = jnp.asarray(_head_mask(t_new), F32)
    qb = q.reshape(bd, t_new, SA_NBLK, SA_WIDTH).transpose(0, 2, 1, 3)
    return (jnp.tile(qb, (1, 1, SA_HEADS, 1)) * hm).astype(BF16)


def _rider_tiling(b, s, steps):
    assert (b * s) % steps == 0
    tm = b * s // steps
    assert tm % LANES == 0 and s % tm == 0
    return tm, s // tm


def _prompt_mod_spec(per_seq, col):
    return pl.BlockSpec((None, 1, D_MODEL), lambda i: (i // per_seq, 0, col))


def _inproj_prompt(x, mod, g, w_bf, qg, kg):
    b, s, _ = x.shape
    tm = TOKEN_TILE
    per_seq = s // tm
    keep = min(MAX_SPAN, s)
    kt_first = per_seq - keep // tm
    tok = lambda w: pl.BlockSpec((tm, w), lambda i: (i, 0))
    out_t =pl.BlockSpec((None, ATT_WIDTH, tm), lambda i: (i // per_seq, 0, jnp.maximum(i % per_seq - kt_first, 0)))
    sd = jax.ShapeDtypeStruct((b * s, ATT_WIDTH), F32)
    sdt = jax.ShapeDtypeStruct((b, ATT_WIDTH, keep), F32)
    q, k, v, u, kt_p, vt_p = pl.pallas_call(
        functools.partial(_inproj_body, kt_rule=(per_seq, kt_first)),
        grid=(b * per_seq,),
        in_specs=[tok(D_MODEL), _prompt_mod_spec(per_seq, MOD_SCALE1), _prompt_mod_spec(per_seq, MOD_SHIFT1),
                  _const_spec((1, D_MODEL)), _const_spec((D_MODEL, 4 * ATT_WIDTH)),
                  _const_spec((1, ATT_WIDTH)), _const_spec((1, ATT_WIDTH))],
        out_specs=[tok(ATT_WIDTH)] * 4 + [out_t] * 2,
        out_shape=[sd] * 4 + [sdt] * 2,
        compiler_params=_cparams(("arbitrary",)),
        name="inproj_prompt",
    )(x.reshape(b * s, D_MODEL), mod, mod, g, w_bf, qg, kg)
    r3 = lambda a: a.reshape(b, s, ATT_WIDTH)
    return r3(q), r3(k), r3(v), r3(u), kt_p, vt_p


def _attn_stream_batches(b, s):
    n_super = s // SUPER
    units = b * (ATT_WIDTH // LANES) * n_super
    steps = units * ATTN_PHASES
    assert steps % SA_NBLK == 0
    return steps // SA_NBLK


def _stream_attn(kt, vt, batch_lo, n_batches, qx, k_new, v_new, q, k, v):
    b, s, _ = q.shape
    assert s % SUPER == 0
    n_super = s // SUPER
    n_hp = ATT_WIDTH // LANES
    assert n_batches == _attn_stream_batches(b, s)

    n_units = b * n_hp * n_super

    def unit_block(lead):
        def index(i):
            un = jnp.minimum((i + lead) // ATTN_PHASES, n_units - 1)
            return un // (n_hp * n_super), un % n_super, (un // n_super) % n_hp
        return pl.BlockSpec((None, SUPER, LANES), index)

    outs = _stream(
        kt, vt, None, batch_lo, n_batches, qx, k_new, v_new,
        functools.partial(_attn_rider, n_super=n_super),
        (q, k, v, _attn_bias()),
        [unit_block(1), unit_block(ATTN_PHASES - 1), unit_block(ATTN_PHASES - 1), _const_spec((2, QB, 2 * QB))],
        [unit_block(0)], [jax.ShapeDtypeStruct((b, s, ATT_WIDTH), F32)], "stream_attn",
        rider_scratch=[pltpu.VMEM((2 * SUPER, LANES), F32)] * 2 + [pltpu.VMEM((len(DILATIONS), SUPER, LANES), F32)] * 3,
        rider_takes_side=True)
    okt, ovt, o_s, o_att = outs
    return (okt, ovt), o_s, o_att


def _stream_post(kt, vt, prev, batch_lo, n_batches, qx, k_new, v_new, x, o_att, y_ssm, mod, consts, weights):
    b, s, _ = x.shape
    tm, per_seq = _rider_tiling(b, s, n_batches * SA_NBLK)
    tok = lambda w: pl.BlockSpec((tm, w), lambda i: (i, 0))
    mods = [_prompt_mod_spec(per_seq, c) for c in (MOD_GATE1, MOD_SCALE2, MOD_SHIFT2, MOD_GATE2)]
    okt, ovt, o_s, y = _stream(
        kt, vt, prev, batch_lo, n_batches, qx, k_new, v_new, _post_body,
        (x.reshape(b * s, D_MODEL), o_att.reshape(b * s, ATT_WIDTH), y_ssm.reshape(b * s, SSM_WIDTH),
         mod, mod, mod, mod, *consts, *weights),
        [tok(D_MODEL), tok(ATT_WIDTH), tok(SSM_WIDTH)] + mods
        + [_const_spec(c.shape) for c in consts] + [_const_spec(w.shape) for w in weights],
        [tok(D_MODEL)], [jax.ShapeDtypeStruct((b * s, D_MODEL), F32)], "stream_post")
    return okt, ovt, o_s, y.reshape(b, s, D_MODEL)


def _s5_tables(a_re, a_im, log_dt, b_re, b_im):
    g, n = a_re.shape
    dt = jnp.exp(log_dt)[:, None]
    x, y = dt * a_re, dt * a_im
    ks = jnp.arange(CHUNK + 1, dtype=F32)
    mag = jnp.exp(x[:, :, None] * ks)
    pw_re, pw_im = mag * jnp.cos(y[:, :, None] * ks), mag * jnp.sin(y[:, :, None] * ks)
    e_re = jnp.expm1(x) * jnp.cos(y) - 2.0 * jnp.sin(0.5 * y) ** 2
    e_im = jnp.exp(x) * jnp.sin(y)
    den = a_re * a_re + a_im * a_im
    f_re = (e_re * a_re + e_im * a_im) / den
    f_im = (e_im * a_re - e_re * a_im) / den
    bb_re = f_re[..., None] * b_re - f_im[..., None] * b_im
    bb_im = f_re[..., None] * b_im + f_im[..., None] * b_re
    return dict(pw_re=pw_re, pw_im=pw_im, bb_re=bb_re, bb_im=bb_im)


def _s5_chunk_operators(tb, c_re, c_im, d_skip):
    pw_re, pw_im, bb_re, bb_im = tb["pw_re"], tb["pw_im"], tb["bb_re"], tb["bb_im"]
    g, n, ch = bb_re.shape
    L = CHUNK
    hp = lax.Precision.HIGHEST
    ct_re, ct_im = c_re.transpose(0, 2, 1), c_im.transpose(0, 2, 1)
    pr, pi = pw_re[:, :, :, None], pw_im[:, :, :, None]
    ca_re = ct_re[:, :, None, :] * pr - ct_im[:, :, None, :] * pi
    ca_im = ct_re[:, :, None, :] * pi + ct_im[:, :, None, :] * pr
    a_cat = jnp.concatenate([ca_re[:, :, :L].reshape(g, n, L * ch), -ca_im[:, :, :L].reshape(g, n, L * ch)], axis=1)
    b_cat = jnp.concatenate([bb_re, bb_im], axis=1)
    base = jnp.einsum("gnd,gnx->gdx", b_cat, a_cat, precision=hp)
    skip = jnp.eye(ch, dtype=F32)[None] * d_skip.reshape(g, ch, 1)
    base = base + jnp.pad(skip, ((0, 0), (0, 0), (0, (L - 1) * ch)))
    toe = jnp.stack([jnp.pad(base, ((0, 0), (0, 0), (s * ch, 0)))[:, :, :L * ch] for s in range(L)], axis=1)
    toe = toe.reshape(g, L * ch, L * ch)
    rv_re = pw_re[:, :, L - 1::-1].transpose(0, 2, 1)[:, :, None, :]
    rv_im = pw_im[:, :, L - 1::-1].transpose(0, 2, 1)[:, :, None, :]
    bt_re, bt_im = bb_re.transpose(0, 2, 1)[:, None], bb_im.transpose(0, 2, 1)[:, None]
    ws_re = (rv_re * bt_re - rv_im * bt_im).reshape(g, L * ch, n)
    ws_im = (rv_re * bt_im + rv_im * bt_re).reshape(g, L * ch, n)
    vr = ca_re[:, :, 1:].reshape(g, n, L * ch)
    vi = (-ca_im[:, :, 1:]).reshape(g, n, L * ch)
    gp = g // 2
    z_w = jnp.zeros((gp, L * ch, n), F32)
    wr, wi = ws_re.reshape(gp, 2, L * ch, n), ws_im.reshape(gp, 2, L * ch, n)
    w_pair = jnp.concatenate([
        jnp.concatenate([wr[:, 0], z_w, wi[:, 0], z_w], axis=2),
        jnp.concatenate([z_w, wr[:, 1], z_w, wi[:, 1]], axis=2)], axis=1)
    z_v = jnp.zeros((gp, n, L * ch), F32)
    vr, vi = vr.reshape(gp, 2, n, L * ch), vi.reshape(gp, 2, n, L * ch)
    v_pair = jnp.concatenate([
        jnp.concatenate([vr[:, 0], z_v], axis=2), jnp.concatenate([z_v, vr[:, 1]], axis=2),
        jnp.concatenate([vi[:, 0], z_v], axis=2), jnp.concatenate([z_v, vi[:, 1]], axis=2)], axis=1)
    a16_re = pw_re[:, :, L].reshape(1, g * n)
    a16_im = pw_im[:, :, L].reshape(1, g * n)
    return toe.astype(BF16), w_pair.astype(BF16), v_pair.astype(BF16), a16_re, a16_im


GROUPS_PER_BLOCK = LANES // SSM_CH
REGROUP_ROWS = 32


def _granule_transpose(arrs):
    n = GROUPS_PER_BLOCK
    gran = lax.broadcasted_iota(jnp.int32, (1, LANES), 1) // SSM_CH
    skew = [a if t == 0 else pltpu.roll(a, t * SSM_CH, axis=1) for t, a in enumerate(arrs)]
    out = []
    for p in range(n):
        c = skew[(-p) % n]
        for g in range(1, n):
            c = jnp.where(gran == g, skew[(g - p) % n], c)
        out.append(c if p == 0 else pltpu.roll(c, LANES - p * SSM_CH, axis=1))
    return out


def _ssm_a_kernel(u_ref, w_ref, xg_ref, sre_ref, sim_ref, xs_ref):
    nb, s_len, _ = u_ref.shape
    n_chunks = s_len // CHUNK
    half = CHUNK // 2

    for b in range(nb):
        def regroup(ct, c, b=b):
            c0 = pl.multiple_of(ct * REGROUP_ROWS, REGROUP_ROWS)
            rows = pl.ds(b * n_chunks + c0, REGROUP_ROWS)
            for j in range(2):
                z = [u_ref[b, pl.ds(c0 * CHUNK + half * j + tt, REGROUP_ROWS, stride=CHUNK), :]
                     for tt in range(half)]
                for p, xp in enumerate(_granule_transpose(z)):
                    xs_ref[2 * p + j, rows, :] = xp
            return c
        lax.fori_loop(0, n_chunks // REGROUP_ROWS, regroup, 0)

    for col in range(2 * GROUPS_PER_BLOCK):
        xg_ref[:, col * LANES:(col + 1) * LANES] = xs_ref[col].astype(BF16)
    for pp in range(GROUPS_PER_BLOCK // 2):
        xp = xg_ref[:, pp * 4 * LANES:(pp + 1) * 4 * LANES]
        st = jnp.dot(xp, w_ref[pp], preferred_element_type=F32)
        sre_ref[pp] = st[:, 0:LANES]
        sim_ref[pp] = st[:, LANES:2 * LANES]


def _ssm_scan_kernel(sre_ref, sim_ref, are_ref, aim_ref, hre_ref, him_ref, fre_ref, fim_ref, *, nb):
    ncb, rows, _ = sre_ref.shape
    n_chunks = rows // nb
    a_re = [are_ref[cb] for cb in range(ncb)]
    a_im = [aim_ref[cb] for cb in range(ncb)]

    def body(c, carry):
        r = pl.ds(c, nb, stride=n_chunks)
        out = []
        for cb in range(ncb):
            h_re, h_im = carry[2 * cb], carry[2 * cb + 1]
            hre_ref[cb, r, :] = h_re
            him_ref[cb, r, :] = h_im
            out.append(a_re[cb] * h_re - a_im[cb] * h_im + sre_ref[cb, r, :])
            out.append(a_re[cb] * h_im + a_im[cb] * h_re + sim_ref[cb, r, :])
        return tuple(out)

    z = jnp.zeros((nb, LANES), F32)
    fin = lax.fori_loop(0, n_chunks, body, (z,) * (2 * ncb), unroll=4)
    for cb in range(ncb):
        fre_ref[cb] = fin[2 * cb]
        fim_ref[cb] = fin[2 * cb + 1]


def _ssm_c_kernel(xg_ref, hre_ref, him_ref, t_ref, v_ref, y_ref, ys_ref):
    nb, s_len, _ = y_ref.shape
    n_chunks = s_len // CHUNK
    half = CHUNK // 2
    for pp in range(GROUPS_PER_BLOCK // 2):
        hcat = jnp.concatenate([hre_ref[pp], him_ref[pp]], axis=1).astype(BF16)
        inter = jnp.dot(hcat, v_ref[pp], preferred_element_type=F32)
        for e in range(2):
            g = 2 * pp + e
            intra = jnp.dot(xg_ref[:, g * 2 * LANES:(g + 1) * 2 * LANES], t_ref[g],
                            preferred_element_type=F32)
            yg = intra + inter[:, e * 2 * LANES:(e + 1) * 2 * LANES]
            ys_ref[2 * g] = yg[:, 0:LANES]
            ys_ref[2 * g + 1] = yg[:, LANES:2 * LANES]

    for b in range(nb):
        def regroup(ct, c, b=b):
            c0 = pl.multiple_of(ct * REGROUP_ROWS, REGROUP_ROWS)
            rows = pl.ds(b * n_chunks + c0, REGROUP_ROWS)
            for j in range(2):
                yp =[ys_ref[2 * p + j, rows, :] for p in range(GROUPS_PER_BLOCK)]
                for tt, zt in enumerate(_granule_transpose(yp)):
                    y_ref[b, pl.ds(c0 * CHUNK + half * j + tt, REGROUP_ROWS, stride=CHUNK), :] = zt
            return c
        lax.fori_loop(0, n_chunks // REGROUP_ROWS, regroup, 0)


def _ssm_prompt(u, toe, w_pair, v_pair, a16_re, a16_im):
    b, s, _ = u.shape
    rows = b * (s // CHUNK)
    nblk = SSM_WIDTH // LANES
    gw = GROUPS_PER_BLOCK * CHUNK * SSM_CH
    ncb = GROUPS_PER_BLOCK * SSM_STATE // LANES
    ublk = pl.BlockSpec((b, s, LANES), lambda j: (0, 0, j))
    sblk = pl.BlockSpec((ncb, rows, LANES), lambda j: (j, 0, 0))
    sd_h = jax.ShapeDtypeStruct((nblk * ncb, rows, LANES), F32)
    xg, s_re, s_im = pl.pallas_call(
        _ssm_a_kernel,
        grid=(nblk,),
        in_specs=[ublk, pl.BlockSpec((GROUPS_PER_BLOCK // 2, 4 * LANES, 2 * LANES), lambda j: (j, 0, 0))],
        out_specs=[pl.BlockSpec((rows, gw), lambda j: (0, j)), sblk, sblk],
        out_shape=[jax.ShapeDtypeStruct((rows, nblk * gw), BF16), sd_h, sd_h],
        scratch_shapes=[pltpu.VMEM((gw // LANES, rows, LANES), F32)],
        compiler_params=_cparams(("arbitrary",)),
        name="ssm_chunk_states",
    )(u, w_pair)

    ablk = pl.BlockSpec((ncb, 1, LANES), lambda j: (j, 0, 0))
    fblk = pl.BlockSpec((ncb, b, LANES), lambda j: (j, 0, 0))
    sd_f = jax.ShapeDtypeStruct((nblk * ncb, b, LANES), F32)
    a16_re = a16_re.reshape(nblk * ncb, 1, LANES)
    a16_im = a16_im.reshape(nblk * ncb, 1, LANES)
    h_re, h_im, f_re, f_im = pl.pallas_call(
        functools.partial(_ssm_scan_kernel, nb=b),
        grid=(nblk,),
        in_specs=[sblk, sblk, ablk, ablk],
        out_specs=[sblk, sblk, fblk, fblk],
        out_shape=[sd_h, sd_h, sd_f, sd_f],
        compiler_params=_cparams(("arbitrary",)),
        name="ssm_scan",
    )(s_re, s_im, a16_re, a16_im)

    y = pl.pallas_call(
        _ssm_c_kernel,
        grid=(nblk,),
        in_specs=[pl.BlockSpec((rows, gw), lambda j: (0, j)), sblk, sblk,
                  pl.BlockSpec((GROUPS_PER_BLOCK, 2 * LANES, 2 * LANES), lambda j: (j, 0, 0)),
                  pl.BlockSpec((GROUPS_PER_BLOCK // 2, 2 * LANES, 4 * LANES), lambda j: (j, 0, 0))],
        out_specs=ublk,
        out_shape=jax.ShapeDtypeStruct((b, s, SSM_WIDTH), F32),
        scratch_shapes=[pltpu.VMEM((gw // LANES, rows, LANES), F32)],
        compiler_params=_cparams(("arbitrary",)),
        name="ssm_outputs",
    )(xg, h_re, h_im, toe, v_pair)
    to_rows = lambda f: jnp.transpose(f, (1, 0, 2)).reshape(b, nblk * ncb * LANES)
    return y, to_rows(f_re), to_rows(f_im)


def _ssm_sample_kernel(u_ref, hre_ref, him_ref, are_ref, aim_ref, bre_ref, bim_ref, cre_ref, cim_ref, d_ref,
                       y_ref, ore_ref, oim_ref, *, t_new):
    h_re = hre_ref[...].T
    h_im = him_ref[...].T
    a_re, a_im = are_ref[...], aim_ref[...]
    bd = h_re.shape[0]
    for t in range(t_new):
        u = u_ref[t]
        ub = u.astype(BF16)
        n_re = a_re * h_re - a_im * h_im + jnp.dot(ub, bre_ref[...], preferred_element_type=F32)
        n_im = a_re * h_im + a_im * h_re + jnp.dot(ub, bim_ref[...], preferred_element_type=F32)
        h_re, h_im = n_re, n_im
        y = (jnp.dot(h_re.astype(BF16), cre_ref[...], preferred_element_type=F32)
             + jnp.dot(h_im.astype(BF16), cim_ref[...], preferred_element_type=F32) + d_ref[...] * u)
        y_ref[t] = y
    ore_ref[...] = h_re.T
    oim_ref[...] = h_im.T


def _block_diag(m):
    g, r, c = m.shape
    spread = jnp.asarray(np.tile(np.eye(c, dtype=np.float32), (1, g)), BF16)
    tiled = jnp.dot(m.reshape(g * r, c).astype(BF16), spread, preferred_element_type=F32)
    keep = np.arange(g * r)[:, None] // r == np.arange(g * c)[None, :] // c
    return jnp.where(jnp.asarray(keep), tiled, 0.0).astype(BF16)


def _ssm_sample(u, h0_re_t, h0_im_t, tb, c_re, c_im, d_skip, t_new):
    gn = SSM_GROUPS * SSM_STATE
    a_re = tb["pw_re"][:, :, 1].reshape(1, gn)
    a_im = tb["pw_im"][:, :, 1].reshape(1, gn)
    b_re = _block_diag(tb["bb_re"].transpose(0, 2, 1))
    b_im = _block_diag(tb["bb_im"].transpose(0, 2, 1))
    cb_re = _block_diag(c_re.transpose(0, 2, 1))
    cb_im = _block_diag(-c_im.transpose(0, 2, 1))
    bd = u.shape[1]
    full = lambda shape: pl.BlockSpec(shape, lambda i: (0,) * len(shape))
    return pl.pallas_call(
        functools.partial(_ssm_sample_kernel, t_new=t_new),
        grid=(1,),
        in_specs=[full((t_new, bd, SSM_WIDTH)), full((gn, bd)), full((gn, bd)), full((1, gn)), full((1, gn)),
                  full((SSM_WIDTH, gn)), full((SSM_WIDTH, gn)), full((gn, SSM_WIDTH)), full((gn, SSM_WIDTH)),
                  full((1, SSM_WIDTH))],
        out_specs=[full((t_new, bd, SSM_WIDTH)), full((gn, bd)), full((gn, bd))],
        out_shape=[jax.ShapeDtypeStruct((t_new, bd, SSM_WIDTH), F32),
                   jax.ShapeDtypeStruct((gn, bd), F32), jax.ShapeDtypeStruct((gn, bd), F32)],
        compiler_params=_cparams(("arbitrary",)),
        name="ssm_sample",
    )(u, h0_re_t, h0_im_t, a_re, a_im, b_re, b_im, cb_re, cb_im, d_skip.reshape(1, SSM_WIDTH))


def _rms(x, gain):
    return x * lax.rsqrt(jnp.mean(x * x, axis=-1, keepdims=True) + EPS) * gain


def _gelu_tanh(x):
    return 0.5 * x * (1.0 + jnp.tanh(math.sqrt(2.0 / math.pi) * (x + 0.044715 * (x * x * x))))


def _post_body(x_ref, oa_ref, ys_ref, g1_ref, sc2_ref, sh2_ref, g2_ref, n2_ref, ag_ref, sg_ref,
               wglu_ref, wout_ref, wg_ref, wu_ref, wd_ref, o_ref):
    ya = _gelu_tanh(ys_ref[...])
    ya = ya * _sigmoid(jnp.dot(ya.astype(BF16), wglu_ref[...], preferred_element_type=F32))
    merged = jnp.concatenate([_rms(oa_ref[...], ag_ref[...]), _rms(ya, sg_ref[...])], axis=1)
    x1 = x_ref[...] + g1_ref[...] * jnp.dot(merged.astype(BF16), wout_ref[...], preferred_element_type=F32)
    h2 = (_rms(x1, n2_ref[...]) * (1.0 + sc2_ref[...]) + sh2_ref[...]).astype(BF16)
    gate = jnp.dot(h2, wg_ref[...], preferred_element_type=F32)
    up = jnp.dot(h2, wu_ref[...], preferred_element_type=F32)
    act = (gate * _sigmoid(gate) * up).astype(BF16)
    o_ref[...] = x1 + g2_ref[...] * jnp.dot(act, wd_ref[...], preferred_element_type=F32)


def _post_sample(x, o_att, y_ssm, mod, bd, consts, weights):
    n = x.shape[0]
    tok = lambda w: pl.BlockSpec((bd, w), lambda t: (t, 0))
    mods = [_sample_mod_spec(bd, c) for c in (MOD_GATE1, MOD_SCALE2, MOD_SHIFT2, MOD_GATE2)]
    return pl.pallas_call(
        _post_body,
        grid=(n // bd,),
        in_specs=[tok(D_MODEL), tok(ATT_WIDTH), tok(SSM_WIDTH)] + mods
                 + [_const_spec(c.shape) for c in consts] + [_const_spec(w.shape) for w in weights],
        out_specs=tok(D_MODEL),
        out_shape=jax.ShapeDtypeStruct((n, D_MODEL), F32),
        compiler_params=_cparams(("arbitrary",)),
        name="post_sample",
    )(x, o_att, y_ssm, mod, mod, mod, mod, *consts, *weights)


def kernel(x_prompt, x_sample, cache_k, cache_v, state_ssm_re, state_ssm_im, c_prompt, c_sample, norm1_g, norm2_g, w_ada, b_ada, w_in, q_gain, k_gain, ssm_a_re, ssm_a_im, ssm_log_dt, ssm_b_re, ssm_b_im, ssm_c_re, ssm_c_im, ssm_d, w_glu, attn_out_g, ssm_out_g, w_out, w_gate, w_up, w_down):
    depth = norm1_g.shape[0]
    assert depth == 1, "one decoder layer"
    b, s, _ = x_prompt.shape
    bd, t_new, _ = x_sample.shape
    n_buf = cache_k.shape[2]
    L = 0

    assert bd % SUBLANES == 0
    pad = (-(b + bd)) % SUBLANES
    c_all = jnp.concatenate([c_sample, c_prompt, jnp.zeros((pad, D_MODEL), F32)], axis=0)
    mod = _ada(c_all, w_ada[L], b_ada[L].reshape(1, -1))
    mod_p = mod[bd:bd + b].reshape(b, 1, N_MOD * D_MODEL)

    w_in_bf = w_in[L].astype(BF16)
    n1 = norm1_g[L].reshape(1, D_MODEL)
    qg = jnp.tile(q_gain[L], ATT_HEADS).reshape(1, ATT_WIDTH)
    kg = jnp.tile(k_gain[L], ATT_HEADS).reshape(1, ATT_WIDTH)
    consts = (norm2_g[L].reshape(1, D_MODEL), attn_out_g[L].reshape(1, ATT_WIDTH), ssm_out_g[L].reshape(1, SSM_WIDTH))
    weights = tuple(w[L].astype(BF16) for w in (w_glu, w_out, w_gate, w_up, w_down))

    tb = _s5_tables(ssm_a_re[L], ssm_a_im[L], ssm_log_dt[L], ssm_b_re[L], ssm_b_im[L])
    toe, w_pair, v_pair, a16_re, a16_im = _s5_chunk_operators(tb, ssm_c_re[L], ssm_c_im[L], ssm_d[L])

    xs = jnp.transpose(x_sample, (1, 0, 2)).reshape(t_new * bd, D_MODEL)
    qs, ks, vs, us = _inproj_sample(xs, mod, bd, n1, w_in_bf, qg, kg)
    batch_major = lambda a: jnp.transpose(a.reshape(t_new, bd, ATT_WIDTH), (1, 0, 2))
    qx = _expand_queries(batch_major(qs))
    ks3, vs3 = batch_major(ks), batch_major(vs)
    kt = jnp.transpose(cache_k[L], (0, 2, 3, 1)).reshape(bd, ATT_WIDTH, n_buf)
    vt = jnp.transpose(cache_v[L], (0, 2, 3, 1)).reshape(bd, ATT_WIDTH, n_buf)
    q, k, v, u, kt_p, vt_p = _inproj_prompt(x_prompt, mod_p, n1, w_in_bf, qg, kg)
    n_first = _attn_stream_batches(b, s)
    assert 0 < n_first < bd
    shifted, o_s_first, o_att = _stream_attn(kt, vt, 0, n_first, qx, ks3, vs3, q, k, v)
    y_ssm, f_re, f_im = _ssm_prompt(u, toe, w_pair, v_pair, a16_re, a16_im)
    okt, ovt, o_s_second, y_prompt = _stream_post(
        kt, vt, shifted, n_first, bd - n_first, qx, ks3, vs3, x_prompt, o_att, y_ssm, mod_p, consts, weights)
    o_att_s = jnp.concatenate([o_s_first, o_s_second], axis=0)
    gn = SSM_GROUPS * SSM_STATE
    h0_re = jnp.transpose(state_ssm_re[L], (1, 2, 0)).reshape(gn, bd)
    h0_im = jnp.transpose(state_ssm_im[L], (1, 2, 0)).reshape(gn, bd)
    y_ssm_t, hs_re, hs_im = _ssm_sample(us.reshape(t_new, bd, SSM_WIDTH), h0_re, h0_im, tb,
                                        ssm_c_re[L], ssm_c_im[L], ssm_d[L], t_new)
    o_att_t = jnp.transpose(o_att_s, (1, 0, 2)).reshape(t_new * bd, ATT_WIDTH)
    y_sample = _post_sample(xs, o_att_t, y_ssm_t.reshape(t_new * bd, SSM_WIDTH), mod, bd, consts, weights)
    y_sample = jnp.transpose(y_sample.reshape(t_new, bd, D_MODEL), (1, 0, 2))

    def from_t(a, nb, keep):
        return jnp.transpose(a.reshape(nb, ATT_HEADS, HEAD_DIM, keep), (0, 3, 1, 2))[None]

    def state_from_t(a):
        return jnp.transpose(a.reshape(SSM_GROUPS, SSM_STATE, bd), (2, 0, 1))[None]

    keep = min(MAX_SPAN, s)
    return (y_prompt, y_sample,
            from_t(kt_p, b, keep), from_t(vt_p, b, keep),
            f_re.reshape(b, SSM_GROUPS, SSM_STATE)[None], f_im.reshape(b, SSM_GROUPS, SSM_STATE)[None],
            from_t(okt, bd, n_buf), from_t(ovt, bd, n_buf),
            state_from_t(hs_re), state_from_t(hs_im))
```
